```python
import math
import jax, jax.numpy as jnp
from jax import lax
import numpy as np

D_MODEL = 1024
BATCH = 16
SEQ = 2048
DEPTH = 1
DEC_BATCH = 128
DEC_SEQ = 8
PAST_LEN = 8192
PAGE_SIZE = 128

MIX_WIDTH = D_MODEL
ATT_WIDTH = MIX_WIDTH // 2
LRU_WIDTH = MIX_WIDTH - ATT_WIDTH
HEAD_DIM = 64
N_ATT_HEADS = ATT_WIDTH // HEAD_DIM
N_LRU_BLOCKS = 8
LRU_BLOCK = LRU_WIDTH // N_LRU_BLOCKS
CONV_WIDTH = 4
LRU_C = 8.0
D_FF = 4 * D_MODEL
DILATED_CONFIGS = ((128, 1), (512, 4), (2048, 16))
WIN_MAX = max(w for w, _ in DILATED_CONFIGS)
N_BUCKETS = 32
BUCKET_MAX_DIST = WIN_MAX
NORM_EPS = 1e-6
NEG_INF = -1e30
IN_COLS = 3 * ATT_WIDTH + 2 * LRU_WIDTH

kernel_name = "hymba_rglru_dilated_swa_step"


def rms_norm(x, g):
    x32 = x.astype(jnp.float32)
    y = x32 * lax.rsqrt(jnp.mean(x32 * x32, axis=-1, keepdims=True) + NORM_EPS)
    return (y * g.astype(jnp.float32)).astype(x.dtype)


def t5_bucket(dist):
    max_exact = N_BUCKETS // 2
    d_f = jnp.maximum(dist, max_exact).astype(jnp.float32)
    large = max_exact + (jnp.log(d_f / max_exact) / math.log(BUCKET_MAX_DIST / max_exact)
                         * (N_BUCKETS - max_exact)).astype(jnp.int32)
    large = jnp.minimum(large, N_BUCKETS - 1)
    return jnp.where(dist < max_exact, dist, large)


def branch_bias(rel_bias, dilation, span):
    dist = jnp.arange(span + 1, dtype=jnp.int32) * dilation
    return rel_bias[t5_bucket(dist)].astype(jnp.float32).T


def dilated_branch_prompt(q, k, v, bias, dilation, span):
    b, s, h, dh = q.shape
    L = s // dilation
    blk = span
    nb = -(-L // blk)
    lp = nb * blk

    def to_sub(t):
        t = t.reshape(b, L, dilation, h, dh).transpose(0, 2, 3, 1, 4)
        t = jnp.pad(t, ((0, 0), (0, 0), (0, 0), (0, lp - L), (0, 0)))
        return t.reshape(b, dilation, h, nb, blk, dh)

    def with_prev(t):
        prev = jnp.pad(t, ((0, 0), (0, 0), (0, 0), (1, 0), (0, 0), (0, 0)))[:, :, :, :-1]
        return jnp.concatenate([prev, t], axis=4)

    qb = to_sub(q)
    kk = with_prev(to_sub(k))
    vv = with_prev(to_sub(v))
    qi = jnp.arange(blk)[:, None]
    ki = jnp.arange(2 * blk)[None, :]
    rel = blk + qi - ki
    in_band = (rel >= 0) & (rel <= span)
    first_blk = (jnp.arange(nb)[:, None, None] == 0) & (ki[None] < blk)
    mask = in_band[None] & ~first_blk
    bias_m = bias[:, jnp.clip(rel, 0, span)]
    logits = jnp.einsum('bchnqd,bchnkd->bchnqk', qb, kk).astype(jnp.float32) * (dh ** -0.5)
    logits = logits + bias_m[None, None, :, None]
    logits = jnp.where(mask[None, None, None], logits, NEG_INF)
    m = jnp.max(logits, axis=-1, keepdims=True)
    p = jnp.exp(logits - m)
    den = jnp.sum(p, axis=-1, keepdims=True)
    o = jnp.einsum('bchnqk,bchnkd->bchnqd', p, vv.astype(jnp.float32)) / den
    lse = (m + jnp.log(den))[..., 0]
    o = o.reshape(b, dilation, h, lp, dh)[:, :, :, :L].transpose(0, 3, 1, 2, 4).reshape(b, s, h, dh)
    lse = lse.reshape(b, dilation, h, lp)[:, :, :, :L].transpose(0, 3, 1, 2).reshape(b, s, h)
    return o, lse


def dilated_branch_sample(q, k_all, v_all, bias, dilation, span, n_buf):
    t, dh = q.shape[1], q.shape[3]
    idx = n_buf + jnp.arange(t)[:, None] - dilation * jnp.arange(span + 1)[None, :]
    valid = idx >= 0
    idx = jnp.maximum(idx, 0)
    kg = k_all[:, idx]
    vg = v_all[:, idx]
    logits = jnp.einsum('bthd,btjhd->bthj', q, kg).astype(jnp.float32) * (dh ** -0.5)
    logits = logits + bias[None, None]
    logits = jnp.where(valid[None, :, None, :], logits, NEG_INF)
    m = jnp.max(logits, axis=-1, keepdims=True)
    p = jnp.exp(logits - m)
    den = jnp.sum(p, axis=-1, keepdims=True)
    o = jnp.einsum('bthj,btjhd->bthd', p, vg.astype(jnp.float32)) / den[..., 0][..., None]
    lse = (m + jnp.log(den))[..., 0]
    return o, lse


def combine_branches(outs, lses):
    w = jax.nn.softmax(jnp.stack(lses, axis=0), axis=0)
    return jnp.einsum('gbth,gbthd->bthd', w, jnp.stack(outs, axis=0))


def causal_conv(x, buf, w, b):
    t = x.shape[1]
    xp = jnp.concatenate([buf.astype(x.dtype), x], axis=1)
    y = b
    for j in range(CONV_WIDTH):
        y = y + w[j] * xp[:, j:j + t]
    return y, xp[:, xp.shape[1] - (CONV_WIDTH - 1):]


def rg_lru(xc, h0, wa, ba, wx, bx, lam):
    b, t, c = xc.shape
    x32 = xc.astype(jnp.float32)
    xb = x32.reshape(b, t, N_LRU_BLOCKS, LRU_BLOCK)
    r = jax.nn.sigmoid(jnp.einsum('btgi,gio->btgo', xb, wa) + ba).reshape(b, t, c)
    gi = jax.nn.sigmoid(jnp.einsum('btgi,gio->btgo', xb, wx) + bx).reshape(b, t, c)
    log_a = -LRU_C * r * jax.nn.softplus(-lam.astype(jnp.float32))
    a = jnp.exp(log_a)
    u = jnp.sqrt(-jnp.expm1(2.0 * log_a)) * (gi * x32)

    def step(h, au):
        a_t, u_t = au
        h = a_t * h + u_t
        return h, h

    h_last, hs = lax.scan(step, h0.astype(jnp.float32), (a.transpose(1, 0, 2), u.transpose(1, 0, 2)))
    return hs.transpose(1, 0, 2), h_last


def hybrid_layer(x, conv_buf, h0, k_buf, v_buf, norm1_g, w_in, rel_bias, conv_w, conv_b,
                 gate_a_w, gate_a_b, gate_x_w, gate_x_b, lru_lambda, att_out_g, rnn_out_g,
                 w_out, norm2_g, w_mlp_in, w_mlp_out):
    b, t, _ = x.shape
    n = rms_norm(x, norm1_g)
    proj = n @ w_in
    q, k, v, xr, gr = jnp.split(
        proj, [ATT_WIDTH, 2 * ATT_WIDTH, 3 * ATT_WIDTH, 3 * ATT_WIDTH + LRU_WIDTH], axis=-1)
    q = q.reshape(b, t, N_ATT_HEADS, HEAD_DIM)
    k = k.reshape(b, t, N_ATT_HEADS, HEAD_DIM)
    v = v.reshape(b, t, N_ATT_HEADS, HEAD_DIM)

    outs, lses = [], []
    if k_buf is None:
        for (win, dil) in DILATED_CONFIGS:
            span = win // dil
            o, l = dilated_branch_prompt(q, k, v, branch_bias(rel_bias, dil, span), dil, span)
            outs.append(o)
            lses.append(l)
        keep = min(WIN_MAX, t)
        new_k, new_v = k[:, t - keep:], v[:, t - keep:]
        conv_buf = jnp.zeros((b, CONV_WIDTH - 1, LRU_WIDTH), x.dtype)
        h0 = jnp.zeros((b, LRU_WIDTH), jnp.float32)
        state_dtype = x.dtype
    else:
        n_buf = k_buf.shape[1]
        k_all = jnp.concatenate([k_buf.astype(k.dtype), k], axis=1)
        v_all = jnp.concatenate([v_buf.astype(v.dtype), v], axis=1)
        for (win, dil) in DILATED_CONFIGS:
            span = win // dil
            o, l = dilated_branch_sample(q, k_all, v_all, branch_bias(rel_bias, dil, span), dil, span, n_buf)
            outs.append(o)
            lses.append(l)
        new_k, new_v = k, v
        state_dtype = h0.dtype
    att = combine_branches(outs, lses).astype(x.dtype).reshape(b, t, ATT_WIDTH)

    xc, new_conv = causal_conv(xr, conv_buf, conv_w, conv_b)
    hs, h_last = rg_lru(xc, h0, gate_a_w, gate_a_b, gate_x_w, gate_x_b, lru_lambda)
    rnn = (hs * jax.nn.gelu(gr.astype(jnp.float32))).astype(x.dtype)

    mixed = jnp.concatenate([rms_norm(att, att_out_g), rms_norm(rnn, rnn_out_g)], axis=-1)
    x = x + mixed @ w_out
    hmid = rms_norm(x, norm2_g) @ w_mlp_in
    x = x + jnp.square(jax.nn.relu(hmid)) @ w_mlp_out
    return x, new_k, new_v, new_conv, h_last.astype(state_dtype)


def setup_inputs(seed: int = 0) -> dict:
    key = jax.random.key(seed)
    ks = jax.random.split(key, 24)
    n_buf = min(WIN_MAX, PAST_LEN)

    def nrm(k, shape, scale):
        return jax.random.normal(k, shape, jnp.float32) * scale

    a0 = jax.random.uniform(ks[14], (DEPTH, LRU_WIDTH), jnp.float32, 0.9, 0.999)
    s0 = a0 ** (1.0 / LRU_C)
    lru_lambda = jnp.log(s0) - jnp.log1p(-s0)
    return {
        "x_prompt": nrm(ks[0], (BATCH, SEQ, D_MODEL), 1.0),
        "x_sample": nrm(ks[1], (DEC_BATCH, DEC_SEQ, D_MODEL), 1.0),
        "cache_k": nrm(ks[2], (DEPTH, DEC_BATCH, n_buf, N_ATT_HEADS, HEAD_DIM), 1.0),
        "cache_v": nrm(ks[3], (DEPTH, DEC_BATCH, n_buf, N_ATT_HEADS, HEAD_DIM), 1.0),
        "state_conv": nrm(ks[4], (DEPTH, DEC_BATCH, CONV_WIDTH - 1, LRU_WIDTH), 1.0),
        "state_h": nrm(ks[5], (DEPTH, DEC_BATCH, LRU_WIDTH), 0.5),
        "norm1_g": 1.0 + nrm(ks[6], (DEPTH, D_MODEL), 0.02),
        "w_in": nrm(ks[7], (DEPTH, D_MODEL, IN_COLS), D_MODEL ** -0.5),
        "rel_bias": nrm(ks[8], (N_BUCKETS, N_ATT_HEADS), 0.5),
        "conv_w": nrm(ks[9], (DEPTH, CONV_WIDTH, LRU_WIDTH), CONV_WIDTH ** -0.5),
        "conv_b": nrm(ks[10], (DEPTH, LRU_WIDTH), 0.02),
        "gate_a_w": nrm(ks[11], (DEPTH, N_LRU_BLOCKS, LRU_BLOCK, LRU_BLOCK), LRU_BLOCK ** -0.5),
        "gate_a_b": nrm(ks[12], (DEPTH, N_LRU_BLOCKS, LRU_BLOCK), 0.02),
        "gate_x_w": nrm(ks[13], (DEPTH, N_LRU_BLOCKS, LRU_BLOCK, LRU_BLOCK), LRU_BLOCK ** -0.5),
        "gate_x_b": nrm(ks[15], (DEPTH, N_LRU_BLOCKS, LRU_BLOCK), 0.02),
        "lru_lambda": lru_lambda,
        "att_out_g": 1.0 + nrm(ks[16], (DEPTH, ATT_WIDTH), 0.02),
        "rnn_out_g": 1.0 + nrm(ks[17], (DEPTH, LRU_WIDTH), 0.02),
        "w_out": nrm(ks[18], (DEPTH, MIX_WIDTH, D_MODEL), MIX_WIDTH ** -0.5),
        "norm2_g": 1.0 + nrm(ks[19], (DEPTH, D_MODEL), 0.02),
        "w_mlp_in": nrm(ks[20], (DEPTH, D_MODEL, D_FF), D_MODEL ** -0.5),
        "w_mlp_out": nrm(ks[21], (DEPTH, D_FF, D_MODEL), D_FF ** -0.5),
        "final_g": 1.0 + nrm(ks[22], (D_MODEL,), 0.02),
    }


def reference(x_prompt, x_sample, cache_k, cache_v, state_conv, state_h, norm1_g, w_in, rel_bias,
              conv_w, conv_b, gate_a_w, gate_a_b, gate_x_w, gate_x_b, lru_lambda, att_out_g,
              rnn_out_g, w_out, norm2_g, w_mlp_in, w_mlp_out, final_g):
    xp, xs = x_prompt, x_sample
    kp_l, vp_l, cp_l, hp_l = [], [], [], []
    ks_l, vs_l, cs_l, hs_l = [], [], [], []
    for l in range(DEPTH):
        w = (norm1_g[l], w_in[l], rel_bias, conv_w[l], conv_b[l], gate_a_w[l], gate_a_b[l],
             gate_x_w[l], gate_x_b[l], lru_lambda[l], att_out_g[l], rnn_out_g[l], w_out[l],
             norm2_g[l], w_mlp_in[l], w_mlp_out[l])
        xp, kp, vp, cp, hp = hybrid_layer(xp, None, None, None, None, *w)
        xs, kn, vn, cn, hn = hybrid_layer(xs, state_conv[l], state_h[l], cache_k[l], cache_v[l], *w)
        kp_l.append(kp)
        vp_l.append(vp)
        cp_l.append(cp)
        hp_l.append(hp)
        ks_l.append(kn)
        vs_l.append(vn)
        cs_l.append(cn)
        hs_l.append(hn)
    y_prompt = rms_norm(xp, final_g)
    y_sample = rms_norm(xs, final_g)
    return (y_prompt, y_sample,
            jnp.stack(kp_l), jnp.stack(vp_l), jnp.stack(cp_l), jnp.stack(hp_l),
            jnp.stack(ks_l), jnp.stack(vs_l), jnp.stack(cs_l), jnp.stack(hs_l))
```

```python
import functools
import math

import numpy as np
import jax
import jax.numpy as jnp
from jax import lax
from jax.experimental import pallas as pl
from jax.experimental.pallas import tpu as pltpu

F32 = jnp.float32
BF16 = jnp.bfloat16

D_MODEL = 1024
ATT_WIDTH = 512
LRU_WIDTH = 512
HEAD_DIM = 64
N_HEADS = 8
N_LRU_BLOCKS = 8
LRU_BLOCK = 64
CONV_WIDTH = 4
LRU_C = 8.0
D_FF = 4096
SPAN = 128
DILATIONS = (1, 4, 16)
WIN_MAX = 2048
N_BUCKETS = 32
NORM_EPS = 1e-6
NEG_INF = -1e30
Q_SCALE = HEAD_DIM ** -0.5

SUBLANES = 8
LANES = 128
HEADS_PER_TILE = LANES // HEAD_DIM
VMEM_LIMIT = 56 * 1024 * 1024

TOKEN_TILE = 512
FF_CHUNK = 1024
SAMPLE_PAD = 128


def _dot(a, b):
    return jnp.dot(a, b, preferred_element_type=F32)


def _dot_nt(a, b):
    return lax.dot_general(a, b, (((1,), (1,)), ((), ())), preferred_element_type=F32)


def _rms(x, g):
    return x * lax.rsqrt(jnp.mean(x * x, axis=-1, keepdims=True) + NORM_EPS) * g


def _const_spec(shape):
    nd = len(shape)
    return pl.BlockSpec(shape, lambda *_: (0,) * nd, pipeline_mode=pl.Buffered(1))


def _lru_gates(xc, wg_ref, ba_ref, bx_ref, lam_ref, a_sc, u_sc):
    xcb = xc.astype(BF16)
    lam = lam_ref[...]
    softplus_neg_lam = jnp.maximum(-lam, 0.0) + jnp.log1p(jnp.exp(-jnp.abs(lam)))
    for p in range(LRU_WIDTH // LANES):
        cols = slice(p * LANES, (p + 1) * LANES)
        g = _dot(xcb[:, cols], wg_ref[p])
        r = jax.nn.sigmoid(g[:, :LANES] + ba_ref[:, cols])
        gi = jax.nn.sigmoid(g[:, LANES:] + bx_ref[:, cols])
        log_a = (-LRU_C) * r * softplus_neg_lam[:, cols]
        a = jnp.exp(log_a)
        a_sc[:, cols] = a
        u_sc[:, cols] = jnp.sqrt(-jnp.tanh(log_a) * (a * a + 1.0)) * (gi * xc[:, cols])


def _group_scan(a, u, row):
    for s in (1, 2, 4):
        keep = row >= s
        u = jnp.where(keep, a * pltpu.roll(u, s, 0) + u, u)
        a = jnp.where(keep, a * pltpu.roll(a, s, 0), a)
    return a, u


def _prompt_mix_kernel(x_ref, g1_ref, win_ref, cw_ref, cb_ref, wg_ref, ba_ref, bx_ref, lam_ref, grnn_ref,
                       q_ref, k_ref, v_ref, rnn_ref, conv_ref, h_ref,
                       xr_ext, a_sc, u_sc, hs_sc, h_carry):
    t = x_ref.shape[0]

    @pl.when(pl.program_id(1) == 0)
    def _():
        xr_ext[0:SUBLANES, :] = jnp.zeros((SUBLANES, LRU_WIDTH), F32)
        h_carry[...] = jnp.zeros_like(h_carry)

    n = _rms(x_ref[...], g1_ref[...]).astype(BF16)
    q_ref[...] = _dot(n, win_ref[:, 0:ATT_WIDTH])
    k_ref[...] = _dot(n, win_ref[:, ATT_WIDTH:2 * ATT_WIDTH])
    v_ref[...] = _dot(n, win_ref[:, 2 * ATT_WIDTH:3 * ATT_WIDTH])
    xr = _dot(n, win_ref[:, 3 * ATT_WIDTH:3 * ATT_WIDTH + LRU_WIDTH])

    xr_ext[SUBLANES:SUBLANES + t, :] = xr
    xc = cb_ref[...] + cw_ref[CONV_WIDTH - 1:CONV_WIDTH, :] * xr
    for back in range(1, CONV_WIDTH):
        xc = xc + cw_ref[CONV_WIDTH - 1 - back:CONV_WIDTH - back, :] * xr_ext[pl.ds(SUBLANES - back, t), :]
    xr_ext[0:SUBLANES, :] = xr[t - SUBLANES:, :]
    conv_ref[...] = xr[t - (CONV_WIDTH - 1):, :]

    _lru_gates(xc, wg_ref, ba_ref, bx_ref, lam_ref, a_sc, u_sc)

    row = lax.broadcasted_iota(jnp.int32, (SUBLANES, LRU_WIDTH), 0)

    def scan_group(g, h):
        rows = pl.ds(pl.multiple_of(g * SUBLANES, SUBLANES), SUBLANES)
        a, u = _group_scan(a_sc[rows, :], u_sc[rows, :], row)
        hs = a * h + u
        hs_sc[rows, :] = hs
        return jnp.broadcast_to(hs[SUBLANES - 1:SUBLANES, :], hs.shape)

    h_last = lax.fori_loop(0, t // SUBLANES, scan_group, h_carry[...], unroll=2)
    h_carry[...] = h_last
    h_ref[...] = h_last[0:1, :]

    gr = _dot(n, win_ref[:, 3 * ATT_WIDTH + LRU_WIDTH:])
    rnn_ref[...] = _rms(hs_sc[...] * jax.nn.gelu(gr), grnn_ref[...]).astype(BF16)


def _sample_mix_kernel(x_ref, st_ref, h0_ref, g1_ref, win_ref, cw_ref, cb_ref, wg_ref, ba_ref, bx_ref, lam_ref,
                       grnn_ref, q_ref, k_ref, v_ref, rnn_ref, conv_ref, h_ref,
                       xr_sc, xc_sc, a_sc, u_sc, hs_sc):
    t = x_ref.shape[0]
    n_seq = t // SUBLANES
    n = _rms(x_ref[...], g1_ref[...]).astype(BF16)
    q_ref[...] = _dot(n, win_ref[:, 0:ATT_WIDTH])
    k_ref[...] = _dot(n, win_ref[:, ATT_WIDTH:2 * ATT_WIDTH])
    v_ref[...] = _dot(n, win_ref[:, 2 * ATT_WIDTH:3 * ATT_WIDTH])
    xr_sc[...] = _dot(n, win_ref[:, 3 * ATT_WIDTH:3 * ATT_WIDTH + LRU_WIDTH])

    row = lax.broadcasted_iota(jnp.int32, (SUBLANES, LRU_WIDTH), 0)
    n_state = CONV_WIDTH - 1

    def conv_group(b, carry):
        rows = pl.ds(pl.multiple_of(b * SUBLANES, SUBLANES), SUBLANES)
        xg = xr_sc[rows, :]
        sg = st_ref[rows, :]
        xc = cb_ref[...] + cw_ref[CONV_WIDTH - 1:CONV_WIDTH, :] * xg
        for back in range(1, CONV_WIDTH):
            prev = jnp.where(row >= back, pltpu.roll(xg, back, 0),
                             pltpu.roll(sg, (back - n_state) % SUBLANES, 0))
            xc = xc + cw_ref[CONV_WIDTH - 1 - back:CONV_WIDTH - back, :] * prev
        xc_sc[rows, :] = xc
        conv_ref[b] = xg[SUBLANES - n_state:, :]
        return carry

    lax.fori_loop(0, n_seq, conv_group, 0)

    _lru_gates(xc_sc[...], wg_ref, ba_ref, bx_ref, lam_ref, a_sc, u_sc)

    def scan_group(b, carry):
        rows = pl.ds(pl.multiple_of(b * SUBLANES, SUBLANES), SUBLANES)
        a, u = _group_scan(a_sc[rows, :], u_sc[rows, :], row)
        hs = a * h0_ref[pl.ds(b, 1), :] + u
        hs_sc[rows, :] = hs
        h_ref[pl.ds(b, 1), :] = hs[SUBLANES - 1:SUBLANES, :]
        return carry

    lax.fori_loop(0, n_seq, scan_group, 0)

    gr = _dot(n, win_ref[:, 3 * ATT_WIDTH + LRU_WIDTH:])
    rnn_ref[...] = _rms(hs_sc[...] * jax.nn.gelu(gr), grnn_ref[...]).astype(BF16)


def _mix_weight_specs():
    return [
        _const_spec((1, D_MODEL)),
        _const_spec((D_MODEL, 3 * ATT_WIDTH + 2 * LRU_WIDTH)),
        _const_spec((CONV_WIDTH, LRU_WIDTH)),
        _const_spec((1, LRU_WIDTH)),
        _const_spec((LRU_WIDTH // LANES, LANES, 2 * LANES)),
        _const_spec((1, LRU_WIDTH)),
        _const_spec((1, LRU_WIDTH)),
        _const_spec((1, LRU_WIDTH)),
        _const_spec((1, LRU_WIDTH)),
    ]


def _prompt_mix(x, weights):
    b, s, _ = x.shape
    t = TOKEN_TILE
    tok = lambda width: pl.BlockSpec((None, t, width), lambda i, j: (i, j, 0))
    per_seq = lambda rows: pl.BlockSpec((None, rows, LRU_WIDTH), lambda i, j: (i, 0, 0))
    return pl.pallas_call(
        _prompt_mix_kernel,
        grid=(b, s // t),
        in_specs=[tok(D_MODEL)] + _mix_weight_specs(),
        out_specs=[tok(ATT_WIDTH), tok(ATT_WIDTH), tok(ATT_WIDTH), tok(LRU_WIDTH),
                   per_seq(CONV_WIDTH - 1), per_seq(1)],
        out_shape=[jax.ShapeDtypeStruct((b, s, ATT_WIDTH), F32)] * 3
        + [jax.ShapeDtypeStruct((b, s, LRU_WIDTH), BF16),
           jax.ShapeDtypeStruct((b, CONV_WIDTH - 1, LRU_WIDTH), F32),
           jax.ShapeDtypeStruct((b, 1, LRU_WIDTH), F32)],
        scratch_shapes=[pltpu.VMEM((t + SUBLANES, LRU_WIDTH), F32)] + [pltpu.VMEM((t, LRU_WIDTH), F32)] * 3
        + [pltpu.VMEM((SUBLANES, LRU_WIDTH), F32)],
        compiler_params=pltpu.CompilerParams(dimension_semantics=("arbitrary", "arbitrary"),
                                             vmem_limit_bytes=VMEM_LIMIT),
        name="prompt_mix",
    )(x, *weights)


def _sample_mix(x, state_rows, h0, weights):
    rows = x.shape[0]
    t = min(TOKEN_TILE, rows)
    n_seq = t // SUBLANES
    tok = lambda width: pl.BlockSpec((t, width), lambda i: (i, 0))
    return pl.pallas_call(
        _sample_mix_kernel,
        grid=(rows // t,),
        in_specs=[tok(D_MODEL), tok(LRU_WIDTH), pl.BlockSpec((n_seq, LRU_WIDTH), lambda i: (i, 0))]
        + _mix_weight_specs(),
        out_specs=[tok(ATT_WIDTH), tok(ATT_WIDTH), tok(ATT_WIDTH), tok(LRU_WIDTH),
                   pl.BlockSpec((n_seq, CONV_WIDTH - 1, LRU_WIDTH), lambda i: (i, 0, 0)),
                   pl.BlockSpec((n_seq, LRU_WIDTH), lambda i: (i, 0))],
        out_shape=[jax.ShapeDtypeStruct((rows, ATT_WIDTH), F32)] * 3
        + [jax.ShapeDtypeStruct((rows, LRU_WIDTH), BF16),
           jax.ShapeDtypeStruct((rows // SUBLANES, CONV_WIDTH - 1, LRU_WIDTH), F32),
           jax.ShapeDtypeStruct((rows // SUBLANES, LRU_WIDTH), F32)],
        scratch_shapes=[pltpu.VMEM((t, LRU_WIDTH), F32)] * 5,
        compiler_params=pltpu.CompilerParams(dimension_semantics=("arbitrary",), vmem_limit_bytes=VMEM_LIMIT),
        name="sample_mix",
    )(x, state_rows, h0, *weights)


def _prompt_attn_kernel(q_ref, k_ref, v_ref, bm_ref, o_ref, pv_sc, m_sc, den_sc):
    s_len = q_ref.shape[0]
    low_head = lax.broadcasted_iota(jnp.int32, (SPAN, LANES), 1) < HEAD_DIM

    def block(branch, q_rows, k_rows, first):
        q = q_ref[q_rows, :] * Q_SCALE
        kk = k_ref[k_rows, :].astype(BF16)
        vv = v_ref[k_rows, :].astype(BF16)
        parts = []
        for h in range(HEADS_PER_TILE):
            qh = jnp.where(low_head if h == 0 else ~low_head, q, 0.0).astype(BF16)
            table = bm_ref[branch, h, :, SPAN:] if first else bm_ref[branch, h]
            logits = _dot_nt(qh, kk) + table
            m = jnp.max(logits, axis=-1, keepdims=True)
            p = jnp.exp(logits - m)
            den = jnp.sum(p, axis=-1, keepdims=True)
            pv = _dot(p.astype(BF16), vv)
            parts.append((pv, jnp.broadcast_to(m, pv.shape), jnp.broadcast_to(den, pv.shape)))
        pv_sc[branch, q_rows, :] = jnp.where(low_head, parts[0][0], parts[1][0])
        m_sc[branch, q_rows, :] = jnp.where(low_head, parts[0][1], parts[1][1])
        den_sc[branch, q_rows, :] = jnp.where(low_head, parts[0][2], parts[1][2])

    for branch, dil in enumerate(DILATIONS):
        sub_len = s_len // dil
        n_blocks = sub_len // SPAN
        if n_blocks == 1:
            def per_class(c, carry, branch=branch, dil=dil):
                rows = pl.ds(c, SPAN, stride=dil)
                block(branch, rows, rows, True)
                return carry
            lax.fori_loop(0, dil, per_class, 0)
            continue
        for c in range(dil):
            first_rows = pl.ds(c, SPAN, stride=dil) if dil > 1 else pl.ds(0, SPAN)
            block(branch, first_rows, first_rows, True)

            def per_block(i, carry, branch=branch, dil=dil, c=c):
                q0 = c + i * (SPAN * dil)
                k0 = q0 - SPAN * dil
                if dil > 1:
                    block(branch, pl.ds(q0, SPAN, stride=dil), pl.ds(k0, 2 * SPAN, stride=dil), False)
                else:
                    block(branch, pl.ds(pl.multiple_of(q0, SPAN), SPAN), pl.ds(pl.multiple_of(k0, SPAN), 2 * SPAN),
                          False)
                return carry
            lax.fori_loop(1, n_blocks, per_block, 0)

    chunk = 2 * SPAN

    def merge(i, carry):
        rows = pl.ds(pl.multiple_of(i * chunk, chunk), chunk)
        ms = [m_sc[g, rows, :] for g in range(len(DILATIONS))]
        m_all = jnp.maximum(jnp.maximum(ms[0], ms[1]), ms[2])
        num = jnp.zeros((chunk, LANES), F32)
        den = jnp.zeros((chunk, LANES), F32)
        for g in range(len(DILATIONS)):
            w = jnp.exp(ms[g] - m_all)
            num = num + w * pv_sc[g, rows, :]
            den = den + w * den_sc[g, rows, :]
        o_ref[rows, :] = num / den
        return carry

    lax.fori_loop(0, s_len // chunk, merge, 0)


def _prompt_attn(q, k, v, bias_tables):
    b, s, _ = q.shape
    n_branch = len(DILATIONS)
    head_pair = pl.BlockSpec((None, s, LANES), lambda i, j: (i, 0, j))
    return pl.pallas_call(
        _prompt_attn_kernel,
        grid=(b, ATT_WIDTH // LANES),
        in_specs=[head_pair, head_pair, head_pair,
                  pl.BlockSpec((n_branch, HEADS_PER_TILE, SPAN, 2 * SPAN), lambda i, j: (0, j, 0, 0))],
        out_specs=head_pair,
        out_shape=jax.ShapeDtypeStruct((b, s, ATT_WIDTH), F32),
        scratch_shapes=[pltpu.VMEM((n_branch, s, LANES), F32)] * 3,
        compiler_params=pltpu.CompilerParams(dimension_semantics=("arbitrary", "arbitrary"),
                                             vmem_limit_bytes=VMEM_LIMIT),
        name="prompt_attn",
    )(q, k, v, bias_tables)


def _sample_attn_kernel(q_ref, kn_ref, vn_ref, ck_ref, cv_ref, bm_ref, o_ref, k_all, v_all):
    n_buf = ck_ref.shape[0]
    t = q_ref.shape[0]
    head_of_lane = lax.broadcasted_iota(jnp.int32, (t, ATT_WIDTH), 1) // HEAD_DIM
    q = q_ref[...] * Q_SCALE
    q_bd = jnp.concatenate([jnp.where(head_of_lane == h, q, 0.0) for h in range(N_HEADS)], axis=0).astype(BF16)

    chunk = 512
    for c in range(n_buf // chunk):
        rows = slice(c * chunk, (c + 1) * chunk)
        k_all[rows, :] = ck_ref[rows, :].astype(BF16)
        v_all[rows, :] = cv_ref[rows, :].astype(BF16)
    pad = jnp.zeros((SAMPLE_PAD - t, ATT_WIDTH), F32)
    k_all[n_buf:, :] = jnp.concatenate([kn_ref[...], pad], axis=0).astype(BF16)
    v_all[n_buf:, :] = jnp.concatenate([vn_ref[...], pad], axis=0).astype(BF16)

    scores = _dot_nt(q_bd, k_all[...])
    n_branch = len(DILATIONS)
    m = jnp.max(scores + bm_ref[0], axis=-1, keepdims=True)
    for g in range(1, n_branch):
        m = jnp.maximum(m, jnp.max(scores + bm_ref[g], axis=-1, keepdims=True))
    p = jnp.exp(scores + bm_ref[0] - m)
    for g in range(1, n_branch):
        p = p + jnp.exp(scores + bm_ref[g] - m)
    den = jnp.sum(p, axis=-1, keepdims=True)
    out = _dot(p.astype(BF16), v_all[...]) / den
    att = jnp.zeros((t, ATT_WIDTH), F32)
    for h in range(N_HEADS):
        att = att + jnp.where(head_of_lane == h, out[h * t:(h + 1) * t, :], 0.0)
    o_ref[...] = att


def _sample_attn(q, k_new, v_new, cache_k, cache_v, bias_tables, steps):
    rows = q.shape[0]
    n_seq = rows // steps
    n_buf = cache_k.shape[1]
    tok = pl.BlockSpec((steps, ATT_WIDTH), lambda i: (i, 0))
    cache = pl.BlockSpec((None, n_buf, ATT_WIDTH), lambda i: (i, 0, 0))
    return pl.pallas_call(
        _sample_attn_kernel,
        grid=(n_seq,),
        in_specs=[tok, tok, tok, cache, cache, _const_spec(bias_tables.shape)],
        out_specs=tok,
        out_shape=jax.ShapeDtypeStruct((rows, ATT_WIDTH), F32),
        scratch_shapes=[pltpu.VMEM((n_buf + SAMPLE_PAD, ATT_WIDTH), BF16)] * 2,
        compiler_params=pltpu.CompilerParams(dimension_semantics=("arbitrary",), vmem_limit_bytes=VMEM_LIMIT),
        name="sample_attn",
    )(q, k_new, v_new, cache_k, cache_v, bias_tables)


def _dense_kernel(x_ref, att_ref, rnn_ref, gatt_ref, woa_ref, wor_ref, g2_ref, w1_ref, w2_ref, gf_ref, y_ref):
    att_n = _rms(att_ref[...], gatt_ref[...]).astype(BF16)
    x1 = x_ref[...] + _dot(att_n, woa_ref[...]) + _dot(rnn_ref[...], wor_ref[...])
    n2 = _rms(x1, g2_ref[...]).astype(BF16)
    mlp = None
    for c in range(D_FF // FF_CHUNK):
        cols = slice(c * FF_CHUNK, (c + 1) * FF_CHUNK)
        hmid = jnp.maximum(_dot(n2, w1_ref[:, cols]), 0.0)
        part = _dot((hmid * hmid).astype(BF16), w2_ref[cols, :])
        mlp = part if mlp is None else mlp + part
    y_ref[...] = _rms(x1 + mlp, gf_ref[...])


def _dense(x, att, rnn, weights):
    rows = x.shape[0]
    t = min(TOKEN_TILE, rows)
    tok = lambda width: pl.BlockSpec((t, width), lambda i: (i, 0))
    return pl.pallas_call(
        _dense_kernel,
        grid=(rows // t,),
        in_specs=[tok(D_MODEL), tok(ATT_WIDTH), tok(LRU_WIDTH),
                  _const_spec((1, ATT_WIDTH)), _const_spec((ATT_WIDTH, D_MODEL)), _const_spec((LRU_WIDTH, D_MODEL)),
                  _const_spec((1, D_MODEL)), _const_spec((D_MODEL, D_FF)), _const_spec((D_FF, D_MODEL)),
                  _const_spec((1, D_MODEL))],
        out_specs=tok(D_MODEL),
        out_shape=jax.ShapeDtypeStruct((rows, D_MODEL), F32),
        compiler_params=pltpu.CompilerParams(dimension_semantics=("arbitrary",), vmem_limit_bytes=VMEM_LIMIT),
        name="dense_tail",
    )(x, att, rnn, *weights)


def _t5_bucket(dist):
    max_exact = N_BUCKETS // 2
    d_f = jnp.maximum(dist, max_exact).astype(F32)
    large = max_exact + (jnp.log(d_f / max_exact) / math.log(WIN_MAX / max_exact)
                         * (N_BUCKETS - max_exact)).astype(jnp.int32)
    large = jnp.minimum(large, N_BUCKETS - 1)
    return jnp.where(dist < max_exact, dist, large)


def _branch_bias(rel_bias, dil):
    dist = jnp.arange(SPAN + 1, dtype=jnp.int32) * dil
    return rel_bias[_t5_bucket(dist)].astype(F32).T


def _prompt_bias_tables(rel_bias):
    rel = SPAN + np.arange(SPAN)[:, None] - np.arange(2 * SPAN)[None, :]
    in_band = (rel >= 0) & (rel <= SPAN)
    idx = np.clip(rel, 0, SPAN)
    return jnp.stack([jnp.where(in_band[None], _branch_bias(rel_bias, dil)[:, idx], NEG_INF) for dil in DILATIONS])


def _sample_bias_tables(rel_bias, n_buf, steps):
    key_pos = np.arange(n_buf + SAMPLE_PAD)
    delta = n_buf + np.arange(steps)[:, None] - key_pos[None, :]
    real = key_pos[None, :] < n_buf + steps
    tables = []
    for dil in DILATIONS:
        valid = real & (delta >= 0) & (delta % dil == 0) & (delta // dil <= SPAN)
        idx = np.clip(delta // dil, 0, SPAN)
        tab = jnp.where(valid[None], _branch_bias(rel_bias, dil)[:, idx], NEG_INF)
        tables.append(tab.reshape(N_HEADS * steps, n_buf + SAMPLE_PAD))
    return jnp.stack(tables)


def _paired_gate_weights(gate_a_w, gate_x_w):
    def pair_diag(w):
        z = jnp.zeros((LRU_BLOCK, LRU_BLOCK), w.dtype)
        return jnp.stack([jnp.block([[w[2 * p], z], [z, w[2 * p + 1]]]) for p in range(N_LRU_BLOCKS // 2)])
    return jnp.concatenate([pair_diag(gate_a_w), pair_diag(gate_x_w)], axis=-1).astype(BF16)


def kernel(x_prompt, x_sample, cache_k, cache_v, state_conv, state_h, norm1_g, w_in, rel_bias, conv_w, conv_b,
           gate_a_w, gate_a_b, gate_x_w, gate_x_b, lru_lambda, att_out_g, rnn_out_g, w_out, norm2_g, w_mlp_in,
           w_mlp_out, final_g):
    depth = w_in.shape[0]
    assert depth == 1, "the final norm is fused into the single layer's dense kernel"
    b, s, _ = x_prompt.shape
    db, steps, _ = x_sample.shape
    n_buf = cache_k.shape[2]
    assert steps == SUBLANES and s == WIN_MAX and n_buf == WIN_MAX
    l = 0
    row = lambda v: v.reshape(1, -1).astype(F32)

    mix_w = (row(norm1_g[l]), w_in[l].astype(BF16), conv_w[l], row(conv_b[l]),
             _paired_gate_weights(gate_a_w[l], gate_x_w[l]), row(gate_a_b[l]), row(gate_x_b[l]),
             row(lru_lambda[l]), row(rnn_out_g[l]))
    dense_w = (row(att_out_g[l]), w_out[l, :ATT_WIDTH].astype(BF16), w_out[l, ATT_WIDTH:].astype(BF16),
               row(norm2_g[l]), w_mlp_in[l].astype(BF16), w_mlp_out[l].astype(BF16), row(final_g))

    q, k, v, rnn, conv_p, h_p = _prompt_mix(x_prompt, mix_w)
    att = _prompt_attn(q, k, v, _prompt_bias_tables(rel_bias))
    y_prompt = _dense(x_prompt.reshape(b * s, D_MODEL), att.reshape(b * s, ATT_WIDTH),
                      rnn.reshape(b * s, LRU_WIDTH), dense_w).reshape(b, s, D_MODEL)

    xs = x_sample.reshape(db * steps, D_MODEL)
    state_rows = jnp.pad(state_conv[l], ((0, 0), (0, SUBLANES - (CONV_WIDTH - 1)), (0, 0))).reshape(db * steps, LRU_WIDTH)
    qs, ks, vs, rnn_s, conv_s, h_s = _sample_mix(xs, state_rows, state_h[l], mix_w)
    att_s = _sample_attn(qs, ks, vs, cache_k[l].reshape(db, n_buf, ATT_WIDTH), cache_v[l].reshape(db, n_buf, ATT_WIDTH),
                         _sample_bias_tables(rel_bias, n_buf, steps), steps)
    y_sample = _dense(xs, att_s, rnn_s, dense_w).reshape(db, steps, D_MODEL)

    kv_p = (1, b, s, N_HEADS, HEAD_DIM)
    kv_s = (1, db, steps, N_HEADS, HEAD_DIM)
    return (y_prompt, y_sample,
            k.reshape(kv_p), v.reshape(kv_p), conv_p[None], h_p.reshape(1, b, LRU_WIDTH),
            ks.reshape(kv_s), vs.reshape(kv_s), conv_s[None], h_s[None])
```

```python
import functools
import math

import numpy as np
import jax
import jax.numpy as jnp
from jax import lax
from jax.experimental import pallas as pl
from jax.experimental.pallas import tpu as pltpu

F32 = jnp.float32
BF16 = jnp.bfloat16

D_MODEL = 1024
ATT_WIDTH = 512
LRU_WIDTH = 512
HEAD_DIM = 64
N_HEADS = 8
N_LRU_BLOCKS = 8
LRU_BLOCK = 64
CONV_WIDTH = 4
LRU_C = 8.0
D_FF = 4096
SPAN = 128
DILATIONS = (1, 4, 16)
WIN_MAX = 2048
N_BUCKETS = 32
NORM_EPS = 1e-6
NEG_INF = -1e30
Q_SCALE = HEAD_DIM ** -0.5

SUBLANES = 8
LANES = 128
HEADS_PER_TILE = LANES // HEAD_DIM
VMEM_LIMIT = 56 * 1024 * 1024

TOKEN_TILE = 512
FF_CHUNK = 1024
SAMPLE_PAD = 128


def _dot(a, b):
    return jnp.dot(a, b, preferred_element_type=F32)


def _dot_nt(a, b):
    return lax.dot_general(a, b, (((1,), (1,)), ((), ())), preferred_element_type=F32)


def _rms(x, g):
    return x * lax.rsqrt(jnp.mean(x * x, axis=-1, keepdims=True) + NORM_EPS) * g


def _const_spec(shape):
    nd = len(shape)
    return pl.BlockSpec(shape, lambda *_: (0,) * nd, pipeline_mode=pl.Buffered(1))


def _lru_gates(xc, wg_ref, ba_ref, bx_ref, lam_ref, a_sc, u_sc):
    xcb = xc.astype(BF16)
    lam = lam_ref[...]
    softplus_neg_lam = jnp.maximum(-lam, 0.0) + jnp.log1p(jnp.exp(-jnp.abs(lam)))
    for p in range(LRU_WIDTH // LANES):
        cols = slice(p * LANES, (p + 1) * LANES)
        g = _dot(xcb[:, cols], wg_ref[p])
        r = jax.nn.sigmoid(g[:, :LANES] + ba_ref[:, cols])
        gi = jax.nn.sigmoid(g[:, LANES:] + bx_ref[:, cols])
        log_a = (-LRU_C) * r * softplus_neg_lam[:, cols]
        a = jnp.exp(log_a)
        a_sc[:, cols] = a
        u_sc[:, cols] = jnp.sqrt(-jnp.tanh(log_a) * (a * a + 1.0)) * (gi * xc[:, cols])


def _group_scan(a, u, row):
    for s in (1, 2, 4):
        keep = row >= s
        u = jnp.where(keep, a * pltpu.roll(u, s, 0) + u, u)
        a = jnp.where(keep, a * pltpu.roll(a, s, 0), a)
    return a, u


def _prompt_mix_kernel(x_ref, g1_ref, win_ref, cw_ref, cb_ref, wg_ref, ba_ref, bx_ref, lam_ref, grnn_ref,
                       q_ref, k_ref, v_ref, rnn_ref, conv_ref, h_ref,
                       xr_ext, a_sc, u_sc, hs_sc, h_carry):
    t = x_ref.shape[0]

    @pl.when(pl.program_id(1) == 0)
    def _():
        xr_ext[0:SUBLANES, :] = jnp.zeros((SUBLANES, LRU_WIDTH), F32)
        h_carry[...] = jnp.zeros_like(h_carry)

    n = _rms(x_ref[...], g1_ref[...]).astype(BF16)
    q_ref[...] = _dot(n, win_ref[:, 0:ATT_WIDTH])
    k_ref[...] = _dot(n, win_ref[:, ATT_WIDTH:2 * ATT_WIDTH])
    v_ref[...] = _dot(n, win_ref[:, 2 * ATT_WIDTH:3 * ATT_WIDTH])
    xr = _dot(n, win_ref[:, 3 * ATT_WIDTH:3 * ATT_WIDTH + LRU_WIDTH])

    xr_ext[SUBLANES:SUBLANES + t, :] = xr
    xc = cb_ref[...] + cw_ref[CONV_WIDTH - 1:CONV_WIDTH, :] * xr
    for back in range(1, CONV_WIDTH):
        xc = xc + cw_ref[CONV_WIDTH - 1 - back:CONV_WIDTH - back, :] * xr_ext[pl.ds(SUBLANES - back, t), :]
    xr_ext[0:SUBLANES, :] = xr[t - SUBLANES:, :]
    conv_ref[...] = xr[t - (CONV_WIDTH - 1):, :]

    _lru_gates(xc, wg_ref, ba_ref, bx_ref, lam_ref, a_sc, u_sc)

    row = lax.broadcasted_iota(jnp.int32, (SUBLANES, LRU_WIDTH), 0)

    def scan_group(g, h):
        rows = pl.ds(pl.multiple_of(g * SUBLANES, SUBLANES), SUBLANES)
        a, u = _group_scan(a_sc[rows, :], u_sc[rows, :], row)
        hs = a * h + u
        hs_sc[rows, :] = hs
        return jnp.broadcast_to(hs[SUBLANES - 1:SUBLANES, :], hs.shape)

    h_last = lax.fori_loop(0, t // SUBLANES, scan_group, h_carry[...], unroll=2)
    h_carry[...] = h_last
    h_ref[...] = h_last[0:1, :]

    gr = _dot(n, win_ref[:, 3 * ATT_WIDTH + LRU_WIDTH:])
    rnn_ref[...] = _rms(hs_sc[...] * jax.nn.gelu(gr), grnn_ref[...]).astype(BF16)


def _sample_mix_kernel(x_ref, st_ref, h0_ref, g1_ref, win_ref, cw_ref, cb_ref, wg_ref, ba_ref, bx_ref, lam_ref,
                       grnn_ref, q_ref, k_ref, v_ref, rnn_ref, conv_ref, h_ref,
                       xr_sc, xc_sc, a_sc, u_sc, hs_sc):
    t = x_ref.shape[0]
    n_seq = t // SUBLANES
    n = _rms(x_ref[...], g1_ref[...]).astype(BF16)
    q_ref[...] = _dot(n, win_ref[:, 0:ATT_WIDTH])
    k_ref[...] = _dot(n, win_ref[:, ATT_WIDTH:2 * ATT_WIDTH])
    v_ref[...] = _dot(n, win_ref[:, 2 * ATT_WIDTH:3 * ATT_WIDTH])
    xr_sc[...] = _dot(n, win_ref[:, 3 * ATT_WIDTH:3 * ATT_WIDTH + LRU_WIDTH])

    row = lax.broadcasted_iota(jnp.int32, (SUBLANES, LRU_WIDTH), 0)
    n_state = CONV_WIDTH - 1

    def conv_group(b, carry):
        rows = pl.ds(pl.multiple_of(b * SUBLANES, SUBLANES), SUBLANES)
        xg = xr_sc[rows, :]
        sg = st_ref[rows, :]
        xc = cb_ref[...] + cw_ref[CONV_WIDTH - 1:CONV_WIDTH, :] * xg
        for back in range(1, CONV_WIDTH):
            prev = jnp.where(row >= back, pltpu.roll(xg, back, 0),
                             pltpu.roll(sg, (back - n_state) % SUBLANES, 0))
            xc = xc + cw_ref[CONV_WIDTH - 1 - back:CONV_WIDTH - back, :] * prev
        xc_sc[rows, :] = xc
        conv_ref[b] = xg[SUBLANES - n_state:, :]
        return carry

    lax.fori_loop(0, n_seq, conv_group, 0)

    _lru_gates(xc_sc[...], wg_ref, ba_ref, bx_ref, lam_ref, a_sc, u_sc)

    def scan_group(b, carry):
        rows = pl.ds(pl.multiple_of(b * SUBLANES, SUBLANES), SUBLANES)
        a, u = _group_scan(a_sc[rows, :], u_sc[rows, :], row)
        hs = a * h0_ref[pl.ds(b, 1), :] + u
        hs_sc[rows, :] = hs
        h_ref[pl.ds(b, 1), :] = hs[SUBLANES - 1:SUBLANES, :]
        return carry

    lax.fori_loop(0, n_seq, scan_group, 0)

    gr = _dot(n, win_ref[:, 3 * ATT_WIDTH + LRU_WIDTH:])
    rnn_ref[...] = _rms(hs_sc[...] * jax.nn.gelu(gr), grnn_ref[...]).astype(BF16)


def _mix_weight_specs():
    return [
        _const_spec((1, D_MODEL)),
        _const_spec((D_MODEL, 3 * ATT_WIDTH + 2 * LRU_WIDTH)),
        _const_spec((CONV_WIDTH, LRU_WIDTH)),
        _const_spec((1, LRU_WIDTH)),
        _const_spec((LRU_WIDTH // LANES, LANES, 2 * LANES)),
        _const_spec((1, LRU_WIDTH)),
        _const_spec((1, LRU_WIDTH)),
        _const_spec((1, LRU_WIDTH)),
        _const_spec((1, LRU_WIDTH)),
    ]


def _prompt_mix(x, weights):
    b, s, _ = x.shape
    t = TOKEN_TILE
    tok = lambda width: pl.BlockSpec((None, t, width), lambda i, j: (i, j, 0))
    per_seq = lambda rows: pl.BlockSpec((None, rows, LRU_WIDTH), lambda i, j: (i, 0, 0))
    return pl.pallas_call(
        _prompt_mix_kernel,
        grid=(b, s // t),
        in_specs=[tok(D_MODEL)] + _mix_weight_specs(),
        out_specs=[tok(ATT_WIDTH), tok(ATT_WIDTH), tok(ATT_WIDTH), tok(LRU_WIDTH),
                   per_seq(CONV_WIDTH - 1), per_seq(1)],
        out_shape=[jax.ShapeDtypeStruct((b, s, ATT_WIDTH), F32)] * 3
        + [jax.ShapeDtypeStruct((b, s, LRU_WIDTH), BF16),
           jax.ShapeDtypeStruct((b, CONV_WIDTH - 1, LRU_WIDTH), F32),
           jax.ShapeDtypeStruct((b, 1, LRU_WIDTH), F32)],
        scratch_shapes=[pltpu.VMEM((t + SUBLANES, LRU_WIDTH), F32)] + [pltpu.VMEM((t, LRU_WIDTH), F32)] * 3
        + [pltpu.VMEM((SUBLANES, LRU_WIDTH), F32)],
        compiler_params=pltpu.CompilerParams(dimension_semantics=("arbitrary", "arbitrary"),
                                             vmem_limit_bytes=VMEM_LIMIT),
        name="prompt_mix",
    )(x, *weights)


def _sample_mix(x, state_rows, h0, weights):
    rows = x.shape[0]
    t = min(TOKEN_TILE, rows)
    n_seq = t // SUBLANES
    tok = lambda width: pl.BlockSpec((t, width), lambda i: (i, 0))
    return pl.pallas_call(
        _sample_mix_kernel,
        grid=(rows // t,),
        in_specs=[tok(D_MODEL), tok(LRU_WIDTH), pl.BlockSpec((n_seq, LRU_WIDTH), lambda i: (i, 0))]
        + _mix_weight_specs(),
        out_specs=[tok(ATT_WIDTH), tok(ATT_WIDTH), tok(ATT_WIDTH), tok(LRU_WIDTH),
                   pl.BlockSpec((n_seq, CONV_WIDTH - 1, LRU_WIDTH), lambda i: (i, 0, 0)),
                   pl.BlockSpec((n_seq, LRU_WIDTH), lambda i: (i, 0))],
        out_shape=[jax.ShapeDtypeStruct((rows, ATT_WIDTH), F32)] * 3
        + [jax.ShapeDtypeStruct((rows, LRU_WIDTH), BF16),
           jax.ShapeDtypeStruct((rows // SUBLANES, CONV_WIDTH - 1, LRU_WIDTH), F32),
           jax.ShapeDtypeStruct((rows // SUBLANES, LRU_WIDTH), F32)],
        scratch_shapes=[pltpu.VMEM((t, LRU_WIDTH), F32)] * 5,
        compiler_params=pltpu.CompilerParams(dimension_semantics=("arbitrary",), vmem_limit_bytes=VMEM_LIMIT),
        name="sample_mix",
    )(x, state_rows, h0, *weights)


def _prompt_attn_kernel(q_ref, k_ref, v_ref, brow_ref, o_ref, tab_sc, pv_sc, m_sc, den_sc):
    s_len = q_ref.shape[0]
    n_branch = len(DILATIONS)
    qi = lax.broadcasted_iota(jnp.int32, (SPAN, SPAN), 0)
    ki = lax.broadcasted_iota(jnp.int32, (SPAN, SPAN), 1)
    low_head = ki < HEAD_DIM

    for g in range(n_branch):
        for h in range(HEADS_PER_TILE):
            rows = slice(h * SPAN, (h + 1) * SPAN)
            prev = pltpu.roll(jnp.broadcast_to(brow_ref[g, h, 0:1, :], (SPAN, SPAN)), 0, 1, stride=1, stride_axis=0)
            cur = pltpu.roll(jnp.broadcast_to(brow_ref[g, h, 1:2, :], (SPAN, SPAN)), 0, 1, stride=1, stride_axis=0)
            tab_sc[g, rows, 0:SPAN] = jnp.where(ki >= qi, prev, NEG_INF)
            tab_sc[g, rows, SPAN:] = jnp.where(ki <= qi, cur, NEG_INF)

    def attend(branch, q_rows, k_rows, first):
        n_keys = SPAN if first else 2 * SPAN
        q = q_ref[q_rows, :] * Q_SCALE
        q2 = jnp.concatenate([jnp.where(low_head, q, 0.0), jnp.where(low_head, 0.0, q)], axis=0).astype(BF16)
        kk = k_ref[k_rows, :].astype(BF16)
        v_aug = jnp.concatenate([v_ref[k_rows, :], jnp.ones((n_keys, LANES), F32)], axis=1).astype(BF16)
        table = tab_sc[branch, :, SPAN:] if first else tab_sc[branch]
        logits = _dot_nt(q2, kk) + table
        m = jnp.max(logits, axis=-1, keepdims=True)
        p = jnp.exp(logits - m).astype(BF16)
        r = _dot(p, v_aug)
        m_b = jnp.broadcast_to(m, (2 * SPAN, LANES))
        pv_sc[branch, q_rows, :] = jnp.where(low_head, r[:SPAN, :LANES], r[SPAN:, :LANES])
        den_sc[branch, q_rows, :] = jnp.where(low_head, r[:SPAN, LANES:], r[SPAN:, LANES:])
        m_sc[branch, q_rows, :] = jnp.where(low_head, m_b[:SPAN], m_b[SPAN:])

    for branch, dil in enumerate(DILATIONS):
        n_blocks = s_len // dil // SPAN
        if dil == 1:
            attend(branch, pl.ds(0, SPAN), pl.ds(0, SPAN), True)
            group = 3
            assert (n_blocks - 1) % group == 0

            def contiguous(it, carry, branch=branch, group=group):
                for u in range(group):
                    q0 = pl.multiple_of((1 + it * group + u) * SPAN, SPAN)
                    attend(branch, pl.ds(q0, SPAN), pl.ds(pl.multiple_of(q0 - SPAN, SPAN), 2 * SPAN), False)
                return carry
            lax.fori_loop(0, (n_blocks - 1) // group, contiguous, 0)
        elif n_blocks > 1:
            for c in range(dil):
                rows = pl.ds(c, SPAN, stride=dil)
                attend(branch, rows, rows, True)

            def strided(i, carry, branch=branch, dil=dil):
                for c in range(dil):
                    q0 = c + i * (SPAN * dil)
                    attend(branch, pl.ds(q0, SPAN, stride=dil), pl.ds(q0 - SPAN * dil, 2 * SPAN, stride=dil), False)
                return carry
            lax.fori_loop(1, n_blocks, strided, 0)
        else:
            group = 4
            assert dil % group == 0

            def single_block(it, carry, branch=branch, dil=dil, group=group):
                for u in range(group):
                    rows = pl.ds(it * group + u, SPAN, stride=dil)
                    attend(branch, rows, rows, True)
                return carry
            lax.fori_loop(0, dil // group, single_block, 0)

    chunk = 2 * SPAN

    def merge(i, carry):
        rows = pl.ds(pl.multiple_of(i * chunk, chunk), chunk)
        ms = [m_sc[g, rows, :] for g in range(n_branch)]
        m_all = jnp.maximum(jnp.maximum(ms[0], ms[1]), ms[2])
        num = jnp.zeros((chunk, LANES), F32)
        den = jnp.zeros((chunk, LANES), F32)
        for g in range(n_branch):
            w = jnp.exp(ms[g] - m_all)
            num = num + w * pv_sc[g, rows, :]
            den = den + w * den_sc[g, rows, :]
        o_ref[rows, :] = num / den
        return carry

    lax.fori_loop(0, s_len // chunk, merge, 0)


def _prompt_attn(q, k, v, bias_rows):
    b, s, _ = q.shape
    n_branch = len(DILATIONS)
    head_pair = pl.BlockSpec((None, s, LANES), lambda i, j: (i, 0, j))
    return pl.pallas_call(
        _prompt_attn_kernel,
        grid=(b, ATT_WIDTH // LANES),
        in_specs=[head_pair, head_pair, head_pair,
                  pl.BlockSpec((n_branch, HEADS_PER_TILE, 2, SPAN), lambda i, j: (0, j, 0, 0))],
        out_specs=head_pair,
        out_shape=jax.ShapeDtypeStruct((b, s, ATT_WIDTH), F32),
        scratch_shapes=[pltpu.VMEM((n_branch, HEADS_PER_TILE * SPAN, 2 * SPAN), F32)]
        + [pltpu.VMEM((n_branch, s, LANES), F32)] * 3,
        compiler_params=pltpu.CompilerParams(dimension_semantics=("arbitrary", "arbitrary"),
                                             vmem_limit_bytes=VMEM_LIMIT),
        name="prompt_attn",
    )(q, k, v, bias_rows)


def _sample_attn_kernel(q_ref, kn_ref, vn_ref, ck_ref, cv_ref, bm_ref, o_ref, k_all, v_all):
    n_buf = ck_ref.shape[0]
    t = q_ref.shape[0]
    head_of_lane = lax.broadcasted_iota(jnp.int32, (t, ATT_WIDTH), 1) // HEAD_DIM
    q = q_ref[...] * Q_SCALE
    q_bd = jnp.concatenate([jnp.where(head_of_lane == h, q, 0.0) for h in range(N_HEADS)], axis=0).astype(BF16)

    chunk = 512
    for c in range(n_buf // chunk):
        rows = slice(c * chunk, (c + 1) * chunk)
        k_all[rows, :] = ck_ref[rows, :].astype(BF16)
        v_all[rows, :] = cv_ref[rows, :].astype(BF16)
    pad = jnp.zeros((SAMPLE_PAD - t, ATT_WIDTH), F32)
    k_all[n_buf:, :] = jnp.concatenate([kn_ref[...], pad], axis=0).astype(BF16)
    v_all[n_buf:, :] = jnp.concatenate([vn_ref[...], pad], axis=0).astype(BF16)

    scores = _dot_nt(q_bd, k_all[...])
    n_branch = len(DILATIONS)
    m = jnp.max(scores + bm_ref[0], axis=-1, keepdims=True)
    for g in range(1, n_branch):
        m = jnp.maximum(m, jnp.max(scores + bm_ref[g], axis=-1, keepdims=True))
    p = jnp.exp(scores + bm_ref[0] - m)
    for g in range(1, n_branch):
        p = p + jnp.exp(scores + bm_ref[g] - m)
    den = jnp.sum(p, axis=-1, keepdims=True)
    out = _dot(p.astype(BF16), v_all[...]) / den
    att = jnp.zeros((t, ATT_WIDTH), F32)
    for h in range(N_HEADS):
        att = att + jnp.where(head_of_lane == h, out[h * t:(h + 1) * t, :], 0.0)
    o_ref[...] = att


def _sample_attn(q, k_new, v_new, cache_k, cache_v, bias_tables, steps):
    rows = q.shape[0]
    n_seq = rows // steps
    n_buf = cache_k.shape[1]
    tok = pl.BlockSpec((steps, ATT_WIDTH), lambda i: (i, 0))
    cache = pl.BlockSpec((None, n_buf, ATT_WIDTH), lambda i: (i, 0, 0))
    return pl.pallas_call(
        _sample_attn_kernel,
        grid=(n_seq,),
        in_specs=[tok, tok, tok, cache, cache, _const_spec(bias_tables.shape)],
        out_specs=tok,
        out_shape=jax.ShapeDtypeStruct((rows, ATT_WIDTH), F32),
        scratch_shapes=[pltpu.VMEM((n_buf + SAMPLE_PAD, ATT_WIDTH), BF16)] * 2,
        compiler_params=pltpu.CompilerParams(dimension_semantics=("arbitrary",), vmem_limit_bytes=VMEM_LIMIT),
        name="sample_attn",
    )(q, k_new, v_new, cache_k, cache_v, bias_tables)


def _dense_kernel(x_ref, att_ref, rnn_ref, gatt_ref, woa_ref, wor_ref, g2_ref, w1_ref, w2_ref, gf_ref, y_ref):
    att_n = _rms(att_ref[...], gatt_ref[...]).astype(BF16)
    x1 = x_ref[...] + _dot(att_n, woa_ref[...]) + _dot(rnn_ref[...], wor_ref[...])
    n2 = _rms(x1, g2_ref[...]).astype(BF16)
    mlp = None
    for c in range(D_FF // FF_CHUNK):
        cols = slice(c * FF_CHUNK, (c + 1) * FF_CHUNK)
        hmid = jnp.maximum(_dot(n2, w1_ref[:, cols]), 0.0)
        part = _dot((hmid * hmid).astype(BF16), w2_ref[cols, :])
        mlp = part if mlp is None else mlp + part
    y_ref[...] = _rms(x1 + mlp, gf_ref[...])


def _dense(x, att, rnn, weights):
    rows = x.shape[0]
    t = min(TOKEN_TILE, rows)
    tok = lambda width: pl.BlockSpec((t, width), lambda i: (i, 0))
    return pl.pallas_call(
        _dense_kernel,
        grid=(rows // t,),
        in_specs=[tok(D_MODEL), tok(ATT_WIDTH), tok(LRU_WIDTH),
                  _const_spec((1, ATT_WIDTH)), _const_spec((ATT_WIDTH, D_MODEL)), _const_spec((LRU_WIDTH, D_MODEL)),
                  _const_spec((1, D_MODEL)), _const_spec((D_MODEL, D_FF)), _const_spec((D_FF, D_MODEL)),
                  _const_spec((1, D_MODEL))],
        out_specs=tok(D_MODEL),
        out_shape=jax.ShapeDtypeStruct((rows, D_MODEL), F32),
        compiler_params=pltpu.CompilerParams(dimension_semantics=("arbitrary",), vmem_limit_bytes=VMEM_LIMIT),
        name="dense_tail",
    )(x, att, rnn, *weights)


def _t5_bucket(dist):
    max_exact = N_BUCKETS // 2
    d_f = jnp.maximum(dist, max_exact).astype(F32)
    large = max_exact + (jnp.log(d_f / max_exact) / math.log(WIN_MAX / max_exact)
                         * (N_BUCKETS - max_exact)).astype(jnp.int32)
    large = jnp.minimum(large, N_BUCKETS - 1)
    return jnp.where(dist < max_exact, dist, large)


def _branch_bias(rel_bias, dil):
    dist = jnp.arange(SPAN + 1, dtype=jnp.int32) * dil
    one_hot = (_t5_bucket(dist)[:, None] == jnp.arange(N_BUCKETS)[None, :]).astype(F32)
    return jnp.dot(one_hot, rel_bias.astype(F32), precision=lax.Precision.HIGHEST).T


def _prompt_bias_rows(rel_bias):
    rows = []
    for dil in DILATIONS:
        bias = _branch_bias(rel_bias, dil)
        prev = bias[:, SPAN:0:-1]
        cur = jnp.concatenate([bias[:, 0:1], bias[:, SPAN - 1:0:-1]], axis=1)
        rows.append(jnp.stack([prev, cur], axis=1))
    return jnp.stack(rows)


def _sample_bias_tables(rel_bias, n_buf, steps):
    key_pos = np.arange(n_buf + SAMPLE_PAD)
    delta = n_buf + np.arange(steps)[:, None] - key_pos[None, :]
    real = key_pos[None, :] < n_buf + steps
    tables = []
    for dil in DILATIONS:
        valid = real & (delta >= 0) & (delta % dil == 0) & (delta // dil <= SPAN)
        idx = np.clip(delta // dil, 0, SPAN)
        tab = jnp.where(valid[None], _branch_bias(rel_bias, dil)[:, idx], NEG_INF)
        tables.append(tab.reshape(N_HEADS * steps, n_buf + SAMPLE_PAD))
    return jnp.stack(tables)


def _paired_gate_weights(gate_a_w, gate_x_w):
    def pair_diag(w):
        z = jnp.zeros((LRU_BLOCK, LRU_BLOCK), w.dtype)
        return jnp.stack([jnp.block([[w[2 * p], z], [z, w[2 * p + 1]]]) for p in range(N_LRU_BLOCKS // 2)])
    return jnp.concatenate([pair_diag(gate_a_w), pair_diag(gate_x_w)], axis=-1).astype(BF16)


def kernel(x_prompt, x_sample, cache_k, cache_v, state_conv, state_h, norm1_g, w_in, rel_bias, conv_w, conv_b,
           gate_a_w, gate_a_b, gate_x_w, gate_x_b, lru_lambda, att_out_g, rnn_out_g, w_out, norm2_g, w_mlp_in,
           w_mlp_out, final_g):
    depth = w_in.shape[0]
    assert depth == 1, "the final norm is fused into the single layer's dense kernel"
    b, s, _ = x_prompt.shape
    db, steps, _ = x_sample.shape
    n_buf = cache_k.shape[2]
    assert steps == SUBLANES and s == WIN_MAX and n_buf == WIN_MAX
    l = 0
    row = lambda v: v.reshape(1, -1).astype(F32)

    mix_w = (row(norm1_g[l]), w_in[l].astype(BF16), conv_w[l], row(conv_b[l]),
             _paired_gate_weights(gate_a_w[l], gate_x_w[l]), row(gate_a_b[l]), row(gate_x_b[l]),
             row(lru_lambda[l]), row(rnn_out_g[l]))
    dense_w = (row(att_out_g[l]), w_out[l, :ATT_WIDTH].astype(BF16), w_out[l, ATT_WIDTH:].astype(BF16),
               row(norm2_g[l]), w_mlp_in[l].astype(BF16), w_mlp_out[l].astype(BF16), row(final_g))

    q, k, v, rnn, conv_p, h_p = _prompt_mix(x_prompt, mix_w)
    att = _prompt_attn(q, k, v, _prompt_bias_rows(rel_bias))
    y_prompt = _dense(x_prompt.reshape(b * s, D_MODEL), att.reshape(b * s, ATT_WIDTH),
                      rnn.reshape(b * s, LRU_WIDTH), dense_w).reshape(b, s, D_MODEL)

    xs = x_sample.reshape(db * steps, D_MODEL)
    state_rows = jnp.pad(state_conv[l], ((0, 0), (0, SUBLANES - (CONV_WIDTH - 1)), (0, 0))).reshape(db * steps, LRU_WIDTH)
    qs, ks, vs, rnn_s, conv_s, h_s = _sample_mix(xs, state_rows, state_h[l], mix_w)
    att_s = _sample_attn(qs, ks, vs, cache_k[l].reshape(db, n_buf, ATT_WIDTH), cache_v[l].reshape(db, n_buf, ATT_WIDTH),
                         _sample_bias_tables(rel_bias, n_buf, steps), steps)
    y_sample = _dense(xs, att_s, rnn_s, dense_w).reshape(db, steps, D_MODEL)

    kv_p = (1, b, s, N_HEADS, HEAD_DIM)
    kv_s = (1, db, steps, N_HEADS, HEAD_DIM)
    return (y_prompt, y_sample,
            k.reshape(kv_p), v.reshape(kv_p), conv_p[None], h_p.reshape(1, b, LRU_WIDTH),
            ks.reshape(kv_s), vs.reshape(kv_s), conv_s[None], h_s[None])
```

```python
import functools
import math

import numpy as np
import jax
import jax.numpy as jnp
from jax import lax
from jax.experimental import pallas as pl
from jax.experimental.pallas import tpu as pltpu

F32 = jnp.float32
BF16 = jnp.bfloat16

D_MODEL = 1024
ATT_WIDTH = 512
LRU_WIDTH = 512
HEAD_DIM = 64
N_HEADS = 8
N_LRU_BLOCKS = 8
LRU_BLOCK = 64
CONV_WIDTH = 4
LRU_C = 8.0
D_FF = 4096
SPAN = 128
DILATIONS = (1, 4, 16)
WIN_MAX = 2048
N_BUCKETS = 32
NORM_EPS = 1e-6
NEG_INF = -1e30
Q_SCALE = HEAD_DIM ** -0.5

SUBLANES = 8
LANES = 128
HEADS_PER_TILE = LANES // HEAD_DIM
VMEM_LIMIT = 56 * 1024 * 1024

TOKEN_TILE = 512
FF_CHUNK = 1024
SAMPLE_PAD = 128


def _dot(a, b):
    return jnp.dot(a, b, preferred_element_type=F32)


def _dot_nt(a, b):
    return lax.dot_general(a, b, (((1,), (1,)), ((), ())), preferred_element_type=F32)


def _rms(x, g):
    return x * lax.rsqrt(jnp.mean(x * x, axis=-1, keepdims=True) + NORM_EPS) * g


def _const_spec(shape):
    nd = len(shape)
    return pl.BlockSpec(shape, lambda *_: (0,) * nd, pipeline_mode=pl.Buffered(1))


def _lru_gates(xc, wg_ref, ba_ref, bx_ref, lam_ref, a_sc, u_sc):
    xcb = xc.astype(BF16)
    lam = lam_ref[...]
    softplus_neg_lam = jnp.maximum(-lam, 0.0) + jnp.log1p(jnp.exp(-jnp.abs(lam)))
    for p in range(LRU_WIDTH // LANES):
        cols = slice(p * LANES, (p + 1) * LANES)
        g = _dot(xcb[:, cols], wg_ref[p])
        r = jax.nn.sigmoid(g[:, :LANES] + ba_ref[:, cols])
        gi = jax.nn.sigmoid(g[:, LANES:] + bx_ref[:, cols])
        log_a = (-LRU_C) * r * softplus_neg_lam[:, cols]
        a = jnp.exp(log_a)
        a_sc[:, cols] = a
        u_sc[:, cols] = jnp.sqrt(-jnp.tanh(log_a) * (a * a + 1.0)) * (gi * xc[:, cols])


def _group_scan(a, u, row):
    for s in (1, 2, 4):
        keep = row >= s
        u = jnp.where(keep, a * pltpu.roll(u, s, 0) + u, u)
        a = jnp.where(keep, a * pltpu.roll(a, s, 0), a)
    return a, u


def _prompt_mix_kernel(x_ref, g1_ref, win_ref, cw_ref, cb_ref, wg_ref, ba_ref, bx_ref, lam_ref, grnn_ref,
                       q_ref, k_ref, v_ref, rnn_ref, conv_ref, h_ref,
                       xr_ext, a_sc, u_sc, hs_sc, h_carry):
    t = x_ref.shape[0]

    @pl.when(pl.program_id(1) == 0)
    def _():
        xr_ext[0:SUBLANES, :] = jnp.zeros((SUBLANES, LRU_WIDTH), F32)
        h_carry[...] = jnp.zeros_like(h_carry)

    n = _rms(x_ref[...], g1_ref[...]).astype(BF16)
    q_ref[...] = _dot(n, win_ref[:, 0:ATT_WIDTH])
    k_ref[...] = _dot(n, win_ref[:, ATT_WIDTH:2 * ATT_WIDTH])
    v_ref[...] = _dot(n, win_ref[:, 2 * ATT_WIDTH:3 * ATT_WIDTH])
    xr = _dot(n, win_ref[:, 3 * ATT_WIDTH:3 * ATT_WIDTH + LRU_WIDTH])

    xr_ext[SUBLANES:SUBLANES + t, :] = xr
    xc = cb_ref[...] + cw_ref[CONV_WIDTH - 1:CONV_WIDTH, :] * xr
    for back in range(1, CONV_WIDTH):
        xc = xc + cw_ref[CONV_WIDTH - 1 - back:CONV_WIDTH - back, :] * xr_ext[pl.ds(SUBLANES - back, t), :]
    xr_ext[0:SUBLANES, :] = xr[t - SUBLANES:, :]
    conv_ref[...] = xr[t - (CONV_WIDTH - 1):, :]

    _lru_gates(xc, wg_ref, ba_ref, bx_ref, lam_ref, a_sc, u_sc)

    row = lax.broadcasted_iota(jnp.int32, (SUBLANES, LRU_WIDTH), 0)

    def scan_group(g, h):
        rows = pl.ds(pl.multiple_of(g * SUBLANES, SUBLANES), SUBLANES)
        a, u = _group_scan(a_sc[rows, :], u_sc[rows, :], row)
        hs = a * h + u
        hs_sc[rows, :] = hs
        return jnp.broadcast_to(hs[SUBLANES - 1:SUBLANES, :], hs.shape)

    h_last = lax.fori_loop(0, t // SUBLANES, scan_group, h_carry[...], unroll=2)
    h_carry[...] = h_last
    h_ref[...] = h_last[0:1, :]

    gr = _dot(n, win_ref[:, 3 * ATT_WIDTH + LRU_WIDTH:])
    rnn_ref[...] = _rms(hs_sc[...] * jax.nn.gelu(gr), grnn_ref[...]).astype(BF16)


def _sample_mix_kernel(x_ref, st_ref, h0_ref, g1_ref, win_ref, cw_ref, cb_ref, wg_ref, ba_ref, bx_ref, lam_ref,
                       grnn_ref, q_ref, k_ref, v_ref, rnn_ref, conv_ref, h_ref,
                       xr_sc, xc_sc, a_sc, u_sc, hs_sc):
    t = x_ref.shape[0]
    n_seq = t // SUBLANES
    n = _rms(x_ref[...], g1_ref[...]).astype(BF16)
    q_ref[...] = _dot(n, win_ref[:, 0:ATT_WIDTH])
    k_ref[...] = _dot(n, win_ref[:, ATT_WIDTH:2 * ATT_WIDTH])
    v_ref[...] = _dot(n, win_ref[:, 2 * ATT_WIDTH:3 * ATT_WIDTH])
    xr_sc[...] = _dot(n, win_ref[:, 3 * ATT_WIDTH:3 * ATT_WIDTH + LRU_WIDTH])

    row = lax.broadcasted_iota(jnp.int32, (SUBLANES, LRU_WIDTH), 0)
    n_state = CONV_WIDTH - 1

    def conv_group(b, carry):
        rows = pl.ds(pl.multiple_of(b * SUBLANES, SUBLANES), SUBLANES)
        xg = xr_sc[rows, :]
        sg = st_ref[rows, :]
        xc = cb_ref[...] + cw_ref[CONV_WIDTH - 1:CONV_WIDTH, :] * xg
        for back in range(1, CONV_WIDTH):
            prev = jnp.where(row >= back, pltpu.roll(xg, back, 0),
                             pltpu.roll(sg, (back - n_state) % SUBLANES, 0))
            xc = xc + cw_ref[CONV_WIDTH - 1 - back:CONV_WIDTH - back, :] * prev
        xc_sc[rows, :] = xc
        conv_ref[b] = xg[SUBLANES - n_state:, :]
        return carry

    lax.fori_loop(0, n_seq, conv_group, 0)

    _lru_gates(xc_sc[...], wg_ref, ba_ref, bx_ref, lam_ref, a_sc, u_sc)

    def scan_group(b, carry):
        rows = pl.ds(pl.multiple_of(b * SUBLANES, SUBLANES), SUBLANES)
        a, u = _group_scan(a_sc[rows, :], u_sc[rows, :], row)
        hs = a * h0_ref[pl.ds(b, 1), :] + u
        hs_sc[rows, :] = hs
        h_ref[pl.ds(b, 1), :] = hs[SUBLANES - 1:SUBLANES, :]
        return carry

    lax.fori_loop(0, n_seq, scan_group, 0)

    gr = _dot(n, win_ref[:, 3 * ATT_WIDTH + LRU_WIDTH:])
    rnn_ref[...] = _rms(hs_sc[...] * jax.nn.gelu(gr), grnn_ref[...]).astype(BF16)


def _mix_weight_specs():
    return [
        _const_spec((1, D_MODEL)),
        _const_spec((D_MODEL, 3 * ATT_WIDTH + 2 * LRU_WIDTH)),
        _const_spec((CONV_WIDTH, LRU_WIDTH)),
        _const_spec((1, LRU_WIDTH)),
        _const_spec((LRU_WIDTH // LANES, LANES, 2 * LANES)),
        _const_spec((1, LRU_WIDTH)),
        _const_spec((1, LRU_WIDTH)),
        _const_spec((1, LRU_WIDTH)),
        _const_spec((1, LRU_WIDTH)),
    ]


def _prompt_mix(x, weights):
    b, s, _ = x.shape
    t = TOKEN_TILE
    tok = lambda width: pl.BlockSpec((None, t, width), lambda i, j: (i, j, 0))
    per_seq = lambda rows: pl.BlockSpec((None, rows, LRU_WIDTH), lambda i, j: (i, 0, 0))
    return pl.pallas_call(
        _prompt_mix_kernel,
        grid=(b, s // t),
        in_specs=[tok(D_MODEL)] + _mix_weight_specs(),
        out_specs=[tok(ATT_WIDTH), tok(ATT_WIDTH), tok(ATT_WIDTH), tok(LRU_WIDTH),
                   per_seq(CONV_WIDTH - 1), per_seq(1)],
        out_shape=[jax.ShapeDtypeStruct((b, s, ATT_WIDTH), F32)] * 3
        + [jax.ShapeDtypeStruct((b, s, LRU_WIDTH), BF16),
           jax.ShapeDtypeStruct((b, CONV_WIDTH - 1, LRU_WIDTH), F32),
           jax.ShapeDtypeStruct((b, 1, LRU_WIDTH), F32)],
        scratch_shapes=[pltpu.VMEM((t + SUBLANES, LRU_WIDTH), F32)] + [pltpu.VMEM((t, LRU_WIDTH), F32)] * 3
        + [pltpu.VMEM((SUBLANES, LRU_WIDTH), F32)],
        compiler_params=pltpu.CompilerParams(dimension_semantics=("arbitrary", "arbitrary"),
                                             vmem_limit_bytes=VMEM_LIMIT),
        name="prompt_mix",
    )(x, *weights)


def _sample_mix(x, state_rows, h0, weights):
    rows = x.shape[0]
    t = min(TOKEN_TILE, rows)
    n_seq = t // SUBLANES
    tok = lambda width: pl.BlockSpec((t, width), lambda i: (i, 0))
    return pl.pallas_call(
        _sample_mix_kernel,
        grid=(rows // t,),
        in_specs=[tok(D_MODEL), tok(LRU_WIDTH), pl.BlockSpec((n_seq, LRU_WIDTH), lambda i: (i, 0))]
        + _mix_weight_specs(),
        out_specs=[tok(ATT_WIDTH), tok(ATT_WIDTH), tok(ATT_WIDTH), tok(LRU_WIDTH),
                   pl.BlockSpec((n_seq, CONV_WIDTH - 1, LRU_WIDTH), lambda i: (i, 0, 0)),
                   pl.BlockSpec((n_seq, LRU_WIDTH), lambda i: (i, 0))],
        out_shape=[jax.ShapeDtypeStruct((rows, ATT_WIDTH), F32)] * 3
        + [jax.ShapeDtypeStruct((rows, LRU_WIDTH), BF16),
           jax.ShapeDtypeStruct((rows // SUBLANES, CONV_WIDTH - 1, LRU_WIDTH), F32),
           jax.ShapeDtypeStruct((rows // SUBLANES, LRU_WIDTH), F32)],
        scratch_shapes=[pltpu.VMEM((t, LRU_WIDTH), F32)] * 5,
        compiler_params=pltpu.CompilerParams(dimension_semantics=("arbitrary",), vmem_limit_bytes=VMEM_LIMIT),
        name="sample_mix",
    )(x, state_rows, h0, *weights)


def _prompt_attn_kernel(q_ref, k_ref, v_ref, brow_ref, o_ref, tab_sc, pv_sc, m_sc, den_sc):
    s_len = q_ref.shape[0]
    n_branch = len(DILATIONS)
    qi = lax.broadcasted_iota(jnp.int32, (SPAN, SPAN), 0)
    ki = lax.broadcasted_iota(jnp.int32, (SPAN, SPAN), 1)
    low_head = ki < HEAD_DIM

    for g in range(n_branch):
        for h in range(HEADS_PER_TILE):
            rows = slice(h * SPAN, (h + 1) * SPAN)
            prev = pltpu.roll(jnp.broadcast_to(brow_ref[g, h, 0:1, :], (SPAN, SPAN)), 0, 1, stride=1, stride_axis=0)
            cur = pltpu.roll(jnp.broadcast_to(brow_ref[g, h, 1:2, :], (SPAN, SPAN)), 0, 1, stride=1, stride_axis=0)
            tab_sc[g, rows, 0:SPAN] = jnp.where(ki >= qi, prev, NEG_INF)
            tab_sc[g, rows, SPAN:] = jnp.where(ki <= qi, cur, NEG_INF)

    def attend(branch, q_rows, k_rows, first):
        n_keys = SPAN if first else 2 * SPAN
        q = q_ref[q_rows, :] * Q_SCALE
        q2 = jnp.concatenate([jnp.where(low_head, q, 0.0), jnp.where(low_head, 0.0, q)], axis=0).astype(BF16)
        kk = k_ref[k_rows, :].astype(BF16)
        v_aug = jnp.concatenate([v_ref[k_rows, :], jnp.ones((n_keys, LANES), F32)], axis=1).astype(BF16)
        table = tab_sc[branch, :, SPAN:] if first else tab_sc[branch]
        logits = _dot_nt(q2, kk) + table
        m = jnp.max(logits, axis=-1, keepdims=True)
        p = jnp.exp(logits - m).astype(BF16)
        r = _dot(p, v_aug)
        m_b = jnp.broadcast_to(m, (2 * SPAN, LANES))
        pv_sc[branch, q_rows, :] = jnp.where(low_head, r[:SPAN, :LANES], r[SPAN:, :LANES])
        den_sc[branch, q_rows, :] = jnp.where(low_head, r[:SPAN, LANES:], r[SPAN:, LANES:])
        m_sc[branch, q_rows, :] = jnp.where(low_head, m_b[:SPAN], m_b[SPAN:])

    for branch, dil in enumerate(DILATIONS):
        n_blocks = s_len // dil // SPAN
        if dil == 1:
            attend(branch, pl.ds(0, SPAN), pl.ds(0, SPAN), True)
            group = 3
            assert (n_blocks - 1) % group == 0

            def contiguous(it, carry, branch=branch, group=group):
                for u in range(group):
                    q0 = pl.multiple_of((1 + it * group + u) * SPAN, SPAN)
                    attend(branch, pl.ds(q0, SPAN), pl.ds(pl.multiple_of(q0 - SPAN, SPAN), 2 * SPAN), False)
                return carry
            lax.fori_loop(0, (n_blocks - 1) // group, contiguous, 0)
        elif n_blocks > 1:
            for c in range(dil):
                rows = pl.ds(c, SPAN, stride=dil)
                attend(branch, rows, rows, True)

            def strided(i, carry, branch=branch, dil=dil):
                for c in range(dil):
                    q0 = c + i * (SPAN * dil)
                    attend(branch, pl.ds(q0, SPAN, stride=dil), pl.ds(q0 - SPAN * dil, 2 * SPAN, stride=dil), False)
                return carry
            lax.fori_loop(1, n_blocks, strided, 0)
        else:
            group = 4
            assert dil % group == 0

            def single_block(it, carry, branch=branch, dil=dil, group=group):
                for u in range(group):
                    rows = pl.ds(it * group + u, SPAN, stride=dil)
                    attend(branch, rows, rows, True)
                return carry
            lax.fori_loop(0, dil // group, single_block, 0)

    chunk = 2 * SPAN

    def merge(i, carry):
        rows = pl.ds(pl.multiple_of(i * chunk, chunk), chunk)
        ms = [m_sc[g, rows, :] for g in range(n_branch)]
        m_all = jnp.maximum(jnp.maximum(ms[0], ms[1]), ms[2])
        num = jnp.zeros((chunk, LANES), F32)
        den = jnp.zeros((chunk, LANES), F32)
        for g in range(n_branch):
            w = jnp.exp(ms[g] - m_all)
            num = num + w * pv_sc[g, rows, :]
            den = den + w * den_sc[g, rows, :]
        o_ref[rows, :] = num / den
        return carry

    lax.fori_loop(0, s_len // chunk, merge, 0)


def _prompt_attn(q, k, v, bias_rows):
    b, s, _ = q.shape
    n_branch = len(DILATIONS)
    head_pair = pl.BlockSpec((None, s, LANES), lambda i, j: (i, 0, j))
    return pl.pallas_call(
        _prompt_attn_kernel,
        grid=(b, ATT_WIDTH // LANES),
        in_specs=[head_pair, head_pair, head_pair,
                  pl.BlockSpec((n_branch, HEADS_PER_TILE, 2, SPAN), lambda i, j: (0, j, 0, 0))],
        out_specs=head_pair,
        out_shape=jax.ShapeDtypeStruct((b, s, ATT_WIDTH), F32),
        scratch_shapes=[pltpu.VMEM((n_branch, HEADS_PER_TILE * SPAN, 2 * SPAN), F32)]
        + [pltpu.VMEM((n_branch, s, LANES), F32)] * 3,
        compiler_params=pltpu.CompilerParams(dimension_semantics=("arbitrary", "arbitrary"),
                                             vmem_limit_bytes=VMEM_LIMIT),
        name="prompt_attn",
    )(q, k, v, bias_rows)


def _sample_attn_kernel(q_ref, kn_ref, vn_ref, ckt_ref, cvt_ref, tc_ref, tn_ref, o_ref):
    t = q_ref.shape[0]
    n_buf = ckt_ref.shape[-1]
    n_branch = len(DILATIONS)
    head_of_lane = lax.broadcasted_iota(jnp.int32, (t, ATT_WIDTH), 1) // HEAD_DIM
    q = q_ref[...] * Q_SCALE
    q_bd = jnp.concatenate([jnp.where(head_of_lane == h, q, 0.0) for h in range(N_HEADS)], axis=0).astype(BF16)
    pad = jnp.zeros((SAMPLE_PAD - t, ATT_WIDTH), F32)
    k_new = jnp.concatenate([kn_ref[...], pad], axis=0).astype(BF16)
    v_new = jnp.concatenate([vn_ref[...], pad], axis=0).astype(BF16)

    kt = ckt_ref[...].reshape(ATT_WIDTH, n_buf).astype(BF16)
    s_c = _dot(q_bd, kt)
    s_n = _dot_nt(q_bd, k_new)
    m = None
    for g in range(n_branch):
        m_g = jnp.maximum(jnp.max(s_c + tc_ref[g], axis=-1, keepdims=True),
                          jnp.max(s_n + tn_ref[g], axis=-1, keepdims=True))
        m = m_g if m is None else jnp.maximum(m, m_g)
    p_c = jnp.exp(s_c + tc_ref[0] - m)
    p_n = jnp.exp(s_n + tn_ref[0] - m)
    for g in range(1, n_branch):
        p_c = p_c + jnp.exp(s_c + tc_ref[g] - m)
        p_n = p_n + jnp.exp(s_n + tn_ref[g] - m)
    den = jnp.sum(p_c, axis=-1, keepdims=True) + jnp.sum(p_n, axis=-1, keepdims=True)
    vt = cvt_ref[...].reshape(ATT_WIDTH, n_buf).astype(BF16)
    out = (_dot_nt(p_c.astype(BF16), vt) + _dot(p_n.astype(BF16), v_new)) / den
    att = jnp.zeros((t, ATT_WIDTH), F32)
    for h in range(N_HEADS):
        att = att + jnp.where(head_of_lane == h, out[h * t:(h + 1) * t, :], 0.0)
    o_ref[...] = att


def _sample_attn(q, k_new, v_new, cache_kt, cache_vt, cached_tables, new_tables, steps):
    rows = q.shape[0]
    n_seq = rows // steps
    n_buf = cache_kt.shape[-1]
    tok = pl.BlockSpec((steps, ATT_WIDTH), lambda i: (i, 0))
    cache = pl.BlockSpec((None, N_HEADS, HEAD_DIM, n_buf), lambda i: (i, 0, 0, 0))
    return pl.pallas_call(
        _sample_attn_kernel,
        grid=(n_seq,),
        in_specs=[tok, tok, tok, cache, cache, _const_spec(cached_tables.shape), _const_spec(new_tables.shape)],
        out_specs=tok,
        out_shape=jax.ShapeDtypeStruct((rows, ATT_WIDTH), F32),
        compiler_params=pltpu.CompilerParams(dimension_semantics=("arbitrary",), vmem_limit_bytes=VMEM_LIMIT),
        name="sample_attn",
    )(q, k_new, v_new, cache_kt, cache_vt, cached_tables, new_tables)


def _dense_kernel(x_ref, att_ref, rnn_ref, gatt_ref, woa_ref, wor_ref, g2_ref, w1_ref, w2_ref, gf_ref, y_ref):
    att_n = _rms(att_ref[...], gatt_ref[...]).astype(BF16)
    x1 = x_ref[...] + _dot(att_n, woa_ref[...]) + _dot(rnn_ref[...], wor_ref[...])
    n2 = _rms(x1, g2_ref[...]).astype(BF16)
    mlp = None
    for c in range(D_FF // FF_CHUNK):
        cols = slice(c * FF_CHUNK, (c + 1) * FF_CHUNK)
        hmid = jnp.maximum(_dot(n2, w1_ref[:, cols]), 0.0)
        part = _dot((hmid * hmid).astype(BF16), w2_ref[cols, :])
        mlp = part if mlp is None else mlp + part
    y_ref[...] = _rms(x1 + mlp, gf_ref[...])


def _dense(x, att, rnn, weights):
    rows = x.shape[0]
    t = min(TOKEN_TILE, rows)
    tok = lambda width: pl.BlockSpec((t, width), lambda i: (i, 0))
    return pl.pallas_call(
        _dense_kernel,
        grid=(rows // t,),
        in_specs=[tok(D_MODEL), tok(ATT_WIDTH), tok(LRU_WIDTH),
                  _const_spec((1, ATT_WIDTH)), _const_spec((ATT_WIDTH, D_MODEL)), _const_spec((LRU_WIDTH, D_MODEL)),
                  _const_spec((1, D_MODEL)), _const_spec((D_MODEL, D_FF)), _const_spec((D_FF, D_MODEL)),
                  _const_spec((1, D_MODEL))],
        out_specs=tok(D_MODEL),
        out_shape=jax.ShapeDtypeStruct((rows, D_MODEL), F32),
        compiler_params=pltpu.CompilerParams(dimension_semantics=("arbitrary",), vmem_limit_bytes=VMEM_LIMIT),
        name="dense_tail",
    )(x, att, rnn, *weights)


def _t5_bucket(dist):
    max_exact = N_BUCKETS // 2
    d_f = jnp.maximum(dist, max_exact).astype(F32)
    large = max_exact + (jnp.log(d_f / max_exact) / math.log(WIN_MAX / max_exact)
                         * (N_BUCKETS - max_exact)).astype(jnp.int32)
    large = jnp.minimum(large, N_BUCKETS - 1)
    return jnp.where(dist < max_exact, dist, large)


def _branch_bias(rel_bias, dil):
    dist = jnp.arange(SPAN + 1, dtype=jnp.int32) * dil
    one_hot = (_t5_bucket(dist)[:, None] == jnp.arange(N_BUCKETS)[None, :]).astype(F32)
    return jnp.dot(one_hot, rel_bias.astype(F32), precision=lax.Precision.HIGHEST).T


def _prompt_bias_rows(rel_bias):
    rows = []
    for dil in DILATIONS:
        bias = _branch_bias(rel_bias, dil)
        prev = bias[:, SPAN:0:-1]
        cur = jnp.concatenate([bias[:, 0:1], bias[:, SPAN - 1:0:-1]], axis=1)
        rows.append(jnp.stack([prev, cur], axis=1))
    return jnp.stack(rows)


def _sample_bias_tables(rel_bias, n_buf, steps):
    cached, new = [], []
    pos = np.arange(n_buf)
    new_row = np.arange(SAMPLE_PAD)
    delta_new = np.arange(steps)[:, None] - new_row[None, :]
    for dil in DILATIONS:
        bias = _branch_bias(rel_bias, dil)
        every = [bias[:, SPAN:0:-1]] + [jnp.full((N_HEADS, SPAN), NEG_INF, F32)] * (dil - 1)
        row0 = jnp.concatenate([jnp.full((N_HEADS, n_buf - SPAN * dil), NEG_INF, F32),
                                jnp.stack(every, axis=-1).reshape(N_HEADS, SPAN * dil)], axis=1)
        rows = [jnp.where(pos[None, :] >= t, jnp.roll(row0, t, axis=1), NEG_INF) for t in range(steps)]
        cached.append(jnp.stack(rows, axis=1).reshape(N_HEADS * steps, n_buf))
        valid = (new_row[None, :] < steps) & (delta_new >= 0) & (delta_new % dil == 0)
        j_of = np.where(valid, delta_new // dil, -1)
        one_hot = (j_of[None] == np.arange(steps)[:, None, None]).astype(np.float32)
        vals = jnp.sum(bias[:, :steps, None, None] * one_hot[None], axis=1)
        new.append(jnp.where(valid[None], vals, NEG_INF).reshape(N_HEADS * steps, SAMPLE_PAD))
    return jnp.stack(cached), jnp.stack(new)


def _paired_gate_weights(gate_a_w, gate_x_w):
    def pair_diag(w):
        z = jnp.zeros((LRU_BLOCK, LRU_BLOCK), w.dtype)
        return jnp.stack([jnp.block([[w[2 * p], z], [z, w[2 * p + 1]]]) for p in range(N_LRU_BLOCKS // 2)])
    return jnp.concatenate([pair_diag(gate_a_w), pair_diag(gate_x_w)], axis=-1).astype(BF16)


def kernel(x_prompt, x_sample, cache_k, cache_v, state_conv, state_h, norm1_g, w_in, rel_bias, conv_w, conv_b,
           gate_a_w, gate_a_b, gate_x_w, gate_x_b, lru_lambda, att_out_g, rnn_out_g, w_out, norm2_g, w_mlp_in,
           w_mlp_out, final_g):
    depth = w_in.shape[0]
    assert depth == 1, "the final norm is fused into the single layer's dense kernel"
    b, s, _ = x_prompt.shape
    db, steps, _ = x_sample.shape
    n_buf = cache_k.shape[2]
    assert steps == SUBLANES and s == WIN_MAX and n_buf == WIN_MAX
    l = 0
    row = lambda v: v.reshape(1, -1).astype(F32)

    mix_w = (row(norm1_g[l]), w_in[l].astype(BF16), conv_w[l], row(conv_b[l]),
             _paired_gate_weights(gate_a_w[l], gate_x_w[l]), row(gate_a_b[l]), row(gate_x_b[l]),
             row(lru_lambda[l]), row(rnn_out_g[l]))
    dense_w = (row(att_out_g[l]), w_out[l, :ATT_WIDTH].astype(BF16), w_out[l, ATT_WIDTH:].astype(BF16),
               row(norm2_g[l]), w_mlp_in[l].astype(BF16), w_mlp_out[l].astype(BF16), row(final_g))

    q, k, v, rnn, conv_p, h_p = _prompt_mix(x_prompt, mix_w)
    att = _prompt_attn(q, k, v, _prompt_bias_rows(rel_bias))
    y_prompt = _dense(x_prompt.reshape(b * s, D_MODEL), att.reshape(b * s, ATT_WIDTH),
                      rnn.reshape(b * s, LRU_WIDTH), dense_w).reshape(b, s, D_MODEL)

    xs = x_sample.reshape(db * steps, D_MODEL)
    state_rows = jnp.pad(state_conv[l], ((0, 0), (0, SUBLANES - (CONV_WIDTH - 1)), (0, 0))).reshape(db * steps, LRU_WIDTH)
    qs, ks, vs, rnn_s, conv_s, h_s = _sample_mix(xs, state_rows, state_h[l], mix_w)
    cache_kt = jnp.transpose(cache_k[l], (0, 2, 3, 1))
    cache_vt = jnp.transpose(cache_v[l], (0, 2, 3, 1))
    att_s = _sample_attn(qs, ks, vs, cache_kt, cache_vt, *_sample_bias_tables(rel_bias, n_buf, steps), steps)
    y_sample = _dense(xs, att_s, rnn_s, dense_w).reshape(db, steps, D_MODEL)

    kv_p = (1, b, s, N_HEADS, HEAD_DIM)
    kv_s = (1, db, steps, N_HEADS, HEAD_DIM)
    return (y_prompt, y_sample,
            k.reshape(kv_p), v.reshape(kv_p), conv_p[None], h_p.reshape(1, b, LRU_WIDTH),
            ks.reshape(kv_s), vs.reshape(kv_s), conv_s[None], h_s[None])
```

```python
import functools
import math

import numpy as np
import jax
import jax.numpy as jnp
from jax import lax
from jax.experimental import pallas as pl
from jax.experimental.pallas import tpu as pltpu

F32 = jnp.float32
BF16 = jnp.bfloat16

D_MODEL = 1024
ATT_WIDTH = 512
LRU_WIDTH = 512
HEAD_DIM = 64
N_HEADS = 8
N_LRU_BLOCKS = 8
LRU_BLOCK = 64
CONV_WIDTH = 4
LRU_C = 8.0
D_FF = 4096
SPAN = 128
DILATIONS = (1, 4, 16)
WIN_MAX = 2048
N_BUCKETS = 32
NORM_EPS = 1e-6
NEG_INF = -1e30
Q_SCALE = HEAD_DIM ** -0.5

SUBLANES = 8
LANES = 128
HEADS_PER_TILE = LANES // HEAD_DIM
VMEM_LIMIT = 56 * 1024 * 1024

TOKEN_TILE = 512
FF_CHUNK = 1024
SAMPLE_PAD = 128


def _dot(a, b):
    return jnp.dot(a, b, preferred_element_type=F32)


def _dot_nt(a, b):
    return lax.dot_general(a, b, (((1,), (1,)), ((), ())), preferred_element_type=F32)


def _rms(x, g):
    return x * lax.rsqrt(jnp.mean(x * x, axis=-1, keepdims=True) + NORM_EPS) * g


def _const_spec(shape):
    nd = len(shape)
    return pl.BlockSpec(shape, lambda *_: (0,) * nd, pipeline_mode=pl.Buffered(1))


def _lru_gates(xc, wg_ref, ba_ref, bx_ref, lam_ref, a_sc, u_sc):
    xcb = xc.astype(BF16)
    lam = lam_ref[...]
    softplus_neg_lam = jnp.maximum(-lam, 0.0) + jnp.log1p(jnp.exp(-jnp.abs(lam)))
    for p in range(LRU_WIDTH // LANES):
        cols = slice(p * LANES, (p + 1) * LANES)
        g = _dot(xcb[:, cols], wg_ref[p])
        r = jax.nn.sigmoid(g[:, :LANES] + ba_ref[:, cols])
        gi = jax.nn.sigmoid(g[:, LANES:] + bx_ref[:, cols])
        log_a = (-LRU_C) * r * softplus_neg_lam[:, cols]
        a = jnp.exp(log_a)
        a_sc[:, cols] = a
        u_sc[:, cols] = jnp.sqrt(-jnp.tanh(log_a) * (a * a + 1.0)) * (gi * xc[:, cols])


def _group_scan(a, u, row):
    for s in (1, 2, 4):
        keep = row >= s
        u = jnp.where(keep, a * pltpu.roll(u, s, 0) + u, u)
        a = jnp.where(keep, a * pltpu.roll(a, s, 0), a)
    return a, u


def _prompt_mix_kernel(x_ref, g1_ref, win_ref, cw_ref, cb_ref, wg_ref, ba_ref, bx_ref, lam_ref, grnn_ref,
                       q_ref, k_ref, v_ref, rnn_ref, conv_ref, h_ref,
                       xr_ext, a_sc, u_sc, hs_sc, h_carry):
    t = x_ref.shape[0]

    @pl.when(pl.program_id(1) == 0)
    def _():
        xr_ext[0:SUBLANES, :] = jnp.zeros((SUBLANES, LRU_WIDTH), F32)
        h_carry[...] = jnp.zeros_like(h_carry)

    n = _rms(x_ref[...], g1_ref[...]).astype(BF16)
    q_ref[...] = _dot(n, win_ref[:, 0:ATT_WIDTH])
    k_ref[...] = _dot(n, win_ref[:, ATT_WIDTH:2 * ATT_WIDTH])
    v_ref[...] = _dot(n, win_ref[:, 2 * ATT_WIDTH:3 * ATT_WIDTH])
    xr = _dot(n, win_ref[:, 3 * ATT_WIDTH:3 * ATT_WIDTH + LRU_WIDTH])

    xr_ext[SUBLANES:SUBLANES + t, :] = xr
    xc = cb_ref[...] + cw_ref[CONV_WIDTH - 1:CONV_WIDTH, :] * xr
    for back in range(1, CONV_WIDTH):
        xc = xc + cw_ref[CONV_WIDTH - 1 - back:CONV_WIDTH - back, :] * xr_ext[pl.ds(SUBLANES - back, t), :]
    xr_ext[0:SUBLANES, :] = xr[t - SUBLANES:, :]
    conv_ref[...] = xr[t - (CONV_WIDTH - 1):, :]

    _lru_gates(xc, wg_ref, ba_ref, bx_ref, lam_ref, a_sc, u_sc)

    row = lax.broadcasted_iota(jnp.int32, (SUBLANES, LRU_WIDTH), 0)

    def scan_group(g, h):
        rows = pl.ds(pl.multiple_of(g * SUBLANES, SUBLANES), SUBLANES)
        a, u = _group_scan(a_sc[rows, :], u_sc[rows, :], row)
        hs = a * h + u
        hs_sc[rows, :] = hs
        return jnp.broadcast_to(hs[SUBLANES - 1:SUBLANES, :], hs.shape)

    h_last = lax.fori_loop(0, t // SUBLANES, scan_group, h_carry[...], unroll=True)
    h_carry[...] = h_last
    h_ref[...] = h_last[0:1, :]

    gr = _dot(n, win_ref[:, 3 * ATT_WIDTH + LRU_WIDTH:])
    rnn_ref[...] = _rms(hs_sc[...] * jax.nn.gelu(gr), grnn_ref[...]).astype(BF16)


def _sample_mix_kernel(x_ref, st_ref, h0_ref, g1_ref, win_ref, cw_ref, cb_ref, wg_ref, ba_ref, bx_ref, lam_ref,
                       grnn_ref, q_ref, k_ref, v_ref, rnn_ref, conv_ref, h_ref,
                       xr_sc, xc_sc, a_sc, u_sc, hs_sc):
    t = x_ref.shape[0]
    n_seq = t // SUBLANES
    n = _rms(x_ref[...], g1_ref[...]).astype(BF16)
    q_ref[...] = _dot(n, win_ref[:, 0:ATT_WIDTH])
    k_ref[...] = _dot(n, win_ref[:, ATT_WIDTH:2 * ATT_WIDTH])
    v_ref[...] = _dot(n, win_ref[:, 2 * ATT_WIDTH:3 * ATT_WIDTH])
    xr_sc[...] = _dot(n, win_ref[:, 3 * ATT_WIDTH:3 * ATT_WIDTH + LRU_WIDTH])

    row = lax.broadcasted_iota(jnp.int32, (SUBLANES, LRU_WIDTH), 0)
    n_state = CONV_WIDTH - 1

    def conv_group(b, carry):
        rows = pl.ds(pl.multiple_of(b * SUBLANES, SUBLANES), SUBLANES)
        xg = xr_sc[rows, :]
        sg = st_ref[rows, :]
        xc = cb_ref[...] + cw_ref[CONV_WIDTH - 1:CONV_WIDTH, :] * xg
        for back in range(1, CONV_WIDTH):
            prev = jnp.where(row >= back, pltpu.roll(xg, back, 0),
                             pltpu.roll(sg, (back - n_state) % SUBLANES, 0))
            xc = xc + cw_ref[CONV_WIDTH - 1 - back:CONV_WIDTH - back, :] * prev
        xc_sc[rows, :] = xc
        conv_ref[b] = xg[SUBLANES - n_state:, :]
        return carry

    lax.fori_loop(0, n_seq, conv_group, 0)

    _lru_gates(xc_sc[...], wg_ref, ba_ref, bx_ref, lam_ref, a_sc, u_sc)

    def scan_group(b, carry):
        rows = pl.ds(pl.multiple_of(b * SUBLANES, SUBLANES), SUBLANES)
        a, u = _group_scan(a_sc[rows, :], u_sc[rows, :], row)
        hs = a * h0_ref[pl.ds(b, 1), :] + u
        hs_sc[rows, :] = hs
        h_ref[pl.ds(b, 1), :] = hs[SUBLANES - 1:SUBLANES, :]
        return carry

    lax.fori_loop(0, n_seq, scan_group, 0)

    gr = _dot(n, win_ref[:, 3 * ATT_WIDTH + LRU_WIDTH:])
    rnn_ref[...] = _rms(hs_sc[...] * jax.nn.gelu(gr), grnn_ref[...]).astype(BF16)


def _mix_weight_specs():
    return [
        _const_spec((1, D_MODEL)),
        _const_spec((D_MODEL, 3 * ATT_WIDTH + 2 * LRU_WIDTH)),
        _const_spec((CONV_WIDTH, LRU_WIDTH)),
        _const_spec((1, LRU_WIDTH)),
        _const_spec((LRU_WIDTH // LANES, LANES, 2 * LANES)),
        _const_spec((1, LRU_WIDTH)),
        _const_spec((1, LRU_WIDTH)),
        _const_spec((1, LRU_WIDTH)),
        _const_spec((1, LRU_WIDTH)),
    ]


def _prompt_mix(x, weights):
    b, s, _ = x.shape
    t = TOKEN_TILE
    tok = lambda width: pl.BlockSpec((None, t, width), lambda i, j: (i, j, 0))
    per_seq = lambda rows: pl.BlockSpec((None, rows, LRU_WIDTH), lambda i, j: (i, 0, 0))
    return pl.pallas_call(
        _prompt_mix_kernel,
        grid=(b, s // t),
        in_specs=[tok(D_MODEL)] + _mix_weight_specs(),
        out_specs=[tok(ATT_WIDTH), tok(ATT_WIDTH), tok(ATT_WIDTH), tok(LRU_WIDTH),
                   per_seq(CONV_WIDTH - 1), per_seq(1)],
        out_shape=[jax.ShapeDtypeStruct((b, s, ATT_WIDTH), F32)] * 3
        + [jax.ShapeDtypeStruct((b, s, LRU_WIDTH), BF16),
           jax.ShapeDtypeStruct((b, CONV_WIDTH - 1, LRU_WIDTH), F32),
           jax.ShapeDtypeStruct((b, 1, LRU_WIDTH), F32)],
        scratch_shapes=[pltpu.VMEM((t + SUBLANES, LRU_WIDTH), F32)] + [pltpu.VMEM((t, LRU_WIDTH), F32)] * 3
        + [pltpu.VMEM((SUBLANES, LRU_WIDTH), F32)],
        compiler_params=pltpu.CompilerParams(dimension_semantics=("arbitrary", "arbitrary"),
                                             vmem_limit_bytes=VMEM_LIMIT),
        name="prompt_mix",
    )(x, *weights)


def _sample_mix(x, state_rows, h0, weights):
    rows = x.shape[0]
    t = min(TOKEN_TILE, rows)
    n_seq = t // SUBLANES
    tok = lambda width: pl.BlockSpec((t, width), lambda i: (i, 0))
    return pl.pallas_call(
        _sample_mix_kernel,
        grid=(rows // t,),
        in_specs=[tok(D_MODEL), tok(LRU_WIDTH), pl.BlockSpec((n_seq, LRU_WIDTH), lambda i: (i, 0))]
        + _mix_weight_specs(),
        out_specs=[tok(ATT_WIDTH), tok(ATT_WIDTH), tok(ATT_WIDTH), tok(LRU_WIDTH),
                   pl.BlockSpec((n_seq, CONV_WIDTH - 1, LRU_WIDTH), lambda i: (i, 0, 0)),
                   pl.BlockSpec((n_seq, LRU_WIDTH), lambda i: (i, 0))],
        out_shape=[jax.ShapeDtypeStruct((rows, ATT_WIDTH), F32)] * 3
        + [jax.ShapeDtypeStruct((rows, LRU_WIDTH), BF16),
           jax.ShapeDtypeStruct((rows // SUBLANES, CONV_WIDTH - 1, LRU_WIDTH), F32),
           jax.ShapeDtypeStruct((rows // SUBLANES, LRU_WIDTH), F32)],
        scratch_shapes=[pltpu.VMEM((t, LRU_WIDTH), F32)] * 5,
        compiler_params=pltpu.CompilerParams(dimension_semantics=("arbitrary",), vmem_limit_bytes=VMEM_LIMIT),
        name="sample_mix",
    )(x, state_rows, h0, *weights)


def _prompt_attn_kernel(q_ref, k_ref, v_ref, brow_ref, o_ref, tab_sc, pv_sc, m_sc, den_sc):
    s_len = q_ref.shape[0]
    n_branch = len(DILATIONS)
    qi = lax.broadcasted_iota(jnp.int32, (SPAN, SPAN), 0)
    ki = lax.broadcasted_iota(jnp.int32, (SPAN, SPAN), 1)
    low_head = ki < HEAD_DIM

    for g in range(n_branch):
        for h in range(HEADS_PER_TILE):
            rows = slice(h * SPAN, (h + 1) * SPAN)
            prev = pltpu.roll(jnp.broadcast_to(brow_ref[g, h, 0:1, :], (SPAN, SPAN)), 0, 1, stride=1, stride_axis=0)
            cur = pltpu.roll(jnp.broadcast_to(brow_ref[g, h, 1:2, :], (SPAN, SPAN)), 0, 1, stride=1, stride_axis=0)
            tab_sc[g, rows, 0:SPAN] = jnp.where(ki >= qi, prev, NEG_INF)
            tab_sc[g, rows, SPAN:] = jnp.where(ki <= qi, cur, NEG_INF)

    def attend(branch, q_rows, k_rows, first):
        n_keys = SPAN if first else 2 * SPAN
        q = q_ref[q_rows, :] * Q_SCALE
        q2 = jnp.concatenate([jnp.where(low_head, q, 0.0), jnp.where(low_head, 0.0, q)], axis=0).astype(BF16)
        kk = k_ref[k_rows, :].astype(BF16)
        v_aug = jnp.concatenate([v_ref[k_rows, :], jnp.ones((n_keys, LANES), F32)], axis=1).astype(BF16)
        table = tab_sc[branch, :, SPAN:] if first else tab_sc[branch]
        logits = _dot_nt(q2, kk) + table
        m = jnp.max(logits, axis=-1, keepdims=True)
        p = jnp.exp(logits - m).astype(BF16)
        r = _dot(p, v_aug)
        m_b = jnp.broadcast_to(m, (2 * SPAN, LANES))
        pv_sc[branch, q_rows, :] = jnp.where(low_head, r[:SPAN, :LANES], r[SPAN:, :LANES])
        den_sc[branch, q_rows, :] = jnp.where(low_head, r[:SPAN, LANES:], r[SPAN:, LANES:])
        m_sc[branch, q_rows, :] = jnp.where(low_head, m_b[:SPAN], m_b[SPAN:])

    for branch, dil in enumerate(DILATIONS):
        n_blocks = s_len // dil // SPAN
        if dil == 1:
            attend(branch, pl.ds(0, SPAN), pl.ds(0, SPAN), True)
            group = 5
            assert (n_blocks - 1) % group == 0

            def contiguous(it, carry, branch=branch, group=group):
                for u in range(group):
                    q0 = pl.multiple_of((1 + it * group + u) * SPAN, SPAN)
                    attend(branch, pl.ds(q0, SPAN), pl.ds(pl.multiple_of(q0 - SPAN, SPAN), 2 * SPAN), False)
                return carry
            lax.fori_loop(0, (n_blocks - 1) // group, contiguous, 0)
        elif n_blocks > 1:
            for c in range(dil):
                rows = pl.ds(c, SPAN, stride=dil)
                attend(branch, rows, rows, True)

            def strided(i, carry, branch=branch, dil=dil):
                for c in range(dil):
                    q0 = c + i * (SPAN * dil)
                    attend(branch, pl.ds(q0, SPAN, stride=dil), pl.ds(q0 - SPAN * dil, 2 * SPAN, stride=dil), False)
                return carry
            lax.fori_loop(1, n_blocks, strided, 0, unroll=True)
        else:
            group = 8
            assert dil % group == 0

            def single_block(it, carry, branch=branch, dil=dil, group=group):
                for u in range(group):
                    rows = pl.ds(it * group + u, SPAN, stride=dil)
                    attend(branch, rows, rows, True)
                return carry
            lax.fori_loop(0, dil // group, single_block, 0)

    chunk = 2 * SPAN

    def merge(i, carry):
        rows = pl.ds(pl.multiple_of(i * chunk, chunk), chunk)
        ms = [m_sc[g, rows, :] for g in range(n_branch)]
        m_all = jnp.maximum(jnp.maximum(ms[0], ms[1]), ms[2])
        num = jnp.zeros((chunk, LANES), F32)
        den = jnp.zeros((chunk, LANES), F32)
        for g in range(n_branch):
            w = jnp.exp(ms[g] - m_all)
            num = num + w * pv_sc[g, rows, :]
            den = den + w * den_sc[g, rows, :]
        o_ref[rows, :] = num / den
        return carry

    lax.fori_loop(0, s_len // chunk, merge, 0)


def _prompt_attn(q, k, v, bias_rows):
    b, s, _ = q.shape
    n_branch = len(DILATIONS)
    head_pair = pl.BlockSpec((None, s, LANES), lambda i, j: (i, 0, j))
    return pl.pallas_call(
        _prompt_attn_kernel,
        grid=(b, ATT_WIDTH // LANES),
        in_specs=[head_pair, head_pair, head_pair,
                  pl.BlockSpec((n_branch, HEADS_PER_TILE, 2, SPAN), lambda i, j: (0, j, 0, 0))],
        out_specs=head_pair,
        out_shape=jax.ShapeDtypeStruct((b, s, ATT_WIDTH), F32),
        scratch_shapes=[pltpu.VMEM((n_branch, HEADS_PER_TILE * SPAN, 2 * SPAN), F32)]
        + [pltpu.VMEM((n_branch, s, LANES), F32)] * 3,
        compiler_params=pltpu.CompilerParams(dimension_semantics=("arbitrary", "arbitrary"),
                                             vmem_limit_bytes=VMEM_LIMIT),
        name="prompt_attn",
    )(q, k, v, bias_rows)


def _sample_attn_kernel(q_ref, kn_ref, vn_ref, ckt_ref, cvt_ref, tc_ref, tn_ref, o_ref):
    t = q_ref.shape[0]
    n_buf = ckt_ref.shape[-1]
    n_branch = len(DILATIONS)
    head_of_lane = lax.broadcasted_iota(jnp.int32, (t, ATT_WIDTH), 1) // HEAD_DIM
    q = q_ref[...] * Q_SCALE
    q_bd = jnp.concatenate([jnp.where(head_of_lane == h, q, 0.0) for h in range(N_HEADS)], axis=0).astype(BF16)
    pad = jnp.zeros((SAMPLE_PAD - t, ATT_WIDTH), F32)
    k_new = jnp.concatenate([kn_ref[...], pad], axis=0).astype(BF16)
    v_new = jnp.concatenate([vn_ref[...], pad], axis=0).astype(BF16)

    kt = ckt_ref[...].reshape(ATT_WIDTH, n_buf).astype(BF16)
    s_c = _dot(q_bd, kt)
    s_n = _dot_nt(q_bd, k_new)
    m = None
    for g in range(n_branch):
        m_g = jnp.maximum(jnp.max(s_c + tc_ref[g], axis=-1, keepdims=True),
                          jnp.max(s_n + tn_ref[g], axis=-1, keepdims=True))
        m = m_g if m is None else jnp.maximum(m, m_g)
    p_c = jnp.exp(s_c + tc_ref[0] - m)
    p_n = jnp.exp(s_n + tn_ref[0] - m)
    for g in range(1, n_branch):
        p_c = p_c + jnp.exp(s_c + tc_ref[g] - m)
        p_n = p_n + jnp.exp(s_n + tn_ref[g] - m)
    den = jnp.sum(p_c, axis=-1, keepdims=True) + jnp.sum(p_n, axis=-1, keepdims=True)
    vt = cvt_ref[...].reshape(ATT_WIDTH, n_buf).astype(BF16)
    out = (_dot_nt(p_c.astype(BF16), vt) + _dot(p_n.astype(BF16), v_new)) / den
    att = jnp.zeros((t, ATT_WIDTH), F32)
    for h in range(N_HEADS):
        att = att + jnp.where(head_of_lane == h, out[h * t:(h + 1) * t, :], 0.0)
    o_ref[...] = att


def _sample_attn(q, k_new, v_new, cache_kt, cache_vt, cached_tables, new_tables, steps):
    rows = q.shape[0]
    n_seq = rows // steps
    n_buf = cache_kt.shape[-1]
    tok = pl.BlockSpec((steps, ATT_WIDTH), lambda i: (i, 0))
    cache = pl.BlockSpec((None, N_HEADS, HEAD_DIM, n_buf), lambda i: (i, 0, 0, 0))
    return pl.pallas_call(
        _sample_attn_kernel,
        grid=(n_seq,),
        in_specs=[tok, tok, tok, cache, cache, _const_spec(cached_tables.shape), _const_spec(new_tables.shape)],
        out_specs=tok,
        out_shape=jax.ShapeDtypeStruct((rows, ATT_WIDTH), F32),
        compiler_params=pltpu.CompilerParams(dimension_semantics=("arbitrary",), vmem_limit_bytes=VMEM_LIMIT),
        name="sample_attn",
    )(q, k_new, v_new, cache_kt, cache_vt, cached_tables, new_tables)


def _dense_kernel(x_ref, att_ref, rnn_ref, gatt_ref, woa_ref, wor_ref, g2_ref, w1_ref, w2_ref, gf_ref, y_ref):
    att_n = _rms(att_ref[...], gatt_ref[...]).astype(BF16)
    x1 = x_ref[...] + _dot(att_n, woa_ref[...]) + _dot(rnn_ref[...], wor_ref[...])
    n2 = _rms(x1, g2_ref[...]).astype(BF16)
    mlp = None
    for c in range(D_FF // FF_CHUNK):
        cols = slice(c * FF_CHUNK, (c + 1) * FF_CHUNK)
        hmid = jnp.maximum(_dot(n2, w1_ref[:, cols]), 0.0)
        part = _dot((hmid * hmid).astype(BF16), w2_ref[cols, :])
        mlp = part if mlp is None else mlp + part
    y_ref[...] = _rms(x1 + mlp, gf_ref[...])


def _dense(x, att, rnn, weights):
    rows = x.shape[0]
    t = min(TOKEN_TILE, rows)
    tok = lambda width: pl.BlockSpec((t, width), lambda i: (i, 0))
    return pl.pallas_call(
        _dense_kernel,
        grid=(rows // t,),
        in_specs=[tok(D_MODEL), tok(ATT_WIDTH), tok(LRU_WIDTH),
                  _const_spec((1, ATT_WIDTH)), _const_spec((ATT_WIDTH, D_MODEL)), _const_spec((LRU_WIDTH, D_MODEL)),
                  _const_spec((1, D_MODEL)), _const_spec((D_MODEL, D_FF)), _const_spec((D_FF, D_MODEL)),
                  _const_spec((1, D_MODEL))],
        out_specs=tok(D_MODEL),
        out_shape=jax.ShapeDtypeStruct((rows, D_MODEL), F32),
        compiler_params=pltpu.CompilerParams(dimension_semantics=("arbitrary",), vmem_limit_bytes=VMEM_LIMIT),
        name="dense_tail",
    )(x, att, rnn, *weights)


def _t5_bucket(dist):
    max_exact = N_BUCKETS // 2
    d_f = jnp.maximum(dist, max_exact).astype(F32)
    large = max_exact + (jnp.log(d_f / max_exact) / math.log(WIN_MAX / max_exact)
                         * (N_BUCKETS - max_exact)).astype(jnp.int32)
    large = jnp.minimum(large, N_BUCKETS - 1)
    return jnp.where(dist < max_exact, dist, large)


def _branch_bias(rel_bias, dil):
    dist = jnp.arange(SPAN + 1, dtype=jnp.int32) * dil
    one_hot = (_t5_bucket(dist)[:, None] == jnp.arange(N_BUCKETS)[None, :]).astype(F32)
    return jnp.dot(one_hot, rel_bias.astype(F32), precision=lax.Precision.HIGHEST).T


def _prompt_bias_rows(rel_bias):
    rows = []
    for dil in DILATIONS:
        bias = _branch_bias(rel_bias, dil)
        prev = bias[:, SPAN:0:-1]
        cur = jnp.concatenate([bias[:, 0:1], bias[:, SPAN - 1:0:-1]], axis=1)
        rows.append(jnp.stack([prev, cur], axis=1))
    return jnp.stack(rows)


def _sample_bias_tables(rel_bias, n_buf, steps):
    cached, new = [], []
    pos = np.arange(n_buf)
    new_row = np.arange(SAMPLE_PAD)
    delta_new = np.arange(steps)[:, None] - new_row[None, :]
    for dil in DILATIONS:
        bias = _branch_bias(rel_bias, dil)
        every = [bias[:, SPAN:0:-1]] + [jnp.full((N_HEADS, SPAN), NEG_INF, F32)] * (dil - 1)
        row0 = jnp.concatenate([jnp.full((N_HEADS, n_buf - SPAN * dil), NEG_INF, F32),
                                jnp.stack(every, axis=-1).reshape(N_HEADS, SPAN * dil)], axis=1)
        rows = [jnp.where(pos[None, :] >= t, jnp.roll(row0, t, axis=1), NEG_INF) for t in range(steps)]
        cached.append(jnp.stack(rows, axis=1).reshape(N_HEADS * steps, n_buf))
        valid = (new_row[None, :] < steps) & (delta_new >= 0) & (delta_new % dil == 0)
        j_of = np.where(valid, delta_new // dil, -1)
        one_hot = (j_of[None] == np.arange(steps)[:, None, None]).astype(np.float32)
        vals = jnp.sum(bias[:, :steps, None, None] * one_hot[None], axis=1)
        new.append(jnp.where(valid[None], vals, NEG_INF).reshape(N_HEADS * steps, SAMPLE_PAD))
    return jnp.stack(cached), jnp.stack(new)


def _paired_gate_weights(gate_a_w, gate_x_w):
    def pair_diag(w):
        z = jnp.zeros((LRU_BLOCK, LRU_BLOCK), w.dtype)
        return jnp.stack([jnp.block([[w[2 * p], z], [z, w[2 * p + 1]]]) for p in range(N_LRU_BLOCKS // 2)])
    return jnp.concatenate([pair_diag(gate_a_w), pair_diag(gate_x_w)], axis=-1).astype(BF16)


def kernel(x_prompt, x_sample, cache_k, cache_v, state_conv, state_h, norm1_g, w_in, rel_bias, conv_w, conv_b,
           gate_a_w, gate_a_b, gate_x_w, gate_x_b, lru_lambda, att_out_g, rnn_out_g, w_out, norm2_g, w_mlp_in,
           w_mlp_out, final_g):
    depth = w_in.shape[0]
    assert depth == 1, "the final norm is fused into the single layer's dense kernel"
    b, s, _ = x_prompt.shape
    db, steps, _ = x_sample.shape
    n_buf = cache_k.shape[2]
    assert steps == SUBLANES and s == WIN_MAX and n_buf == WIN_MAX
    l = 0
    row = lambda v: v.reshape(1, -1).astype(F32)

    mix_w = (row(norm1_g[l]), w_in[l].astype(BF16), conv_w[l], row(conv_b[l]),
             _paired_gate_weights(gate_a_w[l], gate_x_w[l]), row(gate_a_b[l]), row(gate_x_b[l]),
             row(lru_lambda[l]), row(rnn_out_g[l]))
    dense_w = (row(att_out_g[l]), w_out[l, :ATT_WIDTH].astype(BF16), w_out[l, ATT_WIDTH:].astype(BF16),
               row(norm2_g[l]), w_mlp_in[l].astype(BF16), w_mlp_out[l].astype(BF16), row(final_g))

    q, k, v, rnn, conv_p, h_p = _prompt_mix(x_prompt, mix_w)
    att = _prompt_attn(q, k, v, _prompt_bias_rows(rel_bias))
    y_prompt = _dense(x_prompt.reshape(b * s, D_MODEL), att.reshape(b * s, ATT_WIDTH),
                      rnn.reshape(b * s, LRU_WIDTH), dense_w).reshape(b, s, D_MODEL)

    xs = x_sample.reshape(db * steps, D_MODEL)
    state_rows = jnp.pad(state_conv[l], ((0, 0), (0, SUBLANES - (CONV_WIDTH - 1)), (0, 0))).reshape(db * steps, LRU_WIDTH)
    qs, ks, vs, rnn_s, conv_s, h_s = _sample_mix(xs, state_rows, state_h[l], mix_w)
    cache_kt = jnp.transpose(cache_k[l], (0, 2, 3, 1))
    cache_vt = jnp.transpose(cache_v[l], (0, 2, 3, 1))
    att_s = _sample_attn(qs, ks, vs, cache_kt, cache_vt, *_sample_bias_tables(rel_bias, n_buf, steps), steps)
    y_sample = _dense(xs, att_s, rnn_s, dense_w).reshape(db, steps, D_MODEL)

    kv_p = (1, b, s, N_HEADS, HEAD_DIM)
    kv_s = (1, db, steps, N_HEADS, HEAD_DIM)
    return (y_prompt, y_sample,
            k.reshape(kv_p), v.reshape(kv_p), conv_p[None], h_p.reshape(1, b, LRU_WIDTH),
            ks.reshape(kv_s), vs.reshape(kv_s), conv_s[None], h_s[None])
```

```python
import functools
import math

import numpy as np
import jax
import jax.numpy as jnp
from jax import lax
from jax.experimental import pallas as pl
from jax.experimental.pallas import tpu as pltpu

F32 = jnp.float32
BF16 = jnp.bfloat16

D_MODEL = 1024
ATT_WIDTH = 512
LRU_WIDTH = 512
HEAD_DIM = 64
N_HEADS = 8
N_LRU_BLOCKS = 8
LRU_BLOCK = 64
CONV_WIDTH = 4
LRU_C = 8.0
D_FF = 4096
SPAN = 128
DILATIONS = (1, 4, 16)
WIN_MAX = 2048
N_BUCKETS = 32
NORM_EPS = 1e-6
NEG_INF = -1e30
Q_SCALE = HEAD_DIM ** -0.5

SUBLANES = 8
LANES = 128
HEADS_PER_TILE = LANES // HEAD_DIM
VMEM_LIMIT = 56 * 1024 * 1024

TOKEN_TILE = 512
FF_CHUNK = 1024
SAMPLE_PAD = 128


def _dot(a, b):
    return jnp.dot(a, b, preferred_element_type=F32)


def _dot_nt(a, b):
    return lax.dot_general(a, b, (((1,), (1,)), ((), ())), preferred_element_type=F32)


def _rms(x, g):
    return x * lax.rsqrt(jnp.mean(x * x, axis=-1, keepdims=True) + NORM_EPS) * g


def _const_spec(shape):
    nd = len(shape)
    return pl.BlockSpec(shape, lambda *_: (0,) * nd, pipeline_mode=pl.Buffered(1))


def _lru_gates(xc, wg_ref, ba_ref, bx_ref, lam_ref, a_sc, u_sc):
    xcb = xc.astype(BF16)
    lam = lam_ref[...]
    softplus_neg_lam = jnp.maximum(-lam, 0.0) + jnp.log1p(jnp.exp(-jnp.abs(lam)))
    for p in range(LRU_WIDTH // LANES):
        cols = slice(p * LANES, (p + 1) * LANES)
        g = _dot(xcb[:, cols], wg_ref[p])
        r = jax.nn.sigmoid(g[:, :LANES] + ba_ref[:, cols])
        gi = jax.nn.sigmoid(g[:, LANES:] + bx_ref[:, cols])
        log_a = (-LRU_C) * r * softplus_neg_lam[:, cols]
        a = jnp.exp(log_a)
        a_sc[:, cols] = a
        u_sc[:, cols] = jnp.sqrt(-jnp.tanh(log_a) * (a * a + 1.0)) * (gi * xc[:, cols])


def _group_scan(a, u, row):
    for s in (1, 2, 4):
        keep = row >= s
        u = jnp.where(keep, a * pltpu.roll(u, s, 0) + u, u)
        a = jnp.where(keep, a * pltpu.roll(a, s, 0), a)
    return a, u


def _prompt_mix_kernel(x_ref, g1_ref, win_ref, cw_ref, cb_ref, wg_ref, ba_ref, bx_ref, lam_ref, grnn_ref,
                       q_ref, k_ref, v_ref, rnn_ref, conv_ref, h_ref,
                       xr_ext, a_sc, u_sc, hs_sc, h_carry):
    t = x_ref.shape[0]

    @pl.when(pl.program_id(1) == 0)
    def _():
        xr_ext[0:SUBLANES, :] = jnp.zeros((SUBLANES, LRU_WIDTH), F32)
        h_carry[...] = jnp.zeros_like(h_carry)

    n = _rms(x_ref[...], g1_ref[...]).astype(BF16)
    q_ref[...] = _dot(n, win_ref[:, 0:ATT_WIDTH])
    k_ref[...] = _dot(n, win_ref[:, ATT_WIDTH:2 * ATT_WIDTH])
    v_ref[...] = _dot(n, win_ref[:, 2 * ATT_WIDTH:3 * ATT_WIDTH])
    xr = _dot(n, win_ref[:, 3 * ATT_WIDTH:3 * ATT_WIDTH + LRU_WIDTH])

    xr_ext[SUBLANES:SUBLANES + t, :] = xr
    xc = cb_ref[...] + cw_ref[CONV_WIDTH - 1:CONV_WIDTH, :] * xr
    for back in range(1, CONV_WIDTH):
        xc = xc + cw_ref[CONV_WIDTH - 1 - back:CONV_WIDTH - back, :] * xr_ext[pl.ds(SUBLANES - back, t), :]
    xr_ext[0:SUBLANES, :] = xr[t - SUBLANES:, :]
    conv_ref[...] = xr[t - (CONV_WIDTH - 1):, :]

    _lru_gates(xc, wg_ref, ba_ref, bx_ref, lam_ref, a_sc, u_sc)

    row = lax.broadcasted_iota(jnp.int32, (SUBLANES, LRU_WIDTH), 0)

    def scan_group(g, h):
        rows = pl.ds(pl.multiple_of(g * SUBLANES, SUBLANES), SUBLANES)
        a, u = _group_scan(a_sc[rows, :], u_sc[rows, :], row)
        hs = a * h + u
        hs_sc[rows, :] = hs
        return jnp.broadcast_to(hs[SUBLANES - 1:SUBLANES, :], hs.shape)

    h_last = lax.fori_loop(0, t // SUBLANES, scan_group, h_carry[...], unroll=True)
    h_carry[...] = h_last
    h_ref[...] = h_last[0:1, :]

    gr = _dot(n, win_ref[:, 3 * ATT_WIDTH + LRU_WIDTH:])
    rnn_ref[...] = _rms(hs_sc[...] * jax.nn.gelu(gr), grnn_ref[...]).astype(BF16)


def _sample_mix_kernel(x_ref, st_ref, h0_ref, g1_ref, win_ref, cw_ref, cb_ref, wg_ref, ba_ref, bx_ref, lam_ref,
                       grnn_ref, q_ref, k_ref, v_ref, rnn_ref, conv_ref, h_ref,
                       xr_sc, xc_sc, a_sc, u_sc, hs_sc):
    t = x_ref.shape[0]
    n_seq = t // SUBLANES
    n = _rms(x_ref[...], g1_ref[...]).astype(BF16)
    q_ref[...] = _dot(n, win_ref[:, 0:ATT_WIDTH])
    k_ref[...] = _dot(n, win_ref[:, ATT_WIDTH:2 * ATT_WIDTH])
    v_ref[...] = _dot(n, win_ref[:, 2 * ATT_WIDTH:3 * ATT_WIDTH])
    xr_sc[...] = _dot(n, win_ref[:, 3 * ATT_WIDTH:3 * ATT_WIDTH + LRU_WIDTH])

    row = lax.broadcasted_iota(jnp.int32, (SUBLANES, LRU_WIDTH), 0)
    n_state = CONV_WIDTH - 1

    def conv_group(b, carry):
        rows = pl.ds(pl.multiple_of(b * SUBLANES, SUBLANES), SUBLANES)
        xg = xr_sc[rows, :]
        sg = st_ref[rows, :]
        xc = cb_ref[...] + cw_ref[CONV_WIDTH - 1:CONV_WIDTH, :] * xg
        for back in range(1, CONV_WIDTH):
            prev = jnp.where(row >= back, pltpu.roll(xg, back, 0),
                             pltpu.roll(sg, (back - n_state) % SUBLANES, 0))
            xc = xc + cw_ref[CONV_WIDTH - 1 - back:CONV_WIDTH - back, :] * prev
        xc_sc[rows, :] = xc
        conv_ref[b] = xg[SUBLANES - n_state:, :]
        return carry

    lax.fori_loop(0, n_seq, conv_group, 0)

    _lru_gates(xc_sc[...], wg_ref, ba_ref, bx_ref, lam_ref, a_sc, u_sc)

    def scan_group(b, carry):
        rows = pl.ds(pl.multiple_of(b * SUBLANES, SUBLANES), SUBLANES)
        a, u = _group_scan(a_sc[rows, :], u_sc[rows, :], row)
        hs = a * h0_ref[pl.ds(b, 1), :] + u
        hs_sc[rows, :] = hs
        h_ref[pl.ds(b, 1), :] = hs[SUBLANES - 1:SUBLANES, :]
        return carry

    lax.fori_loop(0, n_seq, scan_group, 0)

    gr = _dot(n, win_ref[:, 3 * ATT_WIDTH + LRU_WIDTH:])
    rnn_ref[...] = _rms(hs_sc[...] * jax.nn.gelu(gr), grnn_ref[...]).astype(BF16)


def _mix_weight_specs():
    return [
        _const_spec((1, D_MODEL)),
        _const_spec((D_MODEL, 3 * ATT_WIDTH + 2 * LRU_WIDTH)),
        _const_spec((CONV_WIDTH, LRU_WIDTH)),
        _const_spec((1, LRU_WIDTH)),
        _const_spec((LRU_WIDTH // LANES, LANES, 2 * LANES)),
        _const_spec((1, LRU_WIDTH)),
        _const_spec((1, LRU_WIDTH)),
        _const_spec((1, LRU_WIDTH)),
        _const_spec((1, LRU_WIDTH)),
    ]


def _prompt_mix(x, weights):
    b, s, _ = x.shape
    t = TOKEN_TILE
    tok = lambda width: pl.BlockSpec((None, t, width), lambda i, j: (i, j, 0))
    per_seq = lambda rows: pl.BlockSpec((None, rows, LRU_WIDTH), lambda i, j: (i, 0, 0))
    return pl.pallas_call(
        _prompt_mix_kernel,
        grid=(b, s // t),
        in_specs=[tok(D_MODEL)] + _mix_weight_specs(),
        out_specs=[tok(ATT_WIDTH), tok(ATT_WIDTH), tok(ATT_WIDTH), tok(LRU_WIDTH),
                   per_seq(CONV_WIDTH - 1), per_seq(1)],
        out_shape=[jax.ShapeDtypeStruct((b, s, ATT_WIDTH), F32)] * 3
        + [jax.ShapeDtypeStruct((b, s, LRU_WIDTH), BF16),
           jax.ShapeDtypeStruct((b, CONV_WIDTH - 1, LRU_WIDTH), F32),
           jax.ShapeDtypeStruct((b, 1, LRU_WIDTH), F32)],
        scratch_shapes=[pltpu.VMEM((t + SUBLANES, LRU_WIDTH), F32)] + [pltpu.VMEM((t, LRU_WIDTH), F32)] * 3
        + [pltpu.VMEM((SUBLANES, LRU_WIDTH), F32)],
        compiler_params=pltpu.CompilerParams(dimension_semantics=("arbitrary", "arbitrary"),
                                             vmem_limit_bytes=VMEM_LIMIT),
        name="prompt_mix",
    )(x, *weights)


def _sample_mix(x, state_rows, h0, weights):
    rows = x.shape[0]
    t = min(TOKEN_TILE, rows)
    n_seq = t // SUBLANES
    tok = lambda width: pl.BlockSpec((t, width), lambda i: (i, 0))
    return pl.pallas_call(
        _sample_mix_kernel,
        grid=(rows // t,),
        in_specs=[tok(D_MODEL), tok(LRU_WIDTH), pl.BlockSpec((n_seq, LRU_WIDTH), lambda i: (i, 0))]
        + _mix_weight_specs(),
        out_specs=[tok(ATT_WIDTH), tok(ATT_WIDTH), tok(ATT_WIDTH), tok(LRU_WIDTH),
                   pl.BlockSpec((n_seq, CONV_WIDTH - 1, LRU_WIDTH), lambda i: (i, 0, 0)),
                   pl.BlockSpec((n_seq, LRU_WIDTH), lambda i: (i, 0))],
        out_shape=[jax.ShapeDtypeStruct((rows, ATT_WIDTH), F32)] * 3
        + [jax.ShapeDtypeStruct((rows, LRU_WIDTH), BF16),
           jax.ShapeDtypeStruct((rows // SUBLANES, CONV_WIDTH - 1, LRU_WIDTH), F32),
           jax.ShapeDtypeStruct((rows // SUBLANES, LRU_WIDTH), F32)],
        scratch_shapes=[pltpu.VMEM((t, LRU_WIDTH), F32)] * 5,
        compiler_params=pltpu.CompilerParams(dimension_semantics=("arbitrary",), vmem_limit_bytes=VMEM_LIMIT),
        name="sample_mix",
    )(x, state_rows, h0, *weights)


def _prompt_attn_kernel(q_ref, k_ref, v_ref, brow_ref, o_ref, tab_sc, pv_sc, m_sc, den_sc):
    s_len = q_ref.shape[0]
    n_branch = len(DILATIONS)
    qi = lax.broadcasted_iota(jnp.int32, (SPAN, SPAN), 0)
    ki = lax.broadcasted_iota(jnp.int32, (SPAN, SPAN), 1)
    low_head = ki < HEAD_DIM

    for g in range(n_branch):
        for h in range(HEADS_PER_TILE):
            rows = slice(h * SPAN, (h + 1) * SPAN)
            prev = pltpu.roll(jnp.broadcast_to(brow_ref[g, h, 0:1, :], (SPAN, SPAN)), 0, 1, stride=1, stride_axis=0)
            cur = pltpu.roll(jnp.broadcast_to(brow_ref[g, h, 1:2, :], (SPAN, SPAN)), 0, 1, stride=1, stride_axis=0)
            tab_sc[g, rows, 0:SPAN] = jnp.where(ki >= qi, prev, NEG_INF)
            tab_sc[g, rows, SPAN:] = jnp.where(ki <= qi, cur, NEG_INF)

    def attend(branch, q_rows, k_rows, first):
        n_keys = SPAN if first else 2 * SPAN
        q = q_ref[q_rows, :] * Q_SCALE
        q2 = jnp.concatenate([jnp.where(low_head, q, 0.0), jnp.where(low_head, 0.0, q)], axis=0).astype(BF16)
        kk = k_ref[k_rows, :].astype(BF16)
        v_aug = jnp.concatenate([v_ref[k_rows, :], jnp.ones((n_keys, LANES), F32)], axis=1).astype(BF16)
        table = tab_sc[branch, :, SPAN:] if first else tab_sc[branch]
        logits = _dot_nt(q2, kk) + table
        m = jnp.max(logits, axis=-1, keepdims=True)
        p = jnp.exp(logits - m).astype(BF16)
        r = _dot(p, v_aug)
        m_b = jnp.broadcast_to(m, (2 * SPAN, LANES))
        pv_sc[branch, q_rows, :] = jnp.where(low_head, r[:SPAN, :LANES], r[SPAN:, :LANES])
        den_sc[branch, q_rows, :] = jnp.where(low_head, r[:SPAN, LANES:], r[SPAN:, LANES:])
        m_sc[branch, q_rows, :] = jnp.where(low_head, m_b[:SPAN], m_b[SPAN:])

    for branch, dil in enumerate(DILATIONS):
        n_blocks = s_len // dil // SPAN
        if dil == 1:
            attend(branch, pl.ds(0, SPAN), pl.ds(0, SPAN), True)
            group = 5
            assert (n_blocks - 1) % group == 0

            def contiguous(it, carry, branch=branch, group=group):
                for u in range(group):
                    q0 = pl.multiple_of((1 + it * group + u) * SPAN, SPAN)
                    attend(branch, pl.ds(q0, SPAN), pl.ds(pl.multiple_of(q0 - SPAN, SPAN), 2 * SPAN), False)
                return carry
            lax.fori_loop(0, (n_blocks - 1) // group, contiguous, 0)
        elif n_blocks > 1:
            for c in range(dil):
                rows = pl.ds(c, SPAN, stride=dil)
                attend(branch, rows, rows, True)

            def strided(i, carry, branch=branch, dil=dil):
                for c in range(dil):
                    q0 = c + i * (SPAN * dil)
                    attend(branch, pl.ds(q0, SPAN, stride=dil), pl.ds(q0 - SPAN * dil, 2 * SPAN, stride=dil), False)
                return carry
            lax.fori_loop(1, n_blocks, strided, 0, unroll=True)
        else:
            group = 8
            assert dil % group == 0

            def single_block(it, carry, branch=branch, dil=dil, group=group):
                for u in range(group):
                    rows = pl.ds(it * group + u, SPAN, stride=dil)
                    attend(branch, rows, rows, True)
                return carry
            lax.fori_loop(0, dil // group, single_block, 0)

    chunk = 2 * SPAN

    def merge(i, carry):
        rows = pl.ds(pl.multiple_of(i * chunk, chunk), chunk)
        ms = [m_sc[g, rows, :] for g in range(n_branch)]
        m_all = jnp.maximum(jnp.maximum(ms[0], ms[1]), ms[2])
        num = jnp.zeros((chunk, LANES), F32)
        den = jnp.zeros((chunk, LANES), F32)
        for g in range(n_branch):
            w = jnp.exp(ms[g] - m_all)
            num = num + w * pv_sc[g, rows, :]
            den = den + w * den_sc[g, rows, :]
        o_ref[rows, :] = num / den
        return carry

    lax.fori_loop(0, s_len // chunk, merge, 0)


def _prompt_attn(q, k, v, bias_rows):
    b, s, _ = q.shape
    n_branch = len(DILATIONS)
    head_pair = pl.BlockSpec((None, s, LANES), lambda i, j: (i, 0, j))
    return pl.pallas_call(
        _prompt_attn_kernel,
        grid=(b, ATT_WIDTH // LANES),
        in_specs=[head_pair, head_pair, head_pair,
                  pl.BlockSpec((n_branch, HEADS_PER_TILE, 2, SPAN), lambda i, j: (0, j, 0, 0))],
        out_specs=head_pair,
        out_shape=jax.ShapeDtypeStruct((b, s, ATT_WIDTH), F32),
        scratch_shapes=[pltpu.VMEM((n_branch, HEADS_PER_TILE * SPAN, 2 * SPAN), F32)]
        + [pltpu.VMEM((n_branch, s, LANES), F32)] * 3,
        compiler_params=pltpu.CompilerParams(dimension_semantics=("arbitrary", "arbitrary"),
                                             vmem_limit_bytes=VMEM_LIMIT),
        name="prompt_attn",
    )(q, k, v, bias_rows)


def _sample_attn_stages(q_ref, kn_ref, vn_ref, ckt_ref, cvt_ref, tc_ref, tn_ref, o_ref):
    t = q_ref.shape[0]
    n_buf = ckt_ref.shape[-1]
    n_branch = len(DILATIONS)
    head_of_lane = lax.broadcasted_iota(jnp.int32, (t, ATT_WIDTH), 1) // HEAD_DIM
    q = q_ref[...] * Q_SCALE
    q_bd = jnp.concatenate([jnp.where(head_of_lane == h, q, 0.0) for h in range(N_HEADS)], axis=0).astype(BF16)
    pad = jnp.zeros((SAMPLE_PAD - t, ATT_WIDTH), F32)
    k_new = jnp.concatenate([kn_ref[...], pad], axis=0).astype(BF16)
    v_new = jnp.concatenate([vn_ref[...], pad], axis=0).astype(BF16)

    kt = ckt_ref[...].reshape(ATT_WIDTH, n_buf).astype(BF16)
    s_c = _dot(q_bd, kt)
    s_n = _dot_nt(q_bd, k_new)
    yield
    m = None
    for g in range(n_branch):
        m_g = jnp.maximum(jnp.max(s_c + tc_ref[g], axis=-1, keepdims=True),
                          jnp.max(s_n + tn_ref[g], axis=-1, keepdims=True))
        m = m_g if m is None else jnp.maximum(m, m_g)
    p_c = jnp.exp(s_c + tc_ref[0] - m)
    p_n = jnp.exp(s_n + tn_ref[0] - m)
    for g in range(1, n_branch):
        p_c = p_c + jnp.exp(s_c + tc_ref[g] - m)
        p_n = p_n + jnp.exp(s_n + tn_ref[g] - m)
    den = jnp.sum(p_c, axis=-1, keepdims=True) + jnp.sum(p_n, axis=-1, keepdims=True)
    yield
    vt = cvt_ref[...].reshape(ATT_WIDTH, n_buf).astype(BF16)
    out = (_dot_nt(p_c.astype(BF16), vt) + _dot(p_n.astype(BF16), v_new)) / den
    att = jnp.zeros((t, ATT_WIDTH), F32)
    for h in range(N_HEADS):
        att = att + jnp.where(head_of_lane == h, out[h * t:(h + 1) * t, :], 0.0)
    o_ref[...] = att


def _sample_attn_specs(n_buf, steps, cached_tables, new_tables):
    tok = pl.BlockSpec((steps, ATT_WIDTH), lambda i: (i, 0))
    cache = pl.BlockSpec((None, N_HEADS, HEAD_DIM, n_buf), lambda i: (i, 0, 0, 0))
    return [tok, tok, tok, cache, cache, _const_spec(cached_tables.shape), _const_spec(new_tables.shape)], tok


def _dense_stages(x_ref, att_ref, rnn_ref, gatt_ref, woa_ref, wor_ref, g2_ref, w1_ref, w2_ref, gf_ref, y_ref):
    att_n = _rms(att_ref[...], gatt_ref[...]).astype(BF16)
    x1 = x_ref[...] + _dot(att_n, woa_ref[...]) + _dot(rnn_ref[...], wor_ref[...])
    n2 = _rms(x1, g2_ref[...]).astype(BF16)
    yield
    mlp = None
    for c in range(D_FF // FF_CHUNK):
        cols = slice(c * FF_CHUNK, (c + 1) * FF_CHUNK)
        hmid = jnp.maximum(_dot(n2, w1_ref[:, cols]), 0.0)
        part = _dot((hmid * hmid).astype(BF16), w2_ref[cols, :])
        mlp = part if mlp is None else mlp + part
        yield
    y_ref[...] = _rms(x1 + mlp, gf_ref[...])


def _interleave(*stage_generators):
    pending = list(stage_generators)
    while pending:
        pending = [g for g in pending if next(g, StopIteration) is not StopIteration]


def _dense_kernel(*refs):
    _interleave(_dense_stages(*refs))


N_DENSE_IN = 10


def _dense_and_sample_attn_kernel(*refs):
    dense_in, attn_in = refs[:N_DENSE_IN], refs[N_DENSE_IN:-2]
    y_ref, att_s_ref = refs[-2:]
    _interleave(_sample_attn_stages(*attn_in, att_s_ref), _dense_stages(*dense_in, y_ref))


def _dense_specs(t):
    tok = lambda width: pl.BlockSpec((t, width), lambda i: (i, 0))
    in_specs = [tok(D_MODEL), tok(ATT_WIDTH), tok(LRU_WIDTH),
                _const_spec((1, ATT_WIDTH)), _const_spec((ATT_WIDTH, D_MODEL)), _const_spec((LRU_WIDTH, D_MODEL)),
                _const_spec((1, D_MODEL)), _const_spec((D_MODEL, D_FF)), _const_spec((D_FF, D_MODEL)),
                _const_spec((1, D_MODEL))]
    assert len(in_specs) == N_DENSE_IN
    return in_specs, tok(D_MODEL)


def _dense(x, att, rnn, weights):
    rows = x.shape[0]
    t = min(TOKEN_TILE, rows)
    in_specs, out_spec = _dense_specs(t)
    return pl.pallas_call(
        _dense_kernel,
        grid=(rows // t,),
        in_specs=in_specs,
        out_specs=out_spec,
        out_shape=jax.ShapeDtypeStruct((rows, D_MODEL), F32),
        compiler_params=pltpu.CompilerParams(dimension_semantics=("arbitrary",), vmem_limit_bytes=VMEM_LIMIT),
        name="dense_tail",
    )(x, att, rnn, *weights)


def _dense_and_sample_attn(x, att, rnn, weights, q_s, k_new, v_new, cache_kt, cache_vt, cached_tables, new_tables,
                           steps):
    rows = x.shape[0]
    n_seq = q_s.shape[0] // steps
    t = rows // n_seq
    assert t * n_seq == rows and t % SUBLANES == 0
    dense_in, dense_out = _dense_specs(t)
    attn_in, attn_out = _sample_attn_specs(cache_kt.shape[-1], steps, cached_tables, new_tables)
    return pl.pallas_call(
        _dense_and_sample_attn_kernel,
        grid=(n_seq,),
        in_specs=dense_in + attn_in,
        out_specs=[dense_out, attn_out],
        out_shape=[jax.ShapeDtypeStruct((rows, D_MODEL), F32), jax.ShapeDtypeStruct(q_s.shape, F32)],
        compiler_params=pltpu.CompilerParams(dimension_semantics=("arbitrary",), vmem_limit_bytes=VMEM_LIMIT),
        name="dense_tail_sample_attn",
    )(x, att, rnn, *weights, q_s, k_new, v_new, cache_kt, cache_vt, cached_tables, new_tables)


def _t5_bucket(dist):
    max_exact = N_BUCKETS // 2
    d_f = jnp.maximum(dist, max_exact).astype(F32)
    large = max_exact + (jnp.log(d_f / max_exact) / math.log(WIN_MAX / max_exact)
                         * (N_BUCKETS - max_exact)).astype(jnp.int32)
    large = jnp.minimum(large, N_BUCKETS - 1)
    return jnp.where(dist < max_exact, dist, large)


def _branch_bias(rel_bias, dil):
    dist = jnp.arange(SPAN + 1, dtype=jnp.int32) * dil
    return rel_bias[_t5_bucket(dist)].astype(F32).T


def _prompt_bias_rows(rel_bias):
    rows = []
    for dil in DILATIONS:
        bias = _branch_bias(rel_bias, dil)
        prev = bias[:, SPAN:0:-1]
        cur = jnp.concatenate([bias[:, 0:1], bias[:, SPAN - 1:0:-1]], axis=1)
        rows.append(jnp.stack([prev, cur], axis=1))
    return jnp.stack(rows)


def _sample_bias_tables(rel_bias, n_buf, steps):
    cached, new = [], []
    pos = np.arange(n_buf)
    new_row = np.arange(SAMPLE_PAD)
    delta_new = np.arange(steps)[:, None] - new_row[None, :]
    for dil in DILATIONS:
        bias = _branch_bias(rel_bias, dil)
        every = [bias[:, SPAN:0:-1]] + [jnp.full((N_HEADS, SPAN), NEG_INF, F32)] * (dil - 1)
        row0 = jnp.concatenate([jnp.full((N_HEADS, n_buf - SPAN * dil), NEG_INF, F32),
                                jnp.stack(every, axis=-1).reshape(N_HEADS, SPAN * dil)], axis=1)
        rows = [jnp.where(pos[None, :] >= t, jnp.roll(row0, t, axis=1), NEG_INF) for t in range(steps)]
        cached.append(jnp.stack(rows, axis=1).reshape(N_HEADS * steps, n_buf))
        valid = (new_row[None, :] < steps) & (delta_new >= 0) & (delta_new % dil == 0)
        j_of = np.where(valid, delta_new // dil, -1)
        one_hot = (j_of[None] == np.arange(steps)[:, None, None]).astype(np.float32)
        vals = jnp.sum(bias[:, :steps, None, None] * one_hot[None], axis=1)
        new.append(jnp.where(valid[None], vals, NEG_INF).reshape(N_HEADS * steps, SAMPLE_PAD))
    return jnp.stack(cached), jnp.stack(new)


def _paired_gate_weights(gate_a_w, gate_x_w):
    def pair_diag(w):
        z = jnp.zeros((LRU_BLOCK, LRU_BLOCK), w.dtype)
        return jnp.stack([jnp.block([[w[2 * p], z], [z, w[2 * p + 1]]]) for p in range(N_LRU_BLOCKS // 2)])
    return jnp.concatenate([pair_diag(gate_a_w), pair_diag(gate_x_w)], axis=-1).astype(BF16)


def kernel(x_prompt, x_sample, cache_k, cache_v, state_conv, state_h, norm1_g, w_in, rel_bias, conv_w, conv_b,
           gate_a_w, gate_a_b, gate_x_w, gate_x_b, lru_lambda, att_out_g, rnn_out_g, w_out, norm2_g, w_mlp_in,
           w_mlp_out, final_g):
    depth = w_in.shape[0]
    assert depth == 1, "the final norm is fused into the single layer's dense kernel"
    b, s, _ = x_prompt.shape
    db, steps, _ = x_sample.shape
    n_buf = cache_k.shape[2]
    assert steps == SUBLANES and s == WIN_MAX and n_buf == WIN_MAX
    l = 0
    row = lambda v: v.reshape(1, -1).astype(F32)

    mix_w = (row(norm1_g[l]), w_in[l].astype(BF16), conv_w[l], row(conv_b[l]),
             _paired_gate_weights(gate_a_w[l], gate_x_w[l]), row(gate_a_b[l]), row(gate_x_b[l]),
             row(lru_lambda[l]), row(rnn_out_g[l]))
    dense_w = (row(att_out_g[l]), w_out[l, :ATT_WIDTH].astype(BF16), w_out[l, ATT_WIDTH:].astype(BF16),
               row(norm2_g[l]), w_mlp_in[l].astype(BF16), w_mlp_out[l].astype(BF16), row(final_g))

    xs = x_sample.reshape(db * steps, D_MODEL)
    state_rows = jnp.pad(state_conv[l], ((0, 0), (0, SUBLANES - (CONV_WIDTH - 1)), (0, 0))).reshape(db * steps, LRU_WIDTH)
    qs, ks, vs, rnn_s, conv_s, h_s = _sample_mix(xs, state_rows, state_h[l], mix_w)
    cache_kt = jnp.transpose(cache_k[l], (0, 2, 3, 1))
    cache_vt = jnp.transpose(cache_v[l], (0, 2, 3, 1))

    q, k, v, rnn, conv_p, h_p = _prompt_mix(x_prompt, mix_w)
    att = _prompt_attn(q, k, v, _prompt_bias_rows(rel_bias))
    y_prompt, att_s = _dense_and_sample_attn(
        x_prompt.reshape(b * s, D_MODEL), att.reshape(b * s, ATT_WIDTH), rnn.reshape(b * s, LRU_WIDTH), dense_w,
        qs, ks, vs, cache_kt, cache_vt, *_sample_bias_tables(rel_bias, n_buf, steps), steps)
    y_prompt = y_prompt.reshape(b, s, D_MODEL)
    y_sample = _dense(xs, att_s, rnn_s, dense_w).reshape(db, steps, D_MODEL)

    kv_p = (1, b, s, N_HEADS, HEAD_DIM)
    kv_s = (1, db, steps, N_HEADS, HEAD_DIM)
    return (y_prompt, y_sample,
            k.reshape(kv_p), v.reshape(kv_p), conv_p[None], h_p.reshape(1, b, LRU_WIDTH),
            ks.reshape(kv_s), vs.reshape(kv_s), conv_s[None], h_s[None])
```

```python
import functools
import math

import numpy as np
import jax
import jax.numpy as jnp
from jax import lax
from jax.experimental import pallas as pl
from jax.experimental.pallas import tpu as pltpu

F32 = jnp.float32
BF16 = jnp.bfloat16

D_MODEL = 1024
ATT_WIDTH = 512
LRU_WIDTH = 512
HEAD_DIM = 64
N_HEADS = 8
N_LRU_BLOCKS = 8
LRU_BLOCK = 64
CONV_WIDTH = 4
LRU_C = 8.0
D_FF = 4096
SPAN = 128
DILATIONS = (1, 4, 16)
WIN_MAX = 2048
N_BUCKETS = 32
NORM_EPS = 1e-6
NEG_INF = -1e30
Q_SCALE = HEAD_DIM ** -0.5

SUBLANES = 8
LANES = 128
HEADS_PER_TILE = LANES // HEAD_DIM
VMEM_LIMIT = 56 * 1024 * 1024

TOKEN_TILE = 512
PROMPT_MIX_TILE = 1024
PROMPT_SUB_TILE = 256
FF_CHUNK = 1024
SAMPLE_PAD = 128


def _dot(a, b):
    return jnp.dot(a, b, preferred_element_type=F32)


def _dot_nt(a, b):
    return lax.dot_general(a, b, (((1,), (1,)), ((), ())), preferred_element_type=F32)


def _rms(x, g):
    return x * lax.rsqrt(jnp.mean(x * x, axis=-1, keepdims=True) + NORM_EPS) * g


def _const_spec(shape):
    nd = len(shape)
    return pl.BlockSpec(shape, lambda *_: (0,) * nd, pipeline_mode=pl.Buffered(1))


def _lru_gates(xc, wg_ref, ba_ref, bx_ref, lam_ref, a_sc, u_sc):
    xcb = xc.astype(BF16)
    lam = lam_ref[...]
    softplus_neg_lam = jnp.maximum(-lam, 0.0) + jnp.log1p(jnp.exp(-jnp.abs(lam)))
    for p in range(LRU_WIDTH // LANES):
        cols = slice(p * LANES, (p + 1) * LANES)
        g = _dot(xcb[:, cols], wg_ref[p])
        r = jax.nn.sigmoid(g[:, :LANES] + ba_ref[:, cols])
        gi = jax.nn.sigmoid(g[:, LANES:] + bx_ref[:, cols])
        log_a = (-LRU_C) * r * softplus_neg_lam[:, cols]
        a = jnp.exp(log_a)
        a_sc[:, cols] = a
        u_sc[:, cols] = jnp.sqrt(-jnp.tanh(log_a) * (a * a + 1.0)) * (gi * xc[:, cols])


def _group_scan(a, u, row):
    for s in (1, 2, 4):
        keep = row >= s
        u = jnp.where(keep, a * pltpu.roll(u, s, 0) + u, u)
        a = jnp.where(keep, a * pltpu.roll(a, s, 0), a)
    return a, u


def _prompt_mix_kernel(x_ref, g1_ref, win_ref, cw_ref, cb_ref, wg_ref, ba_ref, bx_ref, lam_ref, grnn_ref,
                       q_ref, k_ref, v_ref, rnn_ref, conv_ref, h_ref,
                       xr_ext, gr_sc, a_sc, u_sc, hs_sc, h_carry):
    t = x_ref.shape[0]
    sub = PROMPT_SUB_TILE
    n_sub = t // sub

    @pl.when(pl.program_id(1) == 0)
    def _():
        xr_ext[0:SUBLANES, :] = jnp.zeros((SUBLANES, LRU_WIDTH), F32)
        h_carry[...] = jnp.zeros_like(h_carry)

    row = lax.broadcasted_iota(jnp.int32, (SUBLANES, LRU_WIDTH), 0)

    def project(s):
        rows = slice(s * sub, (s + 1) * sub)
        n = _rms(x_ref[rows, :], g1_ref[...]).astype(BF16)
        q_ref[rows, :] = _dot(n, win_ref[:, 0:ATT_WIDTH])
        k_ref[rows, :] = _dot(n, win_ref[:, ATT_WIDTH:2 * ATT_WIDTH])
        v_ref[rows, :] = _dot(n, win_ref[:, 2 * ATT_WIDTH:3 * ATT_WIDTH])
        xr_ext[SUBLANES + s * sub:SUBLANES + (s + 1) * sub, :] = _dot(
            n, win_ref[:, 3 * ATT_WIDTH:3 * ATT_WIDTH + LRU_WIDTH])
        gr_sc[rows, :] = _dot(n, win_ref[:, 3 * ATT_WIDTH + LRU_WIDTH:])

    def recur(s, h):
        rows = slice(s * sub, (s + 1) * sub)
        xc = cb_ref[...]
        for back in range(CONV_WIDTH):
            xc = xc + (cw_ref[CONV_WIDTH - 1 - back:CONV_WIDTH - back, :]
                       * xr_ext[pl.ds(SUBLANES + s * sub - back, sub), :])
        _lru_gates(xc, wg_ref, ba_ref, bx_ref, lam_ref, a_sc.at[rows], u_sc.at[rows])
        for g in range(sub // SUBLANES):
            grp = slice(s * sub + g * SUBLANES, s * sub + (g + 1) * SUBLANES)
            a, u = _group_scan(a_sc[grp, :], u_sc[grp, :], row)
            hs = a * h + u
            hs_sc[grp, :] = hs
            h = jnp.broadcast_to(hs[SUBLANES - 1:SUBLANES, :], hs.shape)
        rnn_ref[rows, :] = _rms(hs_sc[rows, :] * jax.nn.gelu(gr_sc[rows, :]), grnn_ref[...]).astype(BF16)
        return h

    h = h_carry[...]
    project(0)
    for s in range(1, n_sub):
        project(s)
        h = recur(s - 1, h)
    h = recur(n_sub - 1, h)
    h_carry[...] = h
    h_ref[...] = h[0:1, :]
    conv_ref[...] = xr_ext[SUBLANES + t - (CONV_WIDTH - 1):SUBLANES + t, :]
    xr_ext[0:SUBLANES, :] = xr_ext[t:t + SUBLANES, :]


def _sample_mix_kernel(x_ref, st_ref, h0_ref, g1_ref, win_ref, cw_ref, cb_ref, wg_ref, ba_ref, bx_ref, lam_ref,
                       grnn_ref, q_ref, k_ref, v_ref, rnn_ref, conv_ref, h_ref,
                       xr_sc, xc_sc, a_sc, u_sc, hs_sc):
    t = x_ref.shape[0]
    n_seq = t // SUBLANES
    n = _rms(x_ref[...], g1_ref[...]).astype(BF16)
    q_ref[...] = _dot(n, win_ref[:, 0:ATT_WIDTH])
    k_ref[...] = _dot(n, win_ref[:, ATT_WIDTH:2 * ATT_WIDTH])
    v_ref[...] = _dot(n, win_ref[:, 2 * ATT_WIDTH:3 * ATT_WIDTH])
    xr_sc[...] = _dot(n, win_ref[:, 3 * ATT_WIDTH:3 * ATT_WIDTH + LRU_WIDTH])

    row = lax.broadcasted_iota(jnp.int32, (SUBLANES, LRU_WIDTH), 0)
    n_state = CONV_WIDTH - 1

    def conv_group(b, carry):
        rows = pl.ds(pl.multiple_of(b * SUBLANES, SUBLANES), SUBLANES)
        xg = xr_sc[rows, :]
        sg = st_ref[rows, :]
        xc = cb_ref[...] + cw_ref[CONV_WIDTH - 1:CONV_WIDTH, :] * xg
        for back in range(1, CONV_WIDTH):
            prev = jnp.where(row >= back, pltpu.roll(xg, back, 0),
                             pltpu.roll(sg, (back - n_state) % SUBLANES, 0))
            xc = xc + cw_ref[CONV_WIDTH - 1 - back:CONV_WIDTH - back, :] * prev
        xc_sc[rows, :] = xc
        conv_ref[b] = xg[SUBLANES - n_state:, :]
        return carry

    lax.fori_loop(0, n_seq, conv_group, 0)

    _lru_gates(xc_sc[...], wg_ref, ba_ref, bx_ref, lam_ref, a_sc, u_sc)

    def scan_group(b, carry):
        rows = pl.ds(pl.multiple_of(b * SUBLANES, SUBLANES), SUBLANES)
        a, u = _group_scan(a_sc[rows, :], u_sc[rows, :], row)
        hs = a * h0_ref[pl.ds(b, 1), :] + u
        hs_sc[rows, :] = hs
        h_ref[pl.ds(b, 1), :] = hs[SUBLANES - 1:SUBLANES, :]
        return carry

    lax.fori_loop(0, n_seq, scan_group, 0)

    gr = _dot(n, win_ref[:, 3 * ATT_WIDTH + LRU_WIDTH:])
    rnn_ref[...] = _rms(hs_sc[...] * jax.nn.gelu(gr), grnn_ref[...]).astype(BF16)


def _mix_weight_specs():
    return [
        _const_spec((1, D_MODEL)),
        _const_spec((D_MODEL, 3 * ATT_WIDTH + 2 * LRU_WIDTH)),
        _const_spec((CONV_WIDTH, LRU_WIDTH)),
        _const_spec((1, LRU_WIDTH)),
        _const_spec((LRU_WIDTH // LANES, LANES, 2 * LANES)),
        _const_spec((1, LRU_WIDTH)),
        _const_spec((1, LRU_WIDTH)),
        _const_spec((1, LRU_WIDTH)),
        _const_spec((1, LRU_WIDTH)),
    ]


def _prompt_mix(x, weights):
    b, s, _ = x.shape
    t = PROMPT_MIX_TILE
    assert s % t == 0 and t % PROMPT_SUB_TILE == 0
    tok = lambda width: pl.BlockSpec((None, t, width), lambda i, j: (i, j, 0))
    per_seq = lambda rows: pl.BlockSpec((None, rows, LRU_WIDTH), lambda i, j: (i, 0, 0))
    return pl.pallas_call(
        _prompt_mix_kernel,
        grid=(b, s // t),
        in_specs=[tok(D_MODEL)] + _mix_weight_specs(),
        out_specs=[tok(ATT_WIDTH), tok(ATT_WIDTH), tok(ATT_WIDTH), tok(LRU_WIDTH),
                   per_seq(CONV_WIDTH - 1), per_seq(1)],
        out_shape=[jax.ShapeDtypeStruct((b, s, ATT_WIDTH), F32)] * 3
        + [jax.ShapeDtypeStruct((b, s, LRU_WIDTH), BF16),
           jax.ShapeDtypeStruct((b, CONV_WIDTH - 1, LRU_WIDTH), F32),
           jax.ShapeDtypeStruct((b, 1, LRU_WIDTH), F32)],
        scratch_shapes=[pltpu.VMEM((t + SUBLANES, LRU_WIDTH), F32)] + [pltpu.VMEM((t, LRU_WIDTH), F32)] * 4
        + [pltpu.VMEM((SUBLANES, LRU_WIDTH), F32)],
        compiler_params=pltpu.CompilerParams(dimension_semantics=("arbitrary", "arbitrary"),
                                             vmem_limit_bytes=VMEM_LIMIT),
        name="prompt_mix",
    )(x, *weights)


def _sample_mix(x, state_rows, h0, weights):
    rows = x.shape[0]
    t = min(TOKEN_TILE, rows)
    n_seq = t // SUBLANES
    tok = lambda width: pl.BlockSpec((t, width), lambda i: (i, 0))
    return pl.pallas_call(
        _sample_mix_kernel,
        grid=(rows // t,),
        in_specs=[tok(D_MODEL), tok(LRU_WIDTH), pl.BlockSpec((n_seq, LRU_WIDTH), lambda i: (i, 0))]
        + _mix_weight_specs(),
        out_specs=[tok(ATT_WIDTH), tok(ATT_WIDTH), tok(ATT_WIDTH), tok(LRU_WIDTH),
                   pl.BlockSpec((n_seq, CONV_WIDTH - 1, LRU_WIDTH), lambda i: (i, 0, 0)),
                   pl.BlockSpec((n_seq, LRU_WIDTH), lambda i: (i, 0))],
        out_shape=[jax.ShapeDtypeStruct((rows, ATT_WIDTH), F32)] * 3
        + [jax.ShapeDtypeStruct((rows, LRU_WIDTH), BF16),
           jax.ShapeDtypeStruct((rows // SUBLANES, CONV_WIDTH - 1, LRU_WIDTH), F32),
           jax.ShapeDtypeStruct((rows // SUBLANES, LRU_WIDTH), F32)],
        scratch_shapes=[pltpu.VMEM((t, LRU_WIDTH), F32)] * 5,
        compiler_params=pltpu.CompilerParams(dimension_semantics=("arbitrary",), vmem_limit_bytes=VMEM_LIMIT),
        name="sample_mix",
    )(x, state_rows, h0, *weights)


def _prompt_attn_kernel(q_ref, k_ref, v_ref, brow_ref, o_ref, tab_sc, pv_sc, m_sc, den_sc):
    s_len = q_ref.shape[0]
    n_branch = len(DILATIONS)
    qi = lax.broadcasted_iota(jnp.int32, (SPAN, SPAN), 0)
    ki = lax.broadcasted_iota(jnp.int32, (SPAN, SPAN), 1)
    low_head = ki < HEAD_DIM

    for g in range(n_branch):
        for h in range(HEADS_PER_TILE):
            rows = slice(h * SPAN, (h + 1) * SPAN)
            prev = pltpu.roll(jnp.broadcast_to(brow_ref[g, h, 0:1, :], (SPAN, SPAN)), 0, 1, stride=1, stride_axis=0)
            cur = pltpu.roll(jnp.broadcast_to(brow_ref[g, h, 1:2, :], (SPAN, SPAN)), 0, 1, stride=1, stride_axis=0)
            tab_sc[g, rows, 0:SPAN] = jnp.where(ki >= qi, prev, NEG_INF)
            tab_sc[g, rows, SPAN:] = jnp.where(ki <= qi, cur, NEG_INF)

    def attend(branch, q_rows, k_rows, first):
        n_keys = SPAN if first else 2 * SPAN
        q = q_ref[q_rows, :] * Q_SCALE
        q2 = jnp.concatenate([jnp.where(low_head, q, 0.0), jnp.where(low_head, 0.0, q)], axis=0).astype(BF16)
        kk = k_ref[k_rows, :].astype(BF16)
        v_aug = jnp.concatenate([v_ref[k_rows, :], jnp.ones((n_keys, LANES), F32)], axis=1).astype(BF16)
        table = tab_sc[branch, :, SPAN:] if first else tab_sc[branch]
        logits = _dot_nt(q2, kk) + table
        m = jnp.max(logits, axis=-1, keepdims=True)
        p = jnp.exp(logits - m).astype(BF16)
        r = _dot(p, v_aug)
        m_b = jnp.broadcast_to(m, (2 * SPAN, LANES))
        pv_sc[branch, q_rows, :] = jnp.where(low_head, r[:SPAN, :LANES], r[SPAN:, :LANES])
        den_sc[branch, q_rows, :] = jnp.where(low_head, r[:SPAN, LANES:], r[SPAN:, LANES:])
        m_sc[branch, q_rows, :] = jnp.where(low_head, m_b[:SPAN], m_b[SPAN:])

    for branch, dil in enumerate(DILATIONS):
        n_blocks = s_len // dil // SPAN
        if dil == 1:
            attend(branch, pl.ds(0, SPAN), pl.ds(0, SPAN), True)
            group = 5
            assert (n_blocks - 1) % group == 0

            def contiguous(it, carry, branch=branch, group=group):
                for u in range(group):
                    q0 = pl.multiple_of((1 + it * group + u) * SPAN, SPAN)
                    attend(branch, pl.ds(q0, SPAN), pl.ds(pl.multiple_of(q0 - SPAN, SPAN), 2 * SPAN), False)
                return carry
            lax.fori_loop(0, (n_blocks - 1) // group, contiguous, 0)
        elif n_blocks > 1:
            for c in range(dil):
                rows = pl.ds(c, SPAN, stride=dil)
                attend(branch, rows, rows, True)

            def strided(i, carry, branch=branch, dil=dil):
                for c in range(dil):
                    q0 = c + i * (SPAN * dil)
                    attend(branch, pl.ds(q0, SPAN, stride=dil), pl.ds(q0 - SPAN * dil, 2 * SPAN, stride=dil), False)
                return carry
            lax.fori_loop(1, n_blocks, strided, 0, unroll=True)
        else:
            group = 8
            assert dil % group == 0

            def single_block(it, carry, branch=branch, dil=dil, group=group):
                for u in range(group):
                    rows = pl.ds(it * group + u, SPAN, stride=dil)
                    attend(branch, rows, rows, True)
                return carry
            lax.fori_loop(0, dil // group, single_block, 0)

    chunk = 2 * SPAN

    def merge(i, carry):
        rows = pl.ds(pl.multiple_of(i * chunk, chunk), chunk)
        ms = [m_sc[g, rows, :] for g in range(n_branch)]
        m_all = jnp.maximum(jnp.maximum(ms[0], ms[1]), ms[2])
        num = jnp.zeros((chunk, LANES), F32)
        den = jnp.zeros((chunk, LANES), F32)
        for g in range(n_branch):
            w = jnp.exp(ms[g] - m_all)
            num = num + w * pv_sc[g, rows, :]
            den = den + w * den_sc[g, rows, :]
        o_ref[rows, :] = num / den
        return carry

    lax.fori_loop(0, s_len // chunk, merge, 0)


def _prompt_attn(q, k, v, bias_rows):
    b, s, _ = q.shape
    n_branch = len(DILATIONS)
    head_pair = pl.BlockSpec((None, s, LANES), lambda i, j: (i, 0, j))
    return pl.pallas_call(
        _prompt_attn_kernel,
        grid=(b, ATT_WIDTH // LANES),
        in_specs=[head_pair, head_pair, head_pair,
                  pl.BlockSpec((n_branch, HEADS_PER_TILE, 2, SPAN), lambda i, j: (0, j, 0, 0))],
        out_specs=head_pair,
        out_shape=jax.ShapeDtypeStruct((b, s, ATT_WIDTH), F32),
        scratch_shapes=[pltpu.VMEM((n_branch, HEADS_PER_TILE * SPAN, 2 * SPAN), F32)]
        + [pltpu.VMEM((n_branch, s, LANES), F32)] * 3,
        compiler_params=pltpu.CompilerParams(dimension_semantics=("arbitrary", "arbitrary"),
                                             vmem_limit_bytes=VMEM_LIMIT),
        name="prompt_attn",
    )(q, k, v, bias_rows)


def _sample_attn_stages(q_ref, kn_ref, vn_ref, ckt_ref, cvt_ref, tc_ref, tn_ref, o_ref):
    t = q_ref.shape[0]
    n_buf = ckt_ref.shape[-1]
    n_branch = len(DILATIONS)
    head_of_lane = lax.broadcasted_iota(jnp.int32, (t, ATT_WIDTH), 1) // HEAD_DIM
    q = q_ref[...] * Q_SCALE
    q_bd = jnp.concatenate([jnp.where(head_of_lane == h, q, 0.0) for h in range(N_HEADS)], axis=0).astype(BF16)
    pad = jnp.zeros((SAMPLE_PAD - t, ATT_WIDTH), F32)
    k_new = jnp.concatenate([kn_ref[...], pad], axis=0).astype(BF16)
    v_new = jnp.concatenate([vn_ref[...], pad], axis=0).astype(BF16)

    kt = ckt_ref[...].reshape(ATT_WIDTH, n_buf).astype(BF16)
    s_c = _dot(q_bd, kt)
    s_n = _dot_nt(q_bd, k_new)
    yield
    m = None
    for g in range(n_branch):
        m_g = jnp.maximum(jnp.max(s_c + tc_ref[g], axis=-1, keepdims=True),
                          jnp.max(s_n + tn_ref[g], axis=-1, keepdims=True))
        m = m_g if m is None else jnp.maximum(m, m_g)
    p_c = jnp.exp(s_c + tc_ref[0] - m)
    p_n = jnp.exp(s_n + tn_ref[0] - m)
    for g in range(1, n_branch):
        p_c = p_c + jnp.exp(s_c + tc_ref[g] - m)
        p_n = p_n + jnp.exp(s_n + tn_ref[g] - m)
    den = jnp.sum(p_c, axis=-1, keepdims=True) + jnp.sum(p_n, axis=-1, keepdims=True)
    yield
    vt = cvt_ref[...].reshape(ATT_WIDTH, n_buf).astype(BF16)
    out = (_dot_nt(p_c.astype(BF16), vt) + _dot(p_n.astype(BF16), v_new)) / den
    att = jnp.zeros((t, ATT_WIDTH), F32)
    for h in range(N_HEADS):
        att = att + jnp.where(head_of_lane == h, out[h * t:(h + 1) * t, :], 0.0)
    o_ref[...] = att


def _sample_attn_specs(n_buf, steps, cached_tables, new_tables):
    tok = pl.BlockSpec((steps, ATT_WIDTH), lambda i: (i, 0))
    cache = pl.BlockSpec((None, N_HEADS, HEAD_DIM, n_buf), lambda i: (i, 0, 0, 0))
    return [tok, tok, tok, cache, cache, _const_spec(cached_tables.shape), _const_spec(new_tables.shape)], tok


def _dense_stages(x_ref, att_ref, rnn_ref, gatt_ref, woa_ref, wor_ref, g2_ref, w1_ref, w2_ref, gf_ref, y_ref):
    att_n = _rms(att_ref[...], gatt_ref[...]).astype(BF16)
    x1 = x_ref[...] + _dot(att_n, woa_ref[...]) + _dot(rnn_ref[...], wor_ref[...])
    n2 = _rms(x1, g2_ref[...]).astype(BF16)
    yield
    mlp = None
    for c in range(D_FF // FF_CHUNK):
        cols = slice(c * FF_CHUNK, (c + 1) * FF_CHUNK)
        hmid = jnp.maximum(_dot(n2, w1_ref[:, cols]), 0.0)
        part = _dot((hmid * hmid).astype(BF16), w2_ref[cols, :])
        mlp = part if mlp is None else mlp + part
        yield
    y_ref[...] = _rms(x1 + mlp, gf_ref[...])


def _interleave(*stage_generators):
    pending = list(stage_generators)
    while pending:
        pending = [g for g in pending if next(g, StopIteration) is not StopIteration]


def _dense_kernel(*refs):
    _interleave(_dense_stages(*refs))


N_DENSE_IN = 10


def _dense_and_sample_attn_kernel(*refs):
    dense_in, attn_in = refs[:N_DENSE_IN], refs[N_DENSE_IN:-2]
    y_ref, att_s_ref = refs[-2:]
    _interleave(_sample_attn_stages(*attn_in, att_s_ref), _dense_stages(*dense_in, y_ref))


def _dense_specs(t):
    tok = lambda width: pl.BlockSpec((t, width), lambda i: (i, 0))
    in_specs = [tok(D_MODEL), tok(ATT_WIDTH), tok(LRU_WIDTH),
                _const_spec((1, ATT_WIDTH)), _const_spec((ATT_WIDTH, D_MODEL)), _const_spec((LRU_WIDTH, D_MODEL)),
                _const_spec((1, D_MODEL)), _const_spec((D_MODEL, D_FF)), _const_spec((D_FF, D_MODEL)),
                _const_spec((1, D_MODEL))]
    assert len(in_specs) == N_DENSE_IN
    return in_specs, tok(D_MODEL)


def _dense(x, att, rnn, weights):
    rows = x.shape[0]
    t = min(TOKEN_TILE, rows)
    in_specs, out_spec = _dense_specs(t)
    return pl.pallas_call(
        _dense_kernel,
        grid=(rows // t,),
        in_specs=in_specs,
        out_specs=out_spec,
        out_shape=jax.ShapeDtypeStruct((rows, D_MODEL), F32),
        compiler_params=pltpu.CompilerParams(dimension_semantics=("arbitrary",), vmem_limit_bytes=VMEM_LIMIT),
        name="dense_tail",
    )(x, att, rnn, *weights)


def _dense_and_sample_attn(x, att, rnn, weights, q_s, k_new, v_new, cache_kt, cache_vt, cached_tables, new_tables,
                           steps):
    rows = x.shape[0]
    n_seq = q_s.shape[0] // steps
    t = rows // n_seq
    assert t * n_seq == rows and t % SUBLANES == 0
    dense_in, dense_out = _dense_specs(t)
    attn_in, attn_out = _sample_attn_specs(cache_kt.shape[-1], steps, cached_tables, new_tables)
    return pl.pallas_call(
        _dense_and_sample_attn_kernel,
        grid=(n_seq,),
        in_specs=dense_in + attn_in,
        out_specs=[dense_out, attn_out],
        out_shape=[jax.ShapeDtypeStruct((rows, D_MODEL), F32), jax.ShapeDtypeStruct(q_s.shape, F32)],
        compiler_params=pltpu.CompilerParams(dimension_semantics=("arbitrary",), vmem_limit_bytes=VMEM_LIMIT),
        name="dense_tail_sample_attn",
    )(x, att, rnn, *weights, q_s, k_new, v_new, cache_kt, cache_vt, cached_tables, new_tables)


def _t5_bucket(dist):
    max_exact = N_BUCKETS // 2
    d_f = jnp.maximum(dist, max_exact).astype(F32)
    large = max_exact + (jnp.log(d_f / max_exact) / math.log(WIN_MAX / max_exact)
                         * (N_BUCKETS - max_exact)).astype(jnp.int32)
    large = jnp.minimum(large, N_BUCKETS - 1)
    return jnp.where(dist < max_exact, dist, large)


def _branch_bias(rel_bias, dil):
    dist = jnp.arange(SPAN + 1, dtype=jnp.int32) * dil
    return rel_bias[_t5_bucket(dist)].astype(F32).T


def _prompt_bias_rows(rel_bias):
    rows = []
    for dil in DILATIONS:
        bias = _branch_bias(rel_bias, dil)
        prev = bias[:, SPAN:0:-1]
        cur = jnp.concatenate([bias[:, 0:1], bias[:, SPAN - 1:0:-1]], axis=1)
        rows.append(jnp.stack([prev, cur], axis=1))
    return jnp.stack(rows)


def _sample_bias_tables(rel_bias, n_buf, steps):
    cached, new = [], []
    pos = np.arange(n_buf)
    new_row = np.arange(SAMPLE_PAD)
    delta_new = np.arange(steps)[:, None] - new_row[None, :]
    for dil in DILATIONS:
        bias = _branch_bias(rel_bias, dil)
        every = [bias[:, SPAN:0:-1]] + [jnp.full((N_HEADS, SPAN), NEG_INF, F32)] * (dil - 1)
        row0 = jnp.concatenate([jnp.full((N_HEADS, n_buf - SPAN * dil), NEG_INF, F32),
                                jnp.stack(every, axis=-1).reshape(N_HEADS, SPAN * dil)], axis=1)
        rows = [jnp.where(pos[None, :] >= t, jnp.roll(row0, t, axis=1), NEG_INF) for t in range(steps)]
        cached.append(jnp.stack(rows, axis=1).reshape(N_HEADS * steps, n_buf))
        valid = (new_row[None, :] < steps) & (delta_new >= 0) & (delta_new % dil == 0)
        j_of = np.where(valid, delta_new // dil, -1)
        one_hot = (j_of[None] == np.arange(steps)[:, None, None]).astype(np.float32)
        vals = jnp.sum(bias[:, :steps, None, None] * one_hot[None], axis=1)
        new.append(jnp.where(valid[None], vals, NEG_INF).reshape(N_HEADS * steps, SAMPLE_PAD))
    return jnp.stack(cached), jnp.stack(new)


def _paired_gate_weights(gate_a_w, gate_x_w):
    def pair_diag(w):
        z = jnp.zeros((LRU_BLOCK, LRU_BLOCK), w.dtype)
        return jnp.stack([jnp.block([[w[2 * p], z], [z, w[2 * p + 1]]]) for p in range(N_LRU_BLOCKS // 2)])
    return jnp.concatenate([pair_diag(gate_a_w), pair_diag(gate_x_w)], axis=-1).astype(BF16)


def kernel(x_prompt, x_sample, cache_k, cache_v, state_conv, state_h, norm1_g, w_in, rel_bias, conv_w, conv_b,
           gate_a_w, gate_a_b, gate_x_w, gate_x_b, lru_lambda, att_out_g, rnn_out_g, w_out, norm2_g, w_mlp_in,
           w_mlp_out, final_g):
    depth = w_in.shape[0]
    assert depth == 1, "the final norm is fused into the single layer's dense kernel"
    b, s, _ = x_prompt.shape
    db, steps, _ = x_sample.shape
    n_buf = cache_k.shape[2]
    assert steps == SUBLANES and s == WIN_MAX and n_buf == WIN_MAX
    l = 0
    row = lambda v: v.reshape(1, -1).astype(F32)

    mix_w = (row(norm1_g[l]), w_in[l].astype(BF16), conv_w[l], row(conv_b[l]),
             _paired_gate_weights(gate_a_w[l], gate_x_w[l]), row(gate_a_b[l]), row(gate_x_b[l]),
             row(lru_lambda[l]), row(rnn_out_g[l]))
    dense_w = (row(att_out_g[l]), w_out[l, :ATT_WIDTH].astype(BF16), w_out[l, ATT_WIDTH:].astype(BF16),
               row(norm2_g[l]), w_mlp_in[l].astype(BF16), w_mlp_out[l].astype(BF16), row(final_g))

    xs = x_sample.reshape(db * steps, D_MODEL)
    state_rows = jnp.pad(state_conv[l], ((0, 0), (0, SUBLANES - (CONV_WIDTH - 1)), (0, 0))).reshape(db * steps, LRU_WIDTH)
    qs, ks, vs, rnn_s, conv_s, h_s = _sample_mix(xs, state_rows, state_h[l], mix_w)
    cache_kt = jnp.transpose(cache_k[l], (0, 2, 3, 1))
    cache_vt = jnp.transpose(cache_v[l], (0, 2, 3, 1))

    q, k, v, rnn, conv_p, h_p = _prompt_mix(x_prompt, mix_w)
    att = _prompt_attn(q, k, v, _prompt_bias_rows(rel_bias))
    y_prompt, att_s = _dense_and_sample_attn(
        x_prompt.reshape(b * s, D_MODEL), att.reshape(b * s, ATT_WIDTH), rnn.reshape(b * s, LRU_WIDTH), dense_w,
        qs, ks, vs, cache_kt, cache_vt, *_sample_bias_tables(rel_bias, n_buf, steps), steps)
    y_prompt = y_prompt.reshape(b, s, D_MODEL)
    y_sample = _dense(xs, att_s, rnn_s, dense_w).reshape(db, steps, D_MODEL)

    kv_p = (1, b, s, N_HEADS, HEAD_DIM)
    kv_s = (1, db, steps, N_HEADS, HEAD_DIM)
    return (y_prompt, y_sample,
            k.reshape(kv_p), v.reshape(kv_p), conv_p[None], h_p.reshape(1, b, LRU_WIDTH),
            ks.reshape(kv_s), vs.reshape(kv_s), conv_s[None], h_s[None])
```

```python
import functools
import math

import numpy as np
import jax
import jax.numpy as jnp
from jax import lax
from jax.experimental import pallas as pl
from jax.experimental.pallas import tpu as pltpu

F32 = jnp.float32
BF16 = jnp.bfloat16

D_MODEL = 1024
ATT_WIDTH = 512
LRU_WIDTH = 512
HEAD_DIM = 64
N_HEADS = 8
N_LRU_BLOCKS = 8
LRU_BLOCK = 64
CONV_WIDTH = 4
LRU_C = 8.0
D_FF = 4096
SPAN = 128
DILATIONS = (1, 4, 16)
WIN_MAX = 2048
N_BUCKETS = 32
NORM_EPS = 1e-6
NEG_INF = -1e30
Q_SCALE = HEAD_DIM ** -0.5

SUBLANES = 8
LANES = 128
HEADS_PER_TILE = LANES // HEAD_DIM
VMEM_LIMIT = 56 * 1024 * 1024

TOKEN_TILE = 512
PROMPT_MIX_TILE = 1024
PROMPT_SUB_TILE = 256
FF_CHUNK = 1024
SAMPLE_PAD = 128


def _dot(a, b):
    return jnp.dot(a, b, preferred_element_type=F32)


def _dot_nt(a, b):
    return lax.dot_general(a, b, (((1,), (1,)), ((), ())), preferred_element_type=F32)


def _rms(x, g):
    return x * lax.rsqrt(jnp.mean(x * x, axis=-1, keepdims=True) + NORM_EPS) * g


def _const_spec(shape):
    nd = len(shape)
    return pl.BlockSpec(shape, lambda *_: (0,) * nd, pipeline_mode=pl.Buffered(1))


def _lru_gates(xc, wg_ref, ba_ref, bx_ref, lam_ref, a_sc, u_sc):
    xcb = xc.astype(BF16)
    lam = lam_ref[...]
    softplus_neg_lam = jnp.maximum(-lam, 0.0) + jnp.log1p(jnp.exp(-jnp.abs(lam)))
    for p in range(LRU_WIDTH // LANES):
        cols = slice(p * LANES, (p + 1) * LANES)
        g = _dot(xcb[:, cols], wg_ref[p])
        r = jax.nn.sigmoid(g[:, :LANES] + ba_ref[:, cols])
        gi = jax.nn.sigmoid(g[:, LANES:] + bx_ref[:, cols])
        log_a = (-LRU_C) * r * softplus_neg_lam[:, cols]
        a = jnp.exp(log_a)
        a_sc[:, cols] = a
        u_sc[:, cols] = jnp.sqrt(-jnp.tanh(log_a) * (a * a + 1.0)) * (gi * xc[:, cols])


def _group_scan(a, u, row):
    for s in (1, 2, 4):
        keep = row >= s
        u = jnp.where(keep, a * pltpu.roll(u, s, 0) + u, u)
        a = jnp.where(keep, a * pltpu.roll(a, s, 0), a)
    return a, u


def _prompt_mix_kernel(x_ref, g1_ref, win_ref, cw_ref, cb_ref, wg_ref, ba_ref, bx_ref, lam_ref, grnn_ref,
                       q_ref, k_ref, v_ref, rnn_ref, conv_ref, h_ref,
                       xr_ext, gr_sc, a_sc, u_sc, hs_sc, h_carry):
    t = x_ref.shape[0]
    sub = PROMPT_SUB_TILE
    n_sub = t // sub

    @pl.when(pl.program_id(1) == 0)
    def _():
        xr_ext[0:SUBLANES, :] = jnp.zeros((SUBLANES, LRU_WIDTH), F32)
        h_carry[...] = jnp.zeros_like(h_carry)

    row = lax.broadcasted_iota(jnp.int32, (SUBLANES, LRU_WIDTH), 0)

    def project(s):
        rows = slice(s * sub, (s + 1) * sub)
        n = _rms(x_ref[rows, :], g1_ref[...]).astype(BF16)
        q_ref[rows, :] = _dot(n, win_ref[:, 0:ATT_WIDTH])
        k_ref[rows, :] = _dot(n, win_ref[:, ATT_WIDTH:2 * ATT_WIDTH])
        v_ref[rows, :] = _dot(n, win_ref[:, 2 * ATT_WIDTH:3 * ATT_WIDTH])
        xr_ext[SUBLANES + s * sub:SUBLANES + (s + 1) * sub, :] = _dot(
            n, win_ref[:, 3 * ATT_WIDTH:3 * ATT_WIDTH + LRU_WIDTH])
        gr_sc[rows, :] = _dot(n, win_ref[:, 3 * ATT_WIDTH + LRU_WIDTH:])

    def recur(s, h):
        rows = slice(s * sub, (s + 1) * sub)
        xc = cb_ref[...]
        for back in range(CONV_WIDTH):
            xc = xc + (cw_ref[CONV_WIDTH - 1 - back:CONV_WIDTH - back, :]
                       * xr_ext[pl.ds(SUBLANES + s * sub - back, sub), :])
        _lru_gates(xc, wg_ref, ba_ref, bx_ref, lam_ref, a_sc.at[rows], u_sc.at[rows])
        for g in range(sub // SUBLANES):
            grp = slice(s * sub + g * SUBLANES, s * sub + (g + 1) * SUBLANES)
            a, u = _group_scan(a_sc[grp, :], u_sc[grp, :], row)
            hs = a * h + u
            hs_sc[grp, :] = hs
            h = jnp.broadcast_to(hs[SUBLANES - 1:SUBLANES, :], hs.shape)
        rnn_ref[rows, :] = _rms(hs_sc[rows, :] * jax.nn.gelu(gr_sc[rows, :]), grnn_ref[...]).astype(BF16)
        return h

    h = h_carry[...]
    project(0)
    for s in range(1, n_sub):
        project(s)
        h = recur(s - 1, h)
    h = recur(n_sub - 1, h)
    h_carry[...] = h
    h_ref[...] = h[0:1, :]
    conv_ref[...] = xr_ext[SUBLANES + t - (CONV_WIDTH - 1):SUBLANES + t, :]
    xr_ext[0:SUBLANES, :] = xr_ext[t:t + SUBLANES, :]


def _sample_mix_kernel(x_ref, st_ref, h0_ref, g1_ref, win_ref, cw_ref, cb_ref, wg_ref, ba_ref, bx_ref, lam_ref,
                       grnn_ref, q_ref, k_ref, v_ref, rnn_ref, conv_ref, h_ref,
                       xr_sc, xc_sc, a_sc, u_sc, hs_sc):
    t = x_ref.shape[0]
    n_seq = t // SUBLANES
    n = _rms(x_ref[...], g1_ref[...]).astype(BF16)
    q_ref[...] = _dot(n, win_ref[:, 0:ATT_WIDTH])
    k_ref[...] = _dot(n, win_ref[:, ATT_WIDTH:2 * ATT_WIDTH])
    v_ref[...] = _dot(n, win_ref[:, 2 * ATT_WIDTH:3 * ATT_WIDTH])
    xr_sc[...] = _dot(n, win_ref[:, 3 * ATT_WIDTH:3 * ATT_WIDTH + LRU_WIDTH])

    row = lax.broadcasted_iota(jnp.int32, (SUBLANES, LRU_WIDTH), 0)
    n_state = CONV_WIDTH - 1

    def conv_group(b, carry):
        rows = pl.ds(pl.multiple_of(b * SUBLANES, SUBLANES), SUBLANES)
        xg = xr_sc[rows, :]
        sg = st_ref[rows, :]
        xc = cb_ref[...] + cw_ref[CONV_WIDTH - 1:CONV_WIDTH, :] * xg
        for back in range(1, CONV_WIDTH):
            prev = jnp.where(row >= back, pltpu.roll(xg, back, 0),
                             pltpu.roll(sg, (back - n_state) % SUBLANES, 0))
            xc = xc + cw_ref[CONV_WIDTH - 1 - back:CONV_WIDTH - back, :] * prev
        xc_sc[rows, :] = xc
        conv_ref[b] = xg[SUBLANES - n_state:, :]
        return carry

    lax.fori_loop(0, n_seq, conv_group, 0)

    _lru_gates(xc_sc[...], wg_ref, ba_ref, bx_ref, lam_ref, a_sc, u_sc)

    def scan_group(b, carry):
        rows = pl.ds(pl.multiple_of(b * SUBLANES, SUBLANES), SUBLANES)
        a, u = _group_scan(a_sc[rows, :], u_sc[rows, :], row)
        hs = a * h0_ref[pl.ds(b, 1), :] + u
        hs_sc[rows, :] = hs
        h_ref[pl.ds(b, 1), :] = hs[SUBLANES - 1:SUBLANES, :]
        return carry

    lax.fori_loop(0, n_seq, scan_group, 0)

    gr = _dot(n, win_ref[:, 3 * ATT_WIDTH + LRU_WIDTH:])
    rnn_ref[...] = _rms(hs_sc[...] * jax.nn.gelu(gr), grnn_ref[...]).astype(BF16)


def _mix_weight_specs():
    return [
        _const_spec((1, D_MODEL)),
        _const_spec((D_MODEL, 3 * ATT_WIDTH + 2 * LRU_WIDTH)),
        _const_spec((CONV_WIDTH, LRU_WIDTH)),
        _const_spec((1, LRU_WIDTH)),
        _const_spec((LRU_WIDTH // LANES, LANES, 2 * LANES)),
        _const_spec((1, LRU_WIDTH)),
        _const_spec((1, LRU_WIDTH)),
        _const_spec((1, LRU_WIDTH)),
        _const_spec((1, LRU_WIDTH)),
    ]


def _prompt_mix(x, weights):
    b, s, _ = x.shape
    t = PROMPT_MIX_TILE
    assert s % t == 0 and t % PROMPT_SUB_TILE == 0
    tok = lambda width: pl.BlockSpec((None, t, width), lambda i, j: (i, j, 0))
    per_seq = lambda rows: pl.BlockSpec((None, rows, LRU_WIDTH), lambda i, j: (i, 0, 0))
    return pl.pallas_call(
        _prompt_mix_kernel,
        grid=(b, s // t),
        in_specs=[tok(D_MODEL)] + _mix_weight_specs(),
        out_specs=[tok(ATT_WIDTH), tok(ATT_WIDTH), tok(ATT_WIDTH), tok(LRU_WIDTH),
                   per_seq(CONV_WIDTH - 1), per_seq(1)],
        out_shape=[jax.ShapeDtypeStruct((b, s, ATT_WIDTH), F32)] * 3
        + [jax.ShapeDtypeStruct((b, s, LRU_WIDTH), BF16),
           jax.ShapeDtypeStruct((b, CONV_WIDTH - 1, LRU_WIDTH), F32),
           jax.ShapeDtypeStruct((b, 1, LRU_WIDTH), F32)],
        scratch_shapes=[pltpu.VMEM((t + SUBLANES, LRU_WIDTH), F32)] + [pltpu.VMEM((t, LRU_WIDTH), F32)] * 4
        + [pltpu.VMEM((SUBLANES, LRU_WIDTH), F32)],
        compiler_params=pltpu.CompilerParams(dimension_semantics=("arbitrary", "arbitrary"),
                                             vmem_limit_bytes=VMEM_LIMIT),
        name="prompt_mix",
    )(x, *weights)


def _sample_mix(x, state_rows, h0, weights):
    rows = x.shape[0]
    t = min(TOKEN_TILE, rows)
    n_seq = t // SUBLANES
    tok = lambda width: pl.BlockSpec((t, width), lambda i: (i, 0))
    return pl.pallas_call(
        _sample_mix_kernel,
        grid=(rows // t,),
        in_specs=[tok(D_MODEL), tok(LRU_WIDTH), pl.BlockSpec((n_seq, LRU_WIDTH), lambda i: (i, 0))]
        + _mix_weight_specs(),
        out_specs=[tok(ATT_WIDTH), tok(ATT_WIDTH), tok(ATT_WIDTH), tok(LRU_WIDTH),
                   pl.BlockSpec((n_seq, CONV_WIDTH - 1, LRU_WIDTH), lambda i: (i, 0, 0)),
                   pl.BlockSpec((n_seq, LRU_WIDTH), lambda i: (i, 0))],
        out_shape=[jax.ShapeDtypeStruct((rows, ATT_WIDTH), F32)] * 3
        + [jax.ShapeDtypeStruct((rows, LRU_WIDTH), BF16),
           jax.ShapeDtypeStruct((rows // SUBLANES, CONV_WIDTH - 1, LRU_WIDTH), F32),
           jax.ShapeDtypeStruct((rows // SUBLANES, LRU_WIDTH), F32)],
        scratch_shapes=[pltpu.VMEM((t, LRU_WIDTH), F32)] * 5,
        compiler_params=pltpu.CompilerParams(dimension_semantics=("arbitrary",), vmem_limit_bytes=VMEM_LIMIT),
        name="sample_mix",
    )(x, state_rows, h0, *weights)


def _prompt_attn_kernel(q_ref, k_ref, v_ref, brow_ref, o_ref, tab_sc, pv_sc, m_sc, den_sc):
    s_len = q_ref.shape[0]
    n_branch = len(DILATIONS)
    qi = lax.broadcasted_iota(jnp.int32, (SPAN, SPAN), 0)
    ki = lax.broadcasted_iota(jnp.int32, (SPAN, SPAN), 1)
    low_head = ki < HEAD_DIM

    for g in range(n_branch):
        for h in range(HEADS_PER_TILE):
            rows = slice(h * SPAN, (h + 1) * SPAN)
            prev = pltpu.roll(jnp.broadcast_to(brow_ref[g, h, 0:1, :], (SPAN, SPAN)), 0, 1, stride=1, stride_axis=0)
            cur = pltpu.roll(jnp.broadcast_to(brow_ref[g, h, 1:2, :], (SPAN, SPAN)), 0, 1, stride=1, stride_axis=0)
            tab_sc[g, rows, 0:SPAN] = jnp.where(ki >= qi, prev, NEG_INF)
            tab_sc[g, rows, SPAN:] = jnp.where(ki <= qi, cur, NEG_INF)

    def attend(branch, q_rows, k_rows, first):
        n_keys = SPAN if first else 2 * SPAN
        q = q_ref[q_rows, :] * Q_SCALE
        q2 = jnp.concatenate([jnp.where(low_head, q, 0.0), jnp.where(low_head, 0.0, q)], axis=0).astype(BF16)
        kk = k_ref[k_rows, :].astype(BF16)
        v_aug = jnp.concatenate([v_ref[k_rows, :], jnp.ones((n_keys, LANES), F32)], axis=1).astype(BF16)
        table = tab_sc[branch, :, SPAN:] if first else tab_sc[branch]
        logits = _dot_nt(q2, kk) + table
        m = jnp.max(logits, axis=-1, keepdims=True)
        p = jnp.exp(logits - m).astype(BF16)
        r = _dot(p, v_aug)
        m_b = jnp.broadcast_to(m, (2 * SPAN, LANES))
        pv_sc[branch, q_rows, :] = jnp.where(low_head, r[:SPAN, :LANES], r[SPAN:, :LANES])
        den_sc[branch, q_rows, :] = jnp.where(low_head, r[:SPAN, LANES:], r[SPAN:, LANES:])
        m_sc[branch, q_rows, :] = jnp.where(low_head, m_b[:SPAN], m_b[SPAN:])

    for branch, dil in enumerate(DILATIONS):
        n_blocks = s_len // dil // SPAN
        if dil == 1:
            attend(branch, pl.ds(0, SPAN), pl.ds(0, SPAN), True)
            group = 5
            assert (n_blocks - 1) % group == 0

            def contiguous(it, carry, branch=branch, group=group):
                for u in range(group):
                    q0 = pl.multiple_of((1 + it * group + u) * SPAN, SPAN)
                    attend(branch, pl.ds(q0, SPAN), pl.ds(pl.multiple_of(q0 - SPAN, SPAN), 2 * SPAN), False)
                return carry
            lax.fori_loop(0, (n_blocks - 1) // group, contiguous, 0, unroll=True)
        elif n_blocks > 1:
            for c in range(dil):
                rows = pl.ds(c, SPAN, stride=dil)
                attend(branch, rows, rows, True)

            def strided(i, carry, branch=branch, dil=dil):
                for c in range(dil):
                    q0 = c + i * (SPAN * dil)
                    attend(branch, pl.ds(q0, SPAN, stride=dil), pl.ds(q0 - SPAN * dil, 2 * SPAN, stride=dil), False)
                return carry
            lax.fori_loop(1, n_blocks, strided, 0, unroll=True)
        else:
            group = 8
            assert dil % group == 0

            def single_block(it, carry, branch=branch, dil=dil, group=group):
                for u in range(group):
                    rows = pl.ds(it * group + u, SPAN, stride=dil)
                    attend(branch, rows, rows, True)
                return carry
            lax.fori_loop(0, dil // group, single_block, 0, unroll=True)

    chunk = 2 * SPAN

    def merge(i, carry):
        rows = pl.ds(pl.multiple_of(i * chunk, chunk), chunk)
        ms = [m_sc[g, rows, :] for g in range(n_branch)]
        m_all = jnp.maximum(jnp.maximum(ms[0], ms[1]), ms[2])
        num = jnp.zeros((chunk, LANES), F32)
        den = jnp.zeros((chunk, LANES), F32)
        for g in range(n_branch):
            w = jnp.exp(ms[g] - m_all)
            num = num + w * pv_sc[g, rows, :]
            den = den + w * den_sc[g, rows, :]
        o_ref[rows, :] = num / den
        return carry

    lax.fori_loop(0, s_len // chunk, merge, 0)


def _prompt_attn(q, k, v, bias_rows):
    b, s, _ = q.shape
    n_branch = len(DILATIONS)
    head_pair = pl.BlockSpec((None, s, LANES), lambda i, j: (i, 0, j))
    return pl.pallas_call(
        _prompt_attn_kernel,
        grid=(b, ATT_WIDTH // LANES),
        in_specs=[head_pair, head_pair, head_pair,
                  pl.BlockSpec((n_branch, HEADS_PER_TILE, 2, SPAN), lambda i, j: (0, j, 0, 0))],
        out_specs=head_pair,
        out_shape=jax.ShapeDtypeStruct((b, s, ATT_WIDTH), F32),
        scratch_shapes=[pltpu.VMEM((n_branch, HEADS_PER_TILE * SPAN, 2 * SPAN), F32)]
        + [pltpu.VMEM((n_branch, s, LANES), F32)] * 3,
        compiler_params=pltpu.CompilerParams(dimension_semantics=("arbitrary", "arbitrary"),
                                             vmem_limit_bytes=VMEM_LIMIT),
        name="prompt_attn",
    )(q, k, v, bias_rows)


def _sample_attn_stages(q_ref, kn_ref, vn_ref, ckt_ref, cvt_ref, tc_ref, tn_ref, o_ref):
    t = q_ref.shape[0]
    n_buf = ckt_ref.shape[-1]
    n_branch = len(DILATIONS)
    head_of_lane = lax.broadcasted_iota(jnp.int32, (t, ATT_WIDTH), 1) // HEAD_DIM
    q = q_ref[...] * Q_SCALE
    q_bd = jnp.concatenate([jnp.where(head_of_lane == h, q, 0.0) for h in range(N_HEADS)], axis=0).astype(BF16)
    pad = jnp.zeros((SAMPLE_PAD - t, ATT_WIDTH), F32)
    k_new = jnp.concatenate([kn_ref[...], pad], axis=0).astype(BF16)
    v_new = jnp.concatenate([vn_ref[...], pad], axis=0).astype(BF16)

    kt = ckt_ref[...].reshape(ATT_WIDTH, n_buf).astype(BF16)
    half = n_buf // 2
    s_c = jnp.concatenate([_dot(q_bd, kt[:, :half]), _dot(q_bd, kt[:, half:])], axis=1)
    s_n = _dot_nt(q_bd, k_new)
    yield
    m = None
    for g in range(n_branch):
        m_g = jnp.maximum(jnp.max(s_c + tc_ref[g], axis=-1, keepdims=True),
                          jnp.max(s_n + tn_ref[g], axis=-1, keepdims=True))
        m = m_g if m is None else jnp.maximum(m, m_g)
    p_c = jnp.exp(s_c + tc_ref[0] - m)
    p_n = jnp.exp(s_n + tn_ref[0] - m)
    for g in range(1, n_branch):
        p_c = p_c + jnp.exp(s_c + tc_ref[g] - m)
        p_n = p_n + jnp.exp(s_n + tn_ref[g] - m)
    den = jnp.sum(p_c, axis=-1, keepdims=True) + jnp.sum(p_n, axis=-1, keepdims=True)
    yield
    vt = cvt_ref[...].reshape(ATT_WIDTH, n_buf).astype(BF16)
    p_cb = p_c.astype(BF16)
    out = (_dot_nt(p_cb[:, :half], vt[:, :half]) + _dot_nt(p_cb[:, half:], vt[:, half:])
           + _dot(p_n.astype(BF16), v_new)) / den
    att = jnp.zeros((t, ATT_WIDTH), F32)
    for h in range(N_HEADS):
        att = att + jnp.where(head_of_lane == h, out[h * t:(h + 1) * t, :], 0.0)
    o_ref[...] = att


def _sample_attn_specs(n_buf, steps, cached_tables, new_tables):
    tok = pl.BlockSpec((steps, ATT_WIDTH), lambda i: (i, 0))
    cache = pl.BlockSpec((None, N_HEADS, HEAD_DIM, n_buf), lambda i: (i, 0, 0, 0))
    return [tok, tok, tok, cache, cache, _const_spec(cached_tables.shape), _const_spec(new_tables.shape)], tok


def _dense_stages(x_ref, att_ref, rnn_ref, gatt_ref, woa_ref, wor_ref, g2_ref, w1_ref, w2_ref, gf_ref, y_ref):
    att_n = _rms(att_ref[...], gatt_ref[...]).astype(BF16)
    x1 = x_ref[...] + _dot(att_n, woa_ref[...]) + _dot(rnn_ref[...], wor_ref[...])
    n2 = _rms(x1, g2_ref[...]).astype(BF16)
    yield
    mlp = None
    for c in range(D_FF // FF_CHUNK):
        cols = slice(c * FF_CHUNK, (c + 1) * FF_CHUNK)
        hmid = jnp.maximum(_dot(n2, w1_ref[:, cols]), 0.0)
        part = _dot((hmid * hmid).astype(BF16), w2_ref[cols, :])
        mlp = part if mlp is None else mlp + part
        yield
    y_ref[...] = _rms(x1 + mlp, gf_ref[...])


def _interleave(*stage_generators):
    pending = list(stage_generators)
    while pending:
        pending = [g for g in pending if next(g, StopIteration) is not StopIteration]


def _dense_kernel(*refs):
    _interleave(_dense_stages(*refs))


N_DENSE_IN = 10


def _dense_and_sample_attn_kernel(*refs):
    dense_in, attn_in = refs[:N_DENSE_IN], refs[N_DENSE_IN:-2]
    y_ref, att_s_ref = refs[-2:]
    _interleave(_sample_attn_stages(*attn_in, att_s_ref), _dense_stages(*dense_in, y_ref))


def _dense_specs(t):
    tok = lambda width: pl.BlockSpec((t, width), lambda i: (i, 0))
    in_specs = [tok(D_MODEL), tok(ATT_WIDTH), tok(LRU_WIDTH),
                _const_spec((1, ATT_WIDTH)), _const_spec((ATT_WIDTH, D_MODEL)), _const_spec((LRU_WIDTH, D_MODEL)),
                _const_spec((1, D_MODEL)), _const_spec((D_MODEL, D_FF)), _const_spec((D_FF, D_MODEL)),
                _const_spec((1, D_MODEL))]
    assert len(in_specs) == N_DENSE_IN
    return in_specs, tok(D_MODEL)


def _dense(x, att, rnn, weights):
    rows = x.shape[0]
    t = min(TOKEN_TILE, rows)
    in_specs, out_spec = _dense_specs(t)
    return pl.pallas_call(
        _dense_kernel,
        grid=(rows // t,),
        in_specs=in_specs,
        out_specs=out_spec,
        out_shape=jax.ShapeDtypeStruct((rows, D_MODEL), F32),
        compiler_params=pltpu.CompilerParams(dimension_semantics=("arbitrary",), vmem_limit_bytes=VMEM_LIMIT),
        name="dense_tail",
    )(x, att, rnn, *weights)


def _dense_and_sample_attn(x, att, rnn, weights, q_s, k_new, v_new, cache_kt, cache_vt, cached_tables, new_tables,
                           steps):
    rows = x.shape[0]
    n_seq = q_s.shape[0] // steps
    t = rows // n_seq
    assert t * n_seq == rows and t % SUBLANES == 0
    dense_in, dense_out = _dense_specs(t)
    attn_in, attn_out = _sample_attn_specs(cache_kt.shape[-1], steps, cached_tables, new_tables)
    return pl.pallas_call(
        _dense_and_sample_attn_kernel,
        grid=(n_seq,),
        in_specs=dense_in + attn_in,
        out_specs=[dense_out, attn_out],
        out_shape=[jax.ShapeDtypeStruct((rows, D_MODEL), F32), jax.ShapeDtypeStruct(q_s.shape, F32)],
        compiler_params=pltpu.CompilerParams(dimension_semantics=("arbitrary",), vmem_limit_bytes=VMEM_LIMIT),
        name="dense_tail_sample_attn",
    )(x, att, rnn, *weights, q_s, k_new, v_new, cache_kt, cache_vt, cached_tables, new_tables)


def _t5_bucket(dist):
    max_exact = N_BUCKETS // 2
    d_f = jnp.maximum(dist, max_exact).astype(F32)
    large = max_exact + (jnp.log(d_f / max_exact) / math.log(WIN_MAX / max_exact)
                         * (N_BUCKETS - max_exact)).astype(jnp.int32)
    large = jnp.minimum(large, N_BUCKETS - 1)
    return jnp.where(dist < max_exact, dist, large)


def _branch_bias(rel_bias, dil):
    dist = jnp.arange(SPAN + 1, dtype=jnp.int32) * dil
    return rel_bias[_t5_bucket(dist)].astype(F32).T


def _prompt_bias_rows(rel_bias):
    rows = []
    for dil in DILATIONS:
        bias = _branch_bias(rel_bias, dil)
        prev = bias[:, SPAN:0:-1]
        cur = jnp.concatenate([bias[:, 0:1], bias[:, SPAN - 1:0:-1]], axis=1)
        rows.append(jnp.stack([prev, cur], axis=1))
    return jnp.stack(rows)


def _sample_bias_tables(rel_bias, n_buf, steps):
    cached, new = [], []
    pos = np.arange(n_buf)
    new_row = np.arange(SAMPLE_PAD)
    delta_new = np.arange(steps)[:, None] - new_row[None, :]
    for dil in DILATIONS:
        bias = _branch_bias(rel_bias, dil)
        every = [bias[:, SPAN:0:-1]] + [jnp.full((N_HEADS, SPAN), NEG_INF, F32)] * (dil - 1)
        row0 = jnp.concatenate([jnp.full((N_HEADS, n_buf - SPAN * dil), NEG_INF, F32),
                                jnp.stack(every, axis=-1).reshape(N_HEADS, SPAN * dil)], axis=1)
        rows = [jnp.where(pos[None, :] >= t, jnp.roll(row0, t, axis=1), NEG_INF) for t in range(steps)]
        cached.append(jnp.stack(rows, axis=1).reshape(N_HEADS * steps, n_buf))
        valid = (new_row[None, :] < steps) & (delta_new >= 0) & (delta_new % dil == 0)
        j_of = np.where(valid, delta_new // dil, -1)
        one_hot = (j_of[None] == np.arange(steps)[:, None, None]).astype(np.float32)
        vals = jnp.sum(bias[:, :steps, None, None] * one_hot[None], axis=1)
        new.append(jnp.where(valid[None], vals, NEG_INF).reshape(N_HEADS * steps, SAMPLE_PAD))
    return jnp.stack(cached), jnp.stack(new)


def _paired_gate_weights(gate_a_w, gate_x_w):
    def pair_diag(w):
        z = jnp.zeros((LRU_BLOCK, LRU_BLOCK), w.dtype)
        return jnp.stack([jnp.block([[w[2 * p], z], [z, w[2 * p + 1]]]) for p in range(N_LRU_BLOCKS // 2)])
    return jnp.concatenate([pair_diag(gate_a_w), pair_diag(gate_x_w)], axis=-1).astype(BF16)


def kernel(x_prompt, x_sample, cache_k, cache_v, state_conv, state_h, norm1_g, w_in, rel_bias, conv_w, conv_b,
           gate_a_w, gate_a_b, gate_x_w, gate_x_b, lru_lambda, att_out_g, rnn_out_g, w_out, norm2_g, w_mlp_in,
           w_mlp_out, final_g):
    depth = w_in.shape[0]
    assert depth == 1, "the final norm is fused into the single layer's dense kernel"
    b, s, _ = x_prompt.shape
    db, steps, _ = x_sample.shape
    n_buf = cache_k.shape[2]
    assert steps == SUBLANES and s == WIN_MAX and n_buf == WIN_MAX
    l = 0
    row = lambda v: v.reshape(1, -1).astype(F32)

    mix_w = (row(norm1_g[l]), w_in[l].astype(BF16), conv_w[l], row(conv_b[l]),
             _paired_gate_weights(gate_a_w[l], gate_x_w[l]), row(gate_a_b[l]), row(gate_x_b[l]),
             row(lru_lambda[l]), row(rnn_out_g[l]))
    dense_w = (row(att_out_g[l]), w_out[l, :ATT_WIDTH].astype(BF16), w_out[l, ATT_WIDTH:].astype(BF16),
               row(norm2_g[l]), w_mlp_in[l].astype(BF16), w_mlp_out[l].astype(BF16), row(final_g))

    xs = x_sample.reshape(db * steps, D_MODEL)
    state_rows = jnp.pad(state_conv[l], ((0, 0), (0, SUBLANES - (CONV_WIDTH - 1)), (0, 0))).reshape(db * steps, LRU_WIDTH)
    qs, ks, vs, rnn_s, conv_s, h_s = _sample_mix(xs, state_rows, state_h[l], mix_w)
    cache_kt = jnp.transpose(cache_k[l], (0, 2, 3, 1))
    cache_vt = jnp.transpose(cache_v[l], (0, 2, 3, 1))

    q, k, v, rnn, conv_p, h_p = _prompt_mix(x_prompt, mix_w)
    att = _prompt_attn(q, k, v, _prompt_bias_rows(rel_bias))
    y_prompt, att_s = _dense_and_sample_attn(
        x_prompt.reshape(b * s, D_MODEL), att.reshape(b * s, ATT_WIDTH), rnn.reshape(b * s, LRU_WIDTH), dense_w,
        qs, ks, vs, cache_kt, cache_vt, *_sample_bias_tables(rel_bias, n_buf, steps), steps)
    y_prompt = y_prompt.reshape(b, s, D_MODEL)
    y_sample = _dense(xs, att_s, rnn_s, dense_w).reshape(db, steps, D_MODEL)

    kv_p = (1, b, s, N_HEADS, HEAD_DIM)
    kv_s = (1, db, steps, N_HEADS, HEAD_DIM)
    return (y_prompt, y_sample,
            k.reshape(kv_p), v.reshape(kv_p), conv_p[None], h_p.reshape(1, b, LRU_WIDTH),
            ks.reshape(kv_s), vs.reshape(kv_s), conv_s[None], h_s[None])
```

```python
import functools
import math

import numpy as np
import jax
import jax.numpy as jnp
from jax import lax
from jax.experimental import pallas as pl
from jax.experimental.pallas import tpu as pltpu

F32 = jnp.float32
BF16 = jnp.bfloat16

D_MODEL = 1024
ATT_WIDTH = 512
LRU_WIDTH = 512
HEAD_DIM = 64
N_HEADS = 8
N_LRU_BLOCKS = 8
LRU_BLOCK = 64
CONV_WIDTH = 4
LRU_C = 8.0
D_FF = 4096
SPAN = 128
DILATIONS = (1, 4, 16)
WIN_MAX = 2048
N_BUCKETS = 32
NORM_EPS = 1e-6
NEG_INF = -1e30
Q_SCALE = HEAD_DIM ** -0.5

SUBLANES = 8
LANES = 128
HEADS_PER_TILE = LANES // HEAD_DIM
VMEM_LIMIT = 56 * 1024 * 1024

TOKEN_TILE = 512
PROMPT_MIX_TILE = 1024
PROMPT_SUB_TILE = 128
FF_CHUNK = 1024
SAMPLE_PAD = 128


def _dot(a, b):
    return jnp.dot(a, b, preferred_element_type=F32)


def _dot_nt(a, b):
    return lax.dot_general(a, b, (((1,), (1,)), ((), ())), preferred_element_type=F32)


def _rms(x, g):
    return x * lax.rsqrt(jnp.mean(x * x, axis=-1, keepdims=True) + NORM_EPS) * g


def _const_spec(shape):
    nd = len(shape)
    return pl.BlockSpec(shape, lambda *_: (0,) * nd, pipeline_mode=pl.Buffered(1))


def _lru_gates(xc, wg_ref, ba_ref, bx_ref, lam_ref, a_sc, u_sc):
    xcb = xc.astype(BF16)
    lam = lam_ref[...]
    softplus_neg_lam = jnp.maximum(-lam, 0.0) + jnp.log1p(jnp.exp(-jnp.abs(lam)))
    for p in range(LRU_WIDTH // LANES):
        cols = slice(p * LANES, (p + 1) * LANES)
        g = _dot(xcb[:, cols], wg_ref[p])
        r = jax.nn.sigmoid(g[:, :LANES] + ba_ref[:, cols])
        gi = jax.nn.sigmoid(g[:, LANES:] + bx_ref[:, cols])
        log_a = (-LRU_C) * r * softplus_neg_lam[:, cols]
        a = jnp.exp(log_a)
        a_sc[:, cols] = a
        u_sc[:, cols] = jnp.sqrt(-jnp.tanh(log_a) * (a * a + 1.0)) * (gi * xc[:, cols])


def _group_scan(a, u, row):
    for s in (1, 2, 4):
        keep = row >= s
        u = jnp.where(keep, a * pltpu.roll(u, s, 0) + u, u)
        a = jnp.where(keep, a * pltpu.roll(a, s, 0), a)
    return a, u


def _prompt_mix_kernel(x_ref, g1_ref, win_ref, cw_ref, cb_ref, wg_ref, ba_ref, bx_ref, lam_ref, grnn_ref,
                       q_ref, k_ref, v_ref, rnn_ref, conv_ref, h_ref,
                       xr_ext, gr_sc, a_sc, u_sc, hs_sc, h_carry):
    t = x_ref.shape[0]
    sub = PROMPT_SUB_TILE
    n_sub = t // sub

    @pl.when(pl.program_id(1) == 0)
    def _():
        xr_ext[0:SUBLANES, :] = jnp.zeros((SUBLANES, LRU_WIDTH), F32)
        h_carry[...] = jnp.zeros_like(h_carry)

    row = lax.broadcasted_iota(jnp.int32, (SUBLANES, LRU_WIDTH), 0)

    def project(s):
        rows = slice(s * sub, (s + 1) * sub)
        n = _rms(x_ref[rows, :], g1_ref[...]).astype(BF16)
        q_ref[rows, :] = _dot(n, win_ref[:, 0:ATT_WIDTH])
        k_ref[rows, :] = _dot(n, win_ref[:, ATT_WIDTH:2 * ATT_WIDTH])
        v_ref[rows, :] = _dot(n, win_ref[:, 2 * ATT_WIDTH:3 * ATT_WIDTH])
        xr_ext[SUBLANES + s * sub:SUBLANES + (s + 1) * sub, :] = _dot(
            n, win_ref[:, 3 * ATT_WIDTH:3 * ATT_WIDTH + LRU_WIDTH])
        gr_sc[rows, :] = _dot(n, win_ref[:, 3 * ATT_WIDTH + LRU_WIDTH:])

    def recur(s, h):
        rows = slice(s * sub, (s + 1) * sub)
        xc = cb_ref[...]
        for back in range(CONV_WIDTH):
            xc = xc + (cw_ref[CONV_WIDTH - 1 - back:CONV_WIDTH - back, :]
                       * xr_ext[pl.ds(SUBLANES + s * sub - back, sub), :])
        _lru_gates(xc, wg_ref, ba_ref, bx_ref, lam_ref, a_sc.at[rows], u_sc.at[rows])
        for g in range(sub // SUBLANES):
            grp = slice(s * sub + g * SUBLANES, s * sub + (g + 1) * SUBLANES)
            a, u = _group_scan(a_sc[grp, :], u_sc[grp, :], row)
            hs = a * h + u
            hs_sc[grp, :] = hs
            h = jnp.broadcast_to(hs[SUBLANES - 1:SUBLANES, :], hs.shape)
        rnn_ref[rows, :] = _rms(hs_sc[rows, :] * jax.nn.gelu(gr_sc[rows, :]), grnn_ref[...]).astype(BF16)
        return h

    h = h_carry[...]
    project(0)
    for s in range(1, n_sub):
        project(s)
        h = recur(s - 1, h)
    h = recur(n_sub - 1, h)
    h_carry[...] = h
    h_ref[...] = h[0:1, :]
    conv_ref[...] = xr_ext[SUBLANES + t - (CONV_WIDTH - 1):SUBLANES + t, :]
    xr_ext[0:SUBLANES, :] = xr_ext[t:t + SUBLANES, :]


def _sample_mix_kernel(x_ref, st_ref, h0_ref, g1_ref, win_ref, cw_ref, cb_ref, wg_ref, ba_ref, bx_ref, lam_ref,
                       grnn_ref, q_ref, k_ref, v_ref, rnn_ref, conv_ref, h_ref,
                       xr_sc, xc_sc, a_sc, u_sc, hs_sc):
    t = x_ref.shape[0]
    n_seq = t // SUBLANES
    n = _rms(x_ref[...], g1_ref[...]).astype(BF16)
    q_ref[...] = _dot(n, win_ref[:, 0:ATT_WIDTH])
    k_ref[...] = _dot(n, win_ref[:, ATT_WIDTH:2 * ATT_WIDTH])
    v_ref[...] = _dot(n, win_ref[:, 2 * ATT_WIDTH:3 * ATT_WIDTH])
    xr_sc[...] = _dot(n, win_ref[:, 3 * ATT_WIDTH:3 * ATT_WIDTH + LRU_WIDTH])

    row = lax.broadcasted_iota(jnp.int32, (SUBLANES, LRU_WIDTH), 0)
    n_state = CONV_WIDTH - 1

    def conv_group(b, carry):
        rows = pl.ds(pl.multiple_of(b * SUBLANES, SUBLANES), SUBLANES)
        xg = xr_sc[rows, :]
        sg = st_ref[rows, :]
        xc = cb_ref[...] + cw_ref[CONV_WIDTH - 1:CONV_WIDTH, :] * xg
        for back in range(1, CONV_WIDTH):
            prev = jnp.where(row >= back, pltpu.roll(xg, back, 0),
                             pltpu.roll(sg, (back - n_state) % SUBLANES, 0))
            xc = xc + cw_ref[CONV_WIDTH - 1 - back:CONV_WIDTH - back, :] * prev
        xc_sc[rows, :] = xc
        conv_ref[b] = xg[SUBLANES - n_state:, :]
        return carry

    lax.fori_loop(0, n_seq, conv_group, 0, unroll=8)

    _lru_gates(xc_sc[...], wg_ref, ba_ref, bx_ref, lam_ref, a_sc, u_sc)

    def scan_group(b, carry):
        rows = pl.ds(pl.multiple_of(b * SUBLANES, SUBLANES), SUBLANES)
        a, u = _group_scan(a_sc[rows, :], u_sc[rows, :], row)
        hs = a * h0_ref[pl.ds(b, 1), :] + u
        hs_sc[rows, :] = hs
        h_ref[pl.ds(b, 1), :] = hs[SUBLANES - 1:SUBLANES, :]
        return carry

    lax.fori_loop(0, n_seq, scan_group, 0, unroll=8)

    gr = _dot(n, win_ref[:, 3 * ATT_WIDTH + LRU_WIDTH:])
    rnn_ref[...] = _rms(hs_sc[...] * jax.nn.gelu(gr), grnn_ref[...]).astype(BF16)


def _mix_weight_specs():
    return [
        _const_spec((1, D_MODEL)),
        _const_spec((D_MODEL, 3 * ATT_WIDTH + 2 * LRU_WIDTH)),
        _const_spec((CONV_WIDTH, LRU_WIDTH)),
        _const_spec((1, LRU_WIDTH)),
        _const_spec((LRU_WIDTH // LANES, LANES, 2 * LANES)),
        _const_spec((1, LRU_WIDTH)),
        _const_spec((1, LRU_WIDTH)),
        _const_spec((1, LRU_WIDTH)),
        _const_spec((1, LRU_WIDTH)),
    ]


def _prompt_mix(x, weights):
    b, s, _ = x.shape
    t = PROMPT_MIX_TILE
    assert s % t == 0 and t % PROMPT_SUB_TILE == 0
    tok = lambda width: pl.BlockSpec((None, t, width), lambda i, j: (i, j, 0))
    per_seq = lambda rows: pl.BlockSpec((None, rows, LRU_WIDTH), lambda i, j: (i, 0, 0))
    return pl.pallas_call(
        _prompt_mix_kernel,
        grid=(b, s // t),
        in_specs=[tok(D_MODEL)] + _mix_weight_specs(),
        out_specs=[tok(ATT_WIDTH), tok(ATT_WIDTH), tok(ATT_WIDTH), tok(LRU_WIDTH),
                   per_seq(CONV_WIDTH - 1), per_seq(1)],
        out_shape=[jax.ShapeDtypeStruct((b, s, ATT_WIDTH), F32)] * 3
        + [jax.ShapeDtypeStruct((b, s, LRU_WIDTH), BF16),
           jax.ShapeDtypeStruct((b, CONV_WIDTH - 1, LRU_WIDTH), F32),
           jax.ShapeDtypeStruct((b, 1, LRU_WIDTH), F32)],
        scratch_shapes=[pltpu.VMEM((t + SUBLANES, LRU_WIDTH), F32)] + [pltpu.VMEM((t, LRU_WIDTH), F32)] * 4
        + [pltpu.VMEM((SUBLANES, LRU_WIDTH), F32)],
        compiler_params=pltpu.CompilerParams(dimension_semantics=("arbitrary", "arbitrary"),
                                             vmem_limit_bytes=VMEM_LIMIT),
        name="prompt_mix",
    )(x, *weights)


def _sample_mix(x, state_rows, h0, weights):
    rows = x.shape[0]
    t = min(TOKEN_TILE, rows)
    n_seq = t // SUBLANES
    tok = lambda width: pl.BlockSpec((t, width), lambda i: (i, 0))
    return pl.pallas_call(
        _sample_mix_kernel,
        grid=(rows // t,),
        in_specs=[tok(D_MODEL), tok(LRU_WIDTH), pl.BlockSpec((n_seq, LRU_WIDTH), lambda i: (i, 0))]
        + _mix_weight_specs(),
        out_specs=[tok(ATT_WIDTH), tok(ATT_WIDTH), tok(ATT_WIDTH), tok(LRU_WIDTH),
                   pl.BlockSpec((n_seq, CONV_WIDTH - 1, LRU_WIDTH), lambda i: (i, 0, 0)),
                   pl.BlockSpec((n_seq, LRU_WIDTH), lambda i: (i, 0))],
        out_shape=[jax.ShapeDtypeStruct((rows, ATT_WIDTH), F32)] * 3
        + [jax.ShapeDtypeStruct((rows, LRU_WIDTH), BF16),
           jax.ShapeDtypeStruct((rows // SUBLANES, CONV_WIDTH - 1, LRU_WIDTH), F32),
           jax.ShapeDtypeStruct((rows // SUBLANES, LRU_WIDTH), F32)],
        scratch_shapes=[pltpu.VMEM((t, LRU_WIDTH), F32)] * 5,
        compiler_params=pltpu.CompilerParams(dimension_semantics=("arbitrary",), vmem_limit_bytes=VMEM_LIMIT),
        name="sample_mix",
    )(x, state_rows, h0, *weights)


def _prompt_attn_kernel(q_ref, k_ref, v_ref, brow_ref, o_ref, tab_sc, pv_sc, m_sc, den_sc):
    s_len = q_ref.shape[0]
    n_branch = len(DILATIONS)
    qi = lax.broadcasted_iota(jnp.int32, (SPAN, SPAN), 0)
    ki = lax.broadcasted_iota(jnp.int32, (SPAN, SPAN), 1)
    low_head = ki < HEAD_DIM

    for g in range(n_branch):
        for h in range(HEADS_PER_TILE):
            rows = slice(h * SPAN, (h + 1) * SPAN)
            prev = pltpu.roll(jnp.broadcast_to(brow_ref[g, h, 0:1, :], (SPAN, SPAN)), 0, 1, stride=1, stride_axis=0)
            cur = pltpu.roll(jnp.broadcast_to(brow_ref[g, h, 1:2, :], (SPAN, SPAN)), 0, 1, stride=1, stride_axis=0)
            tab_sc[g, rows, 0:SPAN] = jnp.where(ki >= qi, prev, NEG_INF)
            tab_sc[g, rows, SPAN:] = jnp.where(ki <= qi, cur, NEG_INF)

    def attend(branch, q_rows, k_rows, first):
        n_keys = SPAN if first else 2 * SPAN
        q = q_ref[q_rows, :] * Q_SCALE
        q2 = jnp.concatenate([jnp.where(low_head, q, 0.0), jnp.where(low_head, 0.0, q)], axis=0).astype(BF16)
        kk = k_ref[k_rows, :].astype(BF16)
        v_aug = jnp.concatenate([v_ref[k_rows, :], jnp.ones((n_keys, LANES), F32)], axis=1).astype(BF16)
        table = tab_sc[branch, :, SPAN:] if first else tab_sc[branch]
        logits = _dot_nt(q2, kk) + table
        m = jnp.max(logits, axis=-1, keepdims=True)
        p = jnp.exp(logits - m).astype(BF16)
        r = _dot(p, v_aug)
        m_b = jnp.broadcast_to(m, (2 * SPAN, LANES))
        pv_sc[branch, q_rows, :] = jnp.where(low_head, r[:SPAN, :LANES], r[SPAN:, :LANES])
        den_sc[branch, q_rows, :] = jnp.where(low_head, r[:SPAN, LANES:], r[SPAN:, LANES:])
        m_sc[branch, q_rows, :] = jnp.where(low_head, m_b[:SPAN], m_b[SPAN:])

    for branch, dil in enumerate(DILATIONS):
        n_blocks = s_len // dil // SPAN
        if dil == 1:
            attend(branch, pl.ds(0, SPAN), pl.ds(0, SPAN), True)
            group = 5
            assert (n_blocks - 1) % group == 0

            def contiguous(it, carry, branch=branch, group=group):
                for u in range(group):
                    q0 = pl.multiple_of((1 + it * group + u) * SPAN, SPAN)
                    attend(branch, pl.ds(q0, SPAN), pl.ds(pl.multiple_of(q0 - SPAN, SPAN), 2 * SPAN), False)
                return carry
            lax.fori_loop(0, (n_blocks - 1) // group, contiguous, 0, unroll=True)
        elif n_blocks > 1:
            for c in range(dil):
                rows = pl.ds(c, SPAN, stride=dil)
                attend(branch, rows, rows, True)

            def strided(i, carry, branch=branch, dil=dil):
                for c in range(dil):
                    q0 = c + i * (SPAN * dil)
                    attend(branch, pl.ds(q0, SPAN, stride=dil), pl.ds(q0 - SPAN * dil, 2 * SPAN, stride=dil), False)
                return carry
            lax.fori_loop(1, n_blocks, strided, 0, unroll=True)
        else:
            group = 8
            assert dil % group == 0

            def single_block(it, carry, branch=branch, dil=dil, group=group):
                for u in range(group):
                    rows = pl.ds(it * group + u, SPAN, stride=dil)
                    attend(branch, rows, rows, True)
                return carry
            lax.fori_loop(0, dil // group, single_block, 0, unroll=True)

    chunk = 2 * SPAN

    def merge(i, carry):
        rows = pl.ds(pl.multiple_of(i * chunk, chunk), chunk)
        ms = [m_sc[g, rows, :] for g in range(n_branch)]
        m_all = jnp.maximum(jnp.maximum(ms[0], ms[1]), ms[2])
        num = jnp.zeros((chunk, LANES), F32)
        den = jnp.zeros((chunk, LANES), F32)
        for g in range(n_branch):
            w = jnp.exp(ms[g] - m_all)
            num = num + w * pv_sc[g, rows, :]
            den = den + w * den_sc[g, rows, :]
        o_ref[rows, :] = num / den
        return carry

    lax.fori_loop(0, s_len // chunk, merge, 0)


def _prompt_attn(q, k, v, bias_rows):
    b, s, _ = q.shape
    n_branch = len(DILATIONS)
    head_pair = pl.BlockSpec((None, s, LANES), lambda i, j: (i, 0, j))
    return pl.pallas_call(
        _prompt_attn_kernel,
        grid=(b, ATT_WIDTH // LANES),
        in_specs=[head_pair, head_pair, head_pair,
                  pl.BlockSpec((n_branch, HEADS_PER_TILE, 2, SPAN), lambda i, j: (0, j, 0, 0))],
        out_specs=head_pair,
        out_shape=jax.ShapeDtypeStruct((b, s, ATT_WIDTH), F32),
        scratch_shapes=[pltpu.VMEM((n_branch, HEADS_PER_TILE * SPAN, 2 * SPAN), F32)]
        + [pltpu.VMEM((n_branch, s, LANES), F32)] * 3,
        compiler_params=pltpu.CompilerParams(dimension_semantics=("arbitrary", "arbitrary"),
                                             vmem_limit_bytes=VMEM_LIMIT),
        name="prompt_attn",
    )(q, k, v, bias_rows)


def _sample_attn_stages(q_ref, kn_ref, vn_ref, ckt_ref, cvt_ref, tc_ref, tn_ref, o_ref):
    t = q_ref.shape[0]
    n_buf = ckt_ref.shape[-1]
    n_branch = len(DILATIONS)
    head_of_lane = lax.broadcasted_iota(jnp.int32, (t, ATT_WIDTH), 1) // HEAD_DIM
    q = q_ref[...] * Q_SCALE
    q_bd = jnp.concatenate([jnp.where(head_of_lane == h, q, 0.0) for h in range(N_HEADS)], axis=0).astype(BF16)
    pad = jnp.zeros((SAMPLE_PAD - t, ATT_WIDTH), F32)
    k_new = jnp.concatenate([kn_ref[...], pad], axis=0).astype(BF16)
    v_new = jnp.concatenate([vn_ref[...], pad], axis=0).astype(BF16)

    kt = ckt_ref[...].reshape(ATT_WIDTH, n_buf).astype(BF16)
    half = n_buf // 2
    s_c = jnp.concatenate([_dot(q_bd, kt[:, :half]), _dot(q_bd, kt[:, half:])], axis=1)
    s_n = _dot_nt(q_bd, k_new)
    yield
    m = None
    for g in range(n_branch):
        m_g = jnp.maximum(jnp.max(s_c + tc_ref[g], axis=-1, keepdims=True),
                          jnp.max(s_n + tn_ref[g], axis=-1, keepdims=True))
        m = m_g if m is None else jnp.maximum(m, m_g)
    p_c = jnp.exp(s_c + tc_ref[0] - m)
    p_n = jnp.exp(s_n + tn_ref[0] - m)
    for g in range(1, n_branch):
        p_c = p_c + jnp.exp(s_c + tc_ref[g] - m)
        p_n = p_n + jnp.exp(s_n + tn_ref[g] - m)
    den = jnp.sum(p_c, axis=-1, keepdims=True) + jnp.sum(p_n, axis=-1, keepdims=True)
    yield
    vt = cvt_ref[...].reshape(ATT_WIDTH, n_buf).astype(BF16)
    p_cb = p_c.astype(BF16)
    out = (_dot_nt(p_cb[:, :half], vt[:, :half]) + _dot_nt(p_cb[:, half:], vt[:, half:])
           + _dot(p_n.astype(BF16), v_new)) / den
    att = jnp.zeros((t, ATT_WIDTH), F32)
    for h in range(N_HEADS):
        att = att + jnp.where(head_of_lane == h, out[h * t:(h + 1) * t, :], 0.0)
    o_ref[...] = att


def _sample_attn_specs(n_buf, steps, cached_tables, new_tables):
    tok = pl.BlockSpec((steps, ATT_WIDTH), lambda i: (i, 0))
    cache = pl.BlockSpec((None, N_HEADS, HEAD_DIM, n_buf), lambda i: (i, 0, 0, 0))
    return [tok, tok, tok, cache, cache, _const_spec(cached_tables.shape), _const_spec(new_tables.shape)], tok


def _dense_stages(x_ref, att_ref, rnn_ref, gatt_ref, woa_ref, wor_ref, g2_ref, w1_ref, w2_ref, gf_ref, y_ref):
    att_n = _rms(att_ref[...], gatt_ref[...]).astype(BF16)
    x1 = x_ref[...] + _dot(att_n, woa_ref[...]) + _dot(rnn_ref[...], wor_ref[...])
    n2 = _rms(x1, g2_ref[...]).astype(BF16)
    yield
    mlp = None
    for c in range(D_FF // FF_CHUNK):
        cols = slice(c * FF_CHUNK, (c + 1) * FF_CHUNK)
        hmid = jnp.maximum(_dot(n2, w1_ref[:, cols]), 0.0)
        part = _dot((hmid * hmid).astype(BF16), w2_ref[cols, :])
        mlp = part if mlp is None else mlp + part
        yield
    y_ref[...] = _rms(x1 + mlp, gf_ref[...])


def _interleave(*stage_generators):
    pending = list(stage_generators)
    while pending:
        pending = [g for g in pending if next(g, StopIteration) is not StopIteration]


def _dense_kernel(*refs):
    _interleave(_dense_stages(*refs))


N_DENSE_IN = 10


def _dense_and_sample_attn_kernel(*refs):
    dense_in, attn_in = refs[:N_DENSE_IN], refs[N_DENSE_IN:-2]
    y_ref, att_s_ref = refs[-2:]
    _interleave(_sample_attn_stages(*attn_in, att_s_ref), _dense_stages(*dense_in, y_ref))


def _dense_specs(t):
    tok = lambda width: pl.BlockSpec((t, width), lambda i: (i, 0))
    in_specs = [tok(D_MODEL), tok(ATT_WIDTH), tok(LRU_WIDTH),
                _const_spec((1, ATT_WIDTH)), _const_spec((ATT_WIDTH, D_MODEL)), _const_spec((LRU_WIDTH, D_MODEL)),
                _const_spec((1, D_MODEL)), _const_spec((D_MODEL, D_FF)), _const_spec((D_FF, D_MODEL)),
                _const_spec((1, D_MODEL))]
    assert len(in_specs) == N_DENSE_IN
    return in_specs, tok(D_MODEL)


def _dense(x, att, rnn, weights):
    rows = x.shape[0]
    t = min(TOKEN_TILE, rows)
    in_specs, out_spec = _dense_specs(t)
    return pl.pallas_call(
        _dense_kernel,
        grid=(rows // t,),
        in_specs=in_specs,
        out_specs=out_spec,
        out_shape=jax.ShapeDtypeStruct((rows, D_MODEL), F32),
        compiler_params=pltpu.CompilerParams(dimension_semantics=("arbitrary",), vmem_limit_bytes=VMEM_LIMIT),
        name="dense_tail",
    )(x, att, rnn, *weights)


def _dense_and_sample_attn(x, att, rnn, weights, q_s, k_new, v_new, cache_kt, cache_vt, cached_tables, new_tables,
                           steps):
    rows = x.shape[0]
    n_seq = q_s.shape[0] // steps
    t = rows // n_seq
    assert t * n_seq == rows and t % SUBLANES == 0
    dense_in, dense_out = _dense_specs(t)
    attn_in, attn_out = _sample_attn_specs(cache_kt.shape[-1], steps, cached_tables, new_tables)
    return pl.pallas_call(
        _dense_and_sample_attn_kernel,
        grid=(n_seq,),
        in_specs=dense_in + attn_in,
        out_specs=[dense_out, attn_out],
        out_shape=[jax.ShapeDtypeStruct((rows, D_MODEL), F32), jax.ShapeDtypeStruct(q_s.shape, F32)],
        compiler_params=pltpu.CompilerParams(dimension_semantics=("arbitrary",), vmem_limit_bytes=VMEM_LIMIT),
        name="dense_tail_sample_attn",
    )(x, att, rnn, *weights, q_s, k_new, v_new, cache_kt, cache_vt, cached_tables, new_tables)


def _t5_bucket(dist):
    max_exact = N_BUCKETS // 2
    d_f = jnp.maximum(dist, max_exact).astype(F32)
    large = max_exact + (jnp.log(d_f / max_exact) / math.log(WIN_MAX / max_exact)
                         * (N_BUCKETS - max_exact)).astype(jnp.int32)
    large = jnp.minimum(large, N_BUCKETS - 1)
    return jnp.where(dist < max_exact, dist, large)


def _branch_bias(rel_bias, dil):
    dist = jnp.arange(SPAN + 1, dtype=jnp.int32) * dil
    return rel_bias[_t5_bucket(dist)].astype(F32).T


def _prompt_bias_rows(rel_bias):
    rows = []
    for dil in DILATIONS:
        bias = _branch_bias(rel_bias, dil)
        prev = bias[:, SPAN:0:-1]
        cur = jnp.concatenate([bias[:, 0:1], bias[:, SPAN - 1:0:-1]], axis=1)
        rows.append(jnp.stack([prev, cur], axis=1))
    return jnp.stack(rows)


def _sample_bias_tables(rel_bias, n_buf, steps):
    cached, new = [], []
    pos = np.arange(n_buf)
    new_row = np.arange(SAMPLE_PAD)
    delta_new = np.arange(steps)[:, None] - new_row[None, :]
    for dil in DILATIONS:
        bias = _branch_bias(rel_bias, dil)
        every = [bias[:, SPAN:0:-1]] + [jnp.full((N_HEADS, SPAN), NEG_INF, F32)] * (dil - 1)
        row0 = jnp.concatenate([jnp.full((N_HEADS, n_buf - SPAN * dil), NEG_INF, F32),
                                jnp.stack(every, axis=-1).reshape(N_HEADS, SPAN * dil)], axis=1)
        rows = [jnp.where(pos[None, :] >= t, jnp.roll(row0, t, axis=1), NEG_INF) for t in range(steps)]
        cached.append(jnp.stack(rows, axis=1).reshape(N_HEADS * steps, n_buf))
        valid = (new_row[None, :] < steps) & (delta_new >= 0) & (delta_new % dil == 0)
        j_of = np.where(valid, delta_new // dil, -1)
        one_hot = (j_of[None] == np.arange(steps)[:, None, None]).astype(np.float32)
        vals = jnp.sum(bias[:, :steps, None, None] * one_hot[None], axis=1)
        new.append(jnp.where(valid[None], vals, NEG_INF).reshape(N_HEADS * steps, SAMPLE_PAD))
    return jnp.stack(cached), jnp.stack(new)


def _paired_gate_weights(gate_a_w, gate_x_w):
    def pair_diag(w):
        z = jnp.zeros((LRU_BLOCK, LRU_BLOCK), w.dtype)
        return jnp.stack([jnp.block([[w[2 * p], z], [z, w[2 * p + 1]]]) for p in range(N_LRU_BLOCKS // 2)])
    return jnp.concatenate([pair_diag(gate_a_w), pair_diag(gate_x_w)], axis=-1).astype(BF16)


def kernel(x_prompt, x_sample, cache_k, cache_v, state_conv, state_h, norm1_g, w_in, rel_bias, conv_w, conv_b,
           gate_a_w, gate_a_b, gate_x_w, gate_x_b, lru_lambda, att_out_g, rnn_out_g, w_out, norm2_g, w_mlp_in,
           w_mlp_out, final_g):
    depth = w_in.shape[0]
    assert depth == 1, "the final norm is fused into the single layer's dense kernel"
    b, s, _ = x_prompt.shape
    db, steps, _ = x_sample.shape
    n_buf = cache_k.shape[2]
    assert steps == SUBLANES and s == WIN_MAX and n_buf == WIN_MAX
    l = 0
    row = lambda v: v.reshape(1, -1).astype(F32)

    mix_w = (row(norm1_g[l]), w_in[l].astype(BF16), conv_w[l], row(conv_b[l]),
             _paired_gate_weights(gate_a_w[l], gate_x_w[l]), row(gate_a_b[l]), row(gate_x_b[l]),
             row(lru_lambda[l]), row(rnn_out_g[l]))
    dense_w = (row(att_out_g[l]), w_out[l, :ATT_WIDTH].astype(BF16), w_out[l, ATT_WIDTH:].astype(BF16),
               row(norm2_g[l]), w_mlp_in[l].astype(BF16), w_mlp_out[l].astype(BF16), row(final_g))

    xs = x_sample.reshape(db * steps, D_MODEL)
    state_rows = jnp.pad(state_conv[l], ((0, 0), (0, SUBLANES - (CONV_WIDTH - 1)), (0, 0))).reshape(db * steps, LRU_WIDTH)
    qs, ks, vs, rnn_s, conv_s, h_s = _sample_mix(xs, state_rows, state_h[l], mix_w)
    cache_kt = jnp.transpose(cache_k[l], (0, 2, 3, 1))
    cache_vt = jnp.transpose(cache_v[l], (0, 2, 3, 1))

    q, k, v, rnn, conv_p, h_p = _prompt_mix(x_prompt, mix_w)
    att = _prompt_attn(q, k, v, _prompt_bias_rows(rel_bias))
    y_prompt, att_s = _dense_and_sample_attn(
        x_prompt.reshape(b * s, D_MODEL), att.reshape(b * s, ATT_WIDTH), rnn.reshape(b * s, LRU_WIDTH), dense_w,
        qs, ks, vs, cache_kt, cache_vt, *_sample_bias_tables(rel_bias, n_buf, steps), steps)
    y_prompt = y_prompt.reshape(b, s, D_MODEL)
    y_sample = _dense(xs, att_s, rnn_s, dense_w).reshape(db, steps, D_MODEL)

    kv_p = (1, b, s, N_HEADS, HEAD_DIM)
    kv_s = (1, db, steps, N_HEADS, HEAD_DIM)
    return (y_prompt, y_sample,
            k.reshape(kv_p), v.reshape(kv_p), conv_p[None], h_p.reshape(1, b, LRU_WIDTH),
            ks.reshape(kv_s), vs.reshape(kv_s), conv_s[None], h_s[None])
```

```python
import functools
import math

import numpy as np
import jax
import jax.numpy as jnp
from jax import lax
from jax.experimental import pallas as pl
from jax.experimental.pallas import tpu as pltpu

F32 = jnp.float32
BF16 = jnp.bfloat16

D_MODEL = 1024
ATT_WIDTH = 512
LRU_WIDTH = 512
HEAD_DIM = 64
N_HEADS = 8
N_LRU_BLOCKS = 8
LRU_BLOCK = 64
CONV_WIDTH = 4
LRU_C = 8.0
D_FF = 4096
SPAN = 128
DILATIONS = (1, 4, 16)
WIN_MAX = 2048
N_BUCKETS = 32
NORM_EPS = 1e-6
NEG_INF = -1e30
Q_SCALE = HEAD_DIM ** -0.5

SUBLANES = 8
LANES = 128
HEADS_PER_TILE = LANES // HEAD_DIM
VMEM_LIMIT = 56 * 1024 * 1024

TOKEN_TILE = 512
PROMPT_MIX_TILE = 1024
PROMPT_SUB_TILE = 256
FF_CHUNK = 1024
SAMPLE_PAD = 128


def _dot(a, b):
    return jnp.dot(a, b, preferred_element_type=F32)


def _dot_nt(a, b):
    return lax.dot_general(a, b, (((1,), (1,)), ((), ())), preferred_element_type=F32)


def _rms(x, g):
    return x * lax.rsqrt(jnp.mean(x * x, axis=-1, keepdims=True) + NORM_EPS) * g


def _const_spec(shape):
    nd = len(shape)
    return pl.BlockSpec(shape, lambda *_: (0,) * nd, pipeline_mode=pl.Buffered(1))


def _lru_gates(xc, wg_ref, ba_ref, bx_ref, lam_ref, a_sc, u_sc):
    xcb = xc.astype(BF16)
    lam = lam_ref[...]
    softplus_neg_lam = jnp.maximum(-lam, 0.0) + jnp.log1p(jnp.exp(-jnp.abs(lam)))
    for p in range(LRU_WIDTH // LANES):
        cols = slice(p * LANES, (p + 1) * LANES)
        g = _dot(xcb[:, cols], wg_ref[p])
        r = jax.nn.sigmoid(g[:, :LANES] + ba_ref[:, cols])
        gi = jax.nn.sigmoid(g[:, LANES:] + bx_ref[:, cols])
        log_a = (-LRU_C) * r * softplus_neg_lam[:, cols]
        a = jnp.exp(log_a)
        a_sc[:, cols] = a
        u_sc[:, cols] = jnp.sqrt(-jnp.tanh(log_a) * (a * a + 1.0)) * (gi * xc[:, cols])


def _group_scan(a, u, row):
    for s in (1, 2, 4):
        keep = row >= s
        u = jnp.where(keep, a * pltpu.roll(u, s, 0) + u, u)
        a = jnp.where(keep, a * pltpu.roll(a, s, 0), a)
    return a, u


def _prompt_mix_kernel(x_ref, g1_ref, win_ref, cw_ref, cb_ref, wg_ref, ba_ref, bx_ref, lam_ref, grnn_ref,
                       q_ref, k_ref, v_ref, rnn_ref, conv_ref, h_ref,
                       xr_ext, gr_sc, a_sc, u_sc, hs_sc, h_carry):
    t = x_ref.shape[0]
    sub = PROMPT_SUB_TILE
    n_sub = t // sub

    @pl.when(pl.program_id(1) == 0)
    def _():
        xr_ext[0:SUBLANES, :] = jnp.zeros((SUBLANES, LRU_WIDTH), F32)
        h_carry[...] = jnp.zeros_like(h_carry)

    row = lax.broadcasted_iota(jnp.int32, (SUBLANES, LRU_WIDTH), 0)

    def project(s):
        rows = slice(s * sub, (s + 1) * sub)
        n = _rms(x_ref[rows, :], g1_ref[...]).astype(BF16)
        q_ref[rows, :] = _dot(n, win_ref[:, 0:ATT_WIDTH])
        k_ref[rows, :] = _dot(n, win_ref[:, ATT_WIDTH:2 * ATT_WIDTH])
        v_ref[rows, :] = _dot(n, win_ref[:, 2 * ATT_WIDTH:3 * ATT_WIDTH])
        xr_ext[SUBLANES + s * sub:SUBLANES + (s + 1) * sub, :] = _dot(
            n, win_ref[:, 3 * ATT_WIDTH:3 * ATT_WIDTH + LRU_WIDTH])
        gr_sc[rows, :] = _dot(n, win_ref[:, 3 * ATT_WIDTH + LRU_WIDTH:])

    def recur(s, h):
        rows = slice(s * sub, (s + 1) * sub)
        xc = cb_ref[...]
        for back in range(CONV_WIDTH):
            xc = xc + (cw_ref[CONV_WIDTH - 1 - back:CONV_WIDTH - back, :]
                       * xr_ext[pl.ds(SUBLANES + s * sub - back, sub), :])
        _lru_gates(xc, wg_ref, ba_ref, bx_ref, lam_ref, a_sc.at[rows], u_sc.at[rows])
        for g in range(sub // SUBLANES):
            grp = slice(s * sub + g * SUBLANES, s * sub + (g + 1) * SUBLANES)
            a, u = _group_scan(a_sc[grp, :], u_sc[grp, :], row)
            hs = a * h + u
            hs_sc[grp, :] = hs
            h = jnp.broadcast_to(hs[SUBLANES - 1:SUBLANES, :], hs.shape)
        rnn_ref[rows, :] = _rms(hs_sc[rows, :] * jax.nn.gelu(gr_sc[rows, :]), grnn_ref[...]).astype(BF16)
        return h

    h = h_carry[...]
    project(0)
    for s in range(1, n_sub):
        project(s)
        h = recur(s - 1, h)
    h = recur(n_sub - 1, h)
    h_carry[...] = h
    h_ref[...] = h[0:1, :]
    conv_ref[...] = xr_ext[SUBLANES + t - (CONV_WIDTH - 1):SUBLANES + t, :]
    xr_ext[0:SUBLANES, :] = xr_ext[t:t + SUBLANES, :]


def _sample_mix_kernel(x_ref, st_ref, h0_ref, g1_ref, win_ref, cw_ref, cb_ref, wg_ref, ba_ref, bx_ref, lam_ref,
                       grnn_ref, q_ref, k_ref, v_ref, rnn_ref, conv_ref, h_ref,
                       xr_sc, xc_sc, a_sc, u_sc, hs_sc):
    t = x_ref.shape[0]
    n_seq = t // SUBLANES
    n = _rms(x_ref[...], g1_ref[...]).astype(BF16)
    q_ref[...] = _dot(n, win_ref[:, 0:ATT_WIDTH])
    k_ref[...] = _dot(n, win_ref[:, ATT_WIDTH:2 * ATT_WIDTH])
    v_ref[...] = _dot(n, win_ref[:, 2 * ATT_WIDTH:3 * ATT_WIDTH])
    xr_sc[...] = _dot(n, win_ref[:, 3 * ATT_WIDTH:3 * ATT_WIDTH + LRU_WIDTH])

    row = lax.broadcasted_iota(jnp.int32, (SUBLANES, LRU_WIDTH), 0)
    n_state = CONV_WIDTH - 1

    def conv_group(b, carry):
        rows = pl.ds(pl.multiple_of(b * SUBLANES, SUBLANES), SUBLANES)
        xg = xr_sc[rows, :]
        sg = st_ref[rows, :]
        xc = cb_ref[...] + cw_ref[CONV_WIDTH - 1:CONV_WIDTH, :] * xg
        for back in range(1, CONV_WIDTH):
            prev = jnp.where(row >= back, pltpu.roll(xg, back, 0),
                             pltpu.roll(sg, (back - n_state) % SUBLANES, 0))
            xc = xc + cw_ref[CONV_WIDTH - 1 - back:CONV_WIDTH - back, :] * prev
        xc_sc[rows, :] = xc
        conv_ref[b] = xg[SUBLANES - n_state:, :]
        return carry

    lax.fori_loop(0, n_seq, conv_group, 0, unroll=8)

    _lru_gates(xc_sc[...], wg_ref, ba_ref, bx_ref, lam_ref, a_sc, u_sc)

    def scan_group(b, carry):
        rows = pl.ds(pl.multiple_of(b * SUBLANES, SUBLANES), SUBLANES)
        a, u = _group_scan(a_sc[rows, :], u_sc[rows, :], row)
        hs = a * h0_ref[pl.ds(b, 1), :] + u
        hs_sc[rows, :] = hs
        h_ref[pl.ds(b, 1), :] = hs[SUBLANES - 1:SUBLANES, :]
        return carry

    lax.fori_loop(0, n_seq, scan_group, 0, unroll=8)

    gr = _dot(n, win_ref[:, 3 * ATT_WIDTH + LRU_WIDTH:])
    rnn_ref[...] = _rms(hs_sc[...] * jax.nn.gelu(gr), grnn_ref[...]).astype(BF16)


def _mix_weight_specs():
    return [
        _const_spec((1, D_MODEL)),
        _const_spec((D_MODEL, 3 * ATT_WIDTH + 2 * LRU_WIDTH)),
        _const_spec((CONV_WIDTH, LRU_WIDTH)),
        _const_spec((1, LRU_WIDTH)),
        _const_spec((LRU_WIDTH // LANES, LANES, 2 * LANES)),
        _const_spec((1, LRU_WIDTH)),
        _const_spec((1, LRU_WIDTH)),
        _const_spec((1, LRU_WIDTH)),
        _const_spec((1, LRU_WIDTH)),
    ]


def _prompt_mix(x, weights):
    b, s, _ = x.shape
    t = PROMPT_MIX_TILE
    assert s % t == 0 and t % PROMPT_SUB_TILE == 0
    tok = lambda width: pl.BlockSpec((None, t, width), lambda i, j: (i, j, 0))
    per_seq = lambda rows: pl.BlockSpec((None, rows, LRU_WIDTH), lambda i, j: (i, 0, 0))
    return pl.pallas_call(
        _prompt_mix_kernel,
        grid=(b, s // t),
        in_specs=[tok(D_MODEL)] + _mix_weight_specs(),
        out_specs=[tok(ATT_WIDTH), tok(ATT_WIDTH), tok(ATT_WIDTH), tok(LRU_WIDTH),
                   per_seq(CONV_WIDTH - 1), per_seq(1)],
        out_shape=[jax.ShapeDtypeStruct((b, s, ATT_WIDTH), F32)] * 3
        + [jax.ShapeDtypeStruct((b, s, LRU_WIDTH), BF16),
           jax.ShapeDtypeStruct((b, CONV_WIDTH - 1, LRU_WIDTH), F32),
           jax.ShapeDtypeStruct((b, 1, LRU_WIDTH), F32)],
        scratch_shapes=[pltpu.VMEM((t + SUBLANES, LRU_WIDTH), F32)] + [pltpu.VMEM((t, LRU_WIDTH), F32)] * 4
        + [pltpu.VMEM((SUBLANES, LRU_WIDTH), F32)],
        compiler_params=pltpu.CompilerParams(dimension_semantics=("arbitrary", "arbitrary"),
                                             vmem_limit_bytes=VMEM_LIMIT),
        name="prompt_mix",
    )(x, *weights)


def _sample_mix(x, state_rows, h0, weights):
    rows = x.shape[0]
    t = min(TOKEN_TILE, rows)
    n_seq = t // SUBLANES
    tok = lambda width: pl.BlockSpec((t, width), lambda i: (i, 0))
    return pl.pallas_call(
        _sample_mix_kernel,
        grid=(rows // t,),
        in_specs=[tok(D_MODEL), tok(LRU_WIDTH), pl.BlockSpec((n_seq, LRU_WIDTH), lambda i: (i, 0))]
        + _mix_weight_specs(),
        out_specs=[tok(ATT_WIDTH), tok(ATT_WIDTH), tok(ATT_WIDTH), tok(LRU_WIDTH),
                   pl.BlockSpec((n_seq, CONV_WIDTH - 1, LRU_WIDTH), lambda i: (i, 0, 0)),
                   pl.BlockSpec((n_seq, LRU_WIDTH), lambda i: (i, 0))],
        out_shape=[jax.ShapeDtypeStruct((rows, ATT_WIDTH), F32)] * 3
        + [jax.ShapeDtypeStruct((rows, LRU_WIDTH), BF16),
           jax.ShapeDtypeStruct((rows // SUBLANES, CONV_WIDTH - 1, LRU_WIDTH), F32),
           jax.ShapeDtypeStruct((rows // SUBLANES, LRU_WIDTH), F32)],
        scratch_shapes=[pltpu.VMEM((t, LRU_WIDTH), F32)] * 5,
        compiler_params=pltpu.CompilerParams(dimension_semantics=("arbitrary",), vmem_limit_bytes=VMEM_LIMIT),
        name="sample_mix",
    )(x, state_rows, h0, *weights)


def _prompt_attn_kernel(q_ref, k_ref, v_ref, brow_ref, o_ref, tab_sc, qp_sc, kp_sc, vp_sc, pv_sc, m_sc, den_sc):
    s_len = q_ref.shape[0]
    n_branch = len(DILATIONS)
    qi = lax.broadcasted_iota(jnp.int32, (SPAN, SPAN), 0)
    ki = lax.broadcasted_iota(jnp.int32, (SPAN, SPAN), 1)
    low_head = ki < HEAD_DIM

    for g in range(n_branch):
        for h in range(HEADS_PER_TILE):
            rows = slice(h * SPAN, (h + 1) * SPAN)
            prev = pltpu.roll(jnp.broadcast_to(brow_ref[g, h, 0:1, :], (SPAN, SPAN)), 0, 1, stride=1, stride_axis=0)
            cur = pltpu.roll(jnp.broadcast_to(brow_ref[g, h, 1:2, :], (SPAN, SPAN)), 0, 1, stride=1, stride_axis=0)
            tab_sc[g, rows, 0:SPAN] = jnp.where(ki >= qi, prev, NEG_INF)
            tab_sc[g, rows, SPAN:] = jnp.where(ki <= qi, cur, NEG_INF)

    perm = DILATIONS[1]
    sub_len = s_len // perm
    assert DILATIONS == (1, perm, perm * perm) and sub_len % (2 * SPAN) == 0
    natural = (q_ref, k_ref, v_ref)
    permuted = (qp_sc, kp_sc, vp_sc)
    for src, dst in zip(natural, permuted):
        for c in range(perm):
            for i0 in range(0, sub_len, 2 * SPAN):
                dst[c * sub_len + i0:c * sub_len + i0 + 2 * SPAN, :] = src[pl.ds(c + perm * i0, 2 * SPAN, stride=perm), :]

    def attend(branch, refs, q_rows, k_rows, first):
        qr, kr, vr = refs
        n_keys = SPAN if first else 2 * SPAN
        q = qr[q_rows, :] * Q_SCALE
        q2 = jnp.concatenate([jnp.where(low_head, q, 0.0), jnp.where(low_head, 0.0, q)], axis=0).astype(BF16)
        kk = kr[k_rows, :].astype(BF16)
        v_aug = jnp.concatenate([vr[k_rows, :], jnp.ones((n_keys, LANES), F32)], axis=1).astype(BF16)
        table = tab_sc[branch, :, SPAN:] if first else tab_sc[branch]
        logits = _dot_nt(q2, kk) + table
        m = jnp.max(logits, axis=-1, keepdims=True)
        p = jnp.exp(logits - m).astype(BF16)
        r = _dot(p, v_aug)
        m_b = jnp.broadcast_to(m, (2 * SPAN, LANES))
        pv_sc[branch, q_rows, :] = jnp.where(low_head, r[:SPAN, :LANES], r[SPAN:, :LANES])
        den_sc[branch, q_rows, :] = jnp.where(low_head, r[:SPAN, LANES:], r[SPAN:, LANES:])
        m_sc[branch, q_rows, :] = jnp.where(low_head, m_b[:SPAN], m_b[SPAN:])

    for i in range(s_len // SPAN):
        if i == 0:
            attend(0, natural, pl.ds(0, SPAN), pl.ds(0, SPAN), True)
        else:
            attend(0, natural, pl.ds(i * SPAN, SPAN), pl.ds((i - 1) * SPAN, 2 * SPAN), False)
    for c in range(perm):
        for i in range(sub_len // SPAN):
            q0 = c * sub_len + i * SPAN
            if i == 0:
                attend(1, permuted, pl.ds(q0, SPAN), pl.ds(q0, SPAN), True)
            else:
                attend(1, permuted, pl.ds(q0, SPAN), pl.ds(q0 - SPAN, 2 * SPAN), False)
    assert sub_len // perm == SPAN
    for c in range(perm):
        for c_hi in range(perm):
            rows = pl.ds(c * sub_len + c_hi, SPAN, stride=perm)
            attend(2, permuted, rows, rows, True)

    chunk = 2 * SPAN
    for c in range(perm):
        for i0 in range(0, sub_len, chunk):
            rows = (pl.ds(c + perm * i0, chunk, stride=perm),) + (pl.ds(c * sub_len + i0, chunk),) * (n_branch - 1)
            ms = [m_sc[g, rows[g], :] for g in range(n_branch)]
            m_all = jnp.maximum(jnp.maximum(ms[0], ms[1]), ms[2])
            num = jnp.zeros((chunk, LANES), F32)
            den = jnp.zeros((chunk, LANES), F32)
            for g in range(n_branch):
                w = jnp.exp(ms[g] - m_all)
                num = num + w * pv_sc[g, rows[g], :]
                den = den + w * den_sc[g, rows[g], :]
            o_ref[rows[0], :] = num / den


def _prompt_attn(q, k, v, bias_rows):
    b, s, _ = q.shape
    n_branch = len(DILATIONS)
    head_pair = pl.BlockSpec((None, s, LANES), lambda i, j: (i, 0, j))
    return pl.pallas_call(
        _prompt_attn_kernel,
        grid=(b, ATT_WIDTH // LANES),
        in_specs=[head_pair, head_pair, head_pair,
                  pl.BlockSpec((n_branch, HEADS_PER_TILE, 2, SPAN), lambda i, j: (0, j, 0, 0))],
        out_specs=head_pair,
        out_shape=jax.ShapeDtypeStruct((b, s, ATT_WIDTH), F32),
        scratch_shapes=[pltpu.VMEM((n_branch, HEADS_PER_TILE * SPAN, 2 * SPAN), F32)]
        + [pltpu.VMEM((s, LANES), F32)] * 3 + [pltpu.VMEM((n_branch, s, LANES), F32)] * 3,
        compiler_params=pltpu.CompilerParams(dimension_semantics=("arbitrary", "arbitrary"),
                                             vmem_limit_bytes=VMEM_LIMIT),
        name="prompt_attn",
    )(q, k, v, bias_rows)


def _sample_attn_stages(q_ref, kn_ref, vn_ref, ckt_ref, cvt_ref, tc_ref, tn_ref, o_ref):
    t = q_ref.shape[0]
    n_buf = ckt_ref.shape[-1]
    n_branch = len(DILATIONS)
    head_of_lane = lax.broadcasted_iota(jnp.int32, (t, ATT_WIDTH), 1) // HEAD_DIM
    q = q_ref[...] * Q_SCALE
    q_bd = jnp.concatenate([jnp.where(head_of_lane == h, q, 0.0) for h in range(N_HEADS)], axis=0).astype(BF16)
    pad = jnp.zeros((SAMPLE_PAD - t, ATT_WIDTH), F32)
    k_new = jnp.concatenate([kn_ref[...], pad], axis=0).astype(BF16)
    v_new = jnp.concatenate([vn_ref[...], pad], axis=0).astype(BF16)

    kt = ckt_ref[...].reshape(ATT_WIDTH, n_buf).astype(BF16)
    half = n_buf // 2
    s_c = jnp.concatenate([_dot(q_bd, kt[:, :half]), _dot(q_bd, kt[:, half:])], axis=1)
    s_n = _dot_nt(q_bd, k_new)
    yield
    m = None
    for g in range(n_branch):
        m_g = jnp.maximum(jnp.max(s_c + tc_ref[g], axis=-1, keepdims=True),
                          jnp.max(s_n + tn_ref[g], axis=-1, keepdims=True))
        m = m_g if m is None else jnp.maximum(m, m_g)
    p_c = jnp.exp(s_c + tc_ref[0] - m)
    p_n = jnp.exp(s_n + tn_ref[0] - m)
    for g in range(1, n_branch):
        p_c = p_c + jnp.exp(s_c + tc_ref[g] - m)
        p_n = p_n + jnp.exp(s_n + tn_ref[g] - m)
    den = jnp.sum(p_c, axis=-1, keepdims=True) + jnp.sum(p_n, axis=-1, keepdims=True)
    yield
    vt = cvt_ref[...].reshape(ATT_WIDTH, n_buf).astype(BF16)
    p_cb = p_c.astype(BF16)
    out = (_dot_nt(p_cb[:, :half], vt[:, :half]) + _dot_nt(p_cb[:, half:], vt[:, half:])
           + _dot(p_n.astype(BF16), v_new)) / den
    att = jnp.zeros((t, ATT_WIDTH), F32)
    for h in range(N_HEADS):
        att = att + jnp.where(head_of_lane == h, out[h * t:(h + 1) * t, :], 0.0)
    o_ref[...] = att


def _sample_attn_specs(n_buf, steps, cached_tables, new_tables):
    tok = pl.BlockSpec((steps, ATT_WIDTH), lambda i: (i, 0))
    cache = pl.BlockSpec((None, N_HEADS, HEAD_DIM, n_buf), lambda i: (i, 0, 0, 0))
    return [tok, tok, tok, cache, cache, _const_spec(cached_tables.shape), _const_spec(new_tables.shape)], tok


def _dense_stages(x_ref, att_ref, rnn_ref, gatt_ref, woa_ref, wor_ref, g2_ref, w1_ref, w2_ref, gf_ref, y_ref):
    att_n = _rms(att_ref[...], gatt_ref[...]).astype(BF16)
    x1 = x_ref[...] + _dot(att_n, woa_ref[...]) + _dot(rnn_ref[...], wor_ref[...])
    n2 = _rms(x1, g2_ref[...]).astype(BF16)
    yield
    mlp = None
    for c in range(D_FF // FF_CHUNK):
        cols = slice(c * FF_CHUNK, (c + 1) * FF_CHUNK)
        hmid = jnp.maximum(_dot(n2, w1_ref[:, cols]), 0.0)
        part = _dot((hmid * hmid).astype(BF16), w2_ref[cols, :])
        mlp = part if mlp is None else mlp + part
        yield
    y_ref[...] = _rms(x1 + mlp, gf_ref[...])


def _interleave(*stage_generators):
    pending = list(stage_generators)
    while pending:
        pending = [g for g in pending if next(g, StopIteration) is not StopIteration]


def _dense_kernel(*refs):
    _interleave(_dense_stages(*refs))


N_DENSE_IN = 10


def _dense_and_sample_attn_kernel(*refs):
    dense_in, attn_in = refs[:N_DENSE_IN], refs[N_DENSE_IN:-2]
    y_ref, att_s_ref = refs[-2:]
    _interleave(_sample_attn_stages(*attn_in, att_s_ref), _dense_stages(*dense_in, y_ref))


def _dense_specs(t):
    tok = lambda width: pl.BlockSpec((t, width), lambda i: (i, 0))
    in_specs = [tok(D_MODEL), tok(ATT_WIDTH), tok(LRU_WIDTH),
                _const_spec((1, ATT_WIDTH)), _const_spec((ATT_WIDTH, D_MODEL)), _const_spec((LRU_WIDTH, D_MODEL)),
                _const_spec((1, D_MODEL)), _const_spec((D_MODEL, D_FF)), _const_spec((D_FF, D_MODEL)),
                _const_spec((1, D_MODEL))]
    assert len(in_specs) == N_DENSE_IN
    return in_specs, tok(D_MODEL)


def _dense(x, att, rnn, weights):
    rows = x.shape[0]
    t = min(TOKEN_TILE, rows)
    in_specs, out_spec = _dense_specs(t)
    return pl.pallas_call(
        _dense_kernel,
        grid=(rows // t,),
        in_specs=in_specs,
        out_specs=out_spec,
        out_shape=jax.ShapeDtypeStruct((rows, D_MODEL), F32),
        compiler_params=pltpu.CompilerParams(dimension_semantics=("arbitrary",), vmem_limit_bytes=VMEM_LIMIT),
        name="dense_tail",
    )(x, att, rnn, *weights)


def _dense_and_sample_attn(x, att, rnn, weights, q_s, k_new, v_new, cache_kt, cache_vt, cached_tables, new_tables,
                           steps):
    rows = x.shape[0]
    n_seq = q_s.shape[0] // steps
    t = rows // n_seq
    assert t * n_seq == rows and t % SUBLANES == 0
    dense_in, dense_out = _dense_specs(t)
    attn_in, attn_out = _sample_attn_specs(cache_kt.shape[-1], steps, cached_tables, new_tables)
    return pl.pallas_call(
        _dense_and_sample_attn_kernel,
        grid=(n_seq,),
        in_specs=dense_in + attn_in,
        out_specs=[dense_out, attn_out],
        out_shape=[jax.ShapeDtypeStruct((rows, D_MODEL), F32), jax.ShapeDtypeStruct(q_s.shape, F32)],
        compiler_params=pltpu.CompilerParams(dimension_semantics=("arbitrary",), vmem_limit_bytes=VMEM_LIMIT),
        name="dense_tail_sample_attn",
    )(x, att, rnn, *weights, q_s, k_new, v_new, cache_kt, cache_vt, cached_tables, new_tables)


def _t5_bucket(dist):
    max_exact = N_BUCKETS // 2
    d_f = jnp.maximum(dist, max_exact).astype(F32)
    large = max_exact + (jnp.log(d_f / max_exact) / math.log(WIN_MAX / max_exact)
                         * (N_BUCKETS - max_exact)).astype(jnp.int32)
    large = jnp.minimum(large, N_BUCKETS - 1)
    return jnp.where(dist < max_exact, dist, large)


def _branch_bias(rel_bias, dil):
    dist = jnp.arange(SPAN + 1, dtype=jnp.int32) * dil
    return rel_bias[_t5_bucket(dist)].astype(F32).T


def _prompt_bias_rows(rel_bias):
    rows = []
    for dil in DILATIONS:
        bias = _branch_bias(rel_bias, dil)
        prev = bias[:, SPAN:0:-1]
        cur = jnp.concatenate([bias[:, 0:1], bias[:, SPAN - 1:0:-1]], axis=1)
        rows.append(jnp.stack([prev, cur], axis=1))
    return jnp.stack(rows)


def _sample_bias_tables(rel_bias, n_buf, steps):
    cached, new = [], []
    pos = np.arange(n_buf)
    new_row = np.arange(SAMPLE_PAD)
    delta_new = np.arange(steps)[:, None] - new_row[None, :]
    for dil in DILATIONS:
        bias = _branch_bias(rel_bias, dil)
        every = [bias[:, SPAN:0:-1]] + [jnp.full((N_HEADS, SPAN), NEG_INF, F32)] * (dil - 1)
        row0 = jnp.concatenate([jnp.full((N_HEADS, n_buf - SPAN * dil), NEG_INF, F32),
                                jnp.stack(every, axis=-1).reshape(N_HEADS, SPAN * dil)], axis=1)
        rows = [jnp.where(pos[None, :] >= t, jnp.roll(row0, t, axis=1), NEG_INF) for t in range(steps)]
        cached.append(jnp.stack(rows, axis=1).reshape(N_HEADS * steps, n_buf))
        valid = (new_row[None, :] < steps) & (delta_new >= 0) & (delta_new % dil == 0)
        j_of = np.where(valid, delta_new // dil, -1)
        one_hot = (j_of[None] == np.arange(steps)[:, None, None]).astype(np.float32)
        vals = jnp.sum(bias[:, :steps, None, None] * one_hot[None], axis=1)
        new.append(jnp.where(valid[None], vals, NEG_INF).reshape(N_HEADS * steps, SAMPLE_PAD))
    return jnp.stack(cached), jnp.stack(new)


def _paired_gate_weights(gate_a_w, gate_x_w):
    def pair_diag(w):
        z = jnp.zeros((LRU_BLOCK, LRU_BLOCK), w.dtype)
        return jnp.stack([jnp.block([[w[2 * p], z], [z, w[2 * p + 1]]]) for p in range(N_LRU_BLOCKS // 2)])
    return jnp.concatenate([pair_diag(gate_a_w), pair_diag(gate_x_w)], axis=-1).astype(BF16)


def kernel(x_prompt, x_sample, cache_k, cache_v, state_conv, state_h, norm1_g, w_in, rel_bias, conv_w, conv_b,
           gate_a_w, gate_a_b, gate_x_w, gate_x_b, lru_lambda, att_out_g, rnn_out_g, w_out, norm2_g, w_mlp_in,
           w_mlp_out, final_g):
    depth = w_in.shape[0]
    assert depth == 1, "the final norm is fused into the single layer's dense kernel"
    b, s, _ = x_prompt.shape
    db, steps, _ = x_sample.shape
    n_buf = cache_k.shape[2]
    assert steps == SUBLANES and s == WIN_MAX and n_buf == WIN_MAX
    l = 0
    row = lambda v: v.reshape(1, -1).astype(F32)

    mix_w = (row(norm1_g[l]), w_in[l].astype(BF16), conv_w[l], row(conv_b[l]),
             _paired_gate_weights(gate_a_w[l], gate_x_w[l]), row(gate_a_b[l]), row(gate_x_b[l]),
             row(lru_lambda[l]), row(rnn_out_g[l]))
    dense_w = (row(att_out_g[l]), w_out[l, :ATT_WIDTH].astype(BF16), w_out[l, ATT_WIDTH:].astype(BF16),
               row(norm2_g[l]), w_mlp_in[l].astype(BF16), w_mlp_out[l].astype(BF16), row(final_g))

    xs = x_sample.reshape(db * steps, D_MODEL)
    state_rows = jnp.pad(state_conv[l], ((0, 0), (0, SUBLANES - (CONV_WIDTH - 1)), (0, 0))).reshape(db * steps, LRU_WIDTH)
    qs, ks, vs, rnn_s, conv_s, h_s = _sample_mix(xs, state_rows, state_h[l], mix_w)
    cache_kt = jnp.transpose(cache_k[l], (0, 2, 3, 1))
    cache_vt = jnp.transpose(cache_v[l], (0, 2, 3, 1))

    q, k, v, rnn, conv_p, h_p = _prompt_mix(x_prompt, mix_w)
    att = _prompt_attn(q, k, v, _prompt_bias_rows(rel_bias))
    y_prompt, att_s = _dense_and_sample_attn(
        x_prompt.reshape(b * s, D_MODEL), att.reshape(b * s, ATT_WIDTH), rnn.reshape(b * s, LRU_WIDTH), dense_w,
        qs, ks, vs, cache_kt, cache_vt, *_sample_bias_tables(rel_bias, n_buf, steps), steps)
    y_prompt = y_prompt.reshape(b, s, D_MODEL)
    y_sample = _dense(xs, att_s, rnn_s, dense_w).reshape(db, steps, D_MODEL)

    kv_p = (1, b, s, N_HEADS, HEAD_DIM)
    kv_s = (1, db, steps, N_HEADS, HEAD_DIM)
    return (y_prompt, y_sample,
            k.reshape(kv_p), v.reshape(kv_p), conv_p[None], h_p.reshape(1, b, LRU_WIDTH),
            ks.reshape(kv_s), vs.reshape(kv_s), conv_s[None], h_s[None])
```

```python
import functools
import math

import numpy as np
import jax
import jax.numpy as jnp
from jax import lax
from jax.experimental import pallas as pl
from jax.experimental.pallas import tpu as pltpu

F32 = jnp.float32
BF16 = jnp.bfloat16

D_MODEL = 1024
ATT_WIDTH = 512
LRU_WIDTH = 512
HEAD_DIM = 64
N_HEADS = 8
N_LRU_BLOCKS = 8
LRU_BLOCK = 64
CONV_WIDTH = 4
LRU_C = 8.0
D_FF = 4096
SPAN = 128
DILATIONS = (1, 4, 16)
WIN_MAX = 2048
N_BUCKETS = 32
NORM_EPS = 1e-6
NEG_INF = -1e30
Q_SCALE = HEAD_DIM ** -0.5

SUBLANES = 8
LANES = 128
HEADS_PER_TILE = LANES // HEAD_DIM
VMEM_LIMIT = 56 * 1024 * 1024

TOKEN_TILE = 512
PROMPT_MIX_TILE = 1024
PROMPT_SUB_TILE = 256
FF_CHUNK = 1024
SAMPLE_PAD = 128


def _dot(a, b):
    return jnp.dot(a, b, preferred_element_type=F32)


def _dot_nt(a, b):
    return lax.dot_general(a, b, (((1,), (1,)), ((), ())), preferred_element_type=F32)


def _rms(x, g):
    return x * lax.rsqrt(jnp.mean(x * x, axis=-1, keepdims=True) + NORM_EPS) * g


def _const_spec(shape):
    nd = len(shape)
    return pl.BlockSpec(shape, lambda *_: (0,) * nd, pipeline_mode=pl.Buffered(1))


def _sigmoid_of_twice(half_x):
    return 0.5 * jnp.tanh(half_x) + 0.5


def _lru_gates(xc, wg_ref, ba_ref, bx_ref, lam_ref, a_sc, u_sc):
    xcb = xc.astype(BF16)
    lam = lam_ref[...]
    softplus_neg_lam = jnp.maximum(-lam, 0.0) + jnp.log1p(jnp.exp(-jnp.abs(lam)))
    log_a_per_r = (-LRU_C) * softplus_neg_lam
    for p in range(LRU_WIDTH // LANES):
        cols = slice(p * LANES, (p + 1) * LANES)
        g = _dot(xcb[:, cols], wg_ref[p])
        r = _sigmoid_of_twice(g[:, :LANES] + ba_ref[:, cols])
        gi = _sigmoid_of_twice(g[:, LANES:] + bx_ref[:, cols])
        log_a = r * log_a_per_r[:, cols]
        a = jnp.exp(log_a)
        a_sc[:, cols] = a
        u_sc[:, cols] = jnp.sqrt(-jnp.tanh(log_a) * (a * a + 1.0)) * (gi * xc[:, cols])


def _group_scan(a, u, row):
    for s in (1, 2, 4):
        keep = row >= s
        u = jnp.where(keep, a * pltpu.roll(u, s, 0) + u, u)
        a = jnp.where(keep, a * pltpu.roll(a, s, 0), a)
    return a, u


def _prompt_mix_kernel(x_ref, g1_ref, win_ref, cw_ref, cb_ref, wg_ref, ba_ref, bx_ref, lam_ref, grnn_ref,
                       q_ref, k_ref, v_ref, rnn_ref, conv_ref, h_ref,
                       xr_ext, gr_sc, a_sc, u_sc, hs_sc, h_carry):
    t = x_ref.shape[0]
    sub = PROMPT_SUB_TILE
    n_sub = t // sub

    @pl.when(pl.program_id(1) == 0)
    def _():
        xr_ext[0:SUBLANES, :] = jnp.zeros((SUBLANES, LRU_WIDTH), F32)
        h_carry[...] = jnp.zeros_like(h_carry)

    row = lax.broadcasted_iota(jnp.int32, (SUBLANES, LRU_WIDTH), 0)

    def project(s):
        rows = slice(s * sub, (s + 1) * sub)
        n = _rms(x_ref[rows, :], g1_ref[...]).astype(BF16)
        q_ref[rows, :] = _dot(n, win_ref[:, 0:ATT_WIDTH])
        k_ref[rows, :] = _dot(n, win_ref[:, ATT_WIDTH:2 * ATT_WIDTH])
        v_ref[rows, :] = _dot(n, win_ref[:, 2 * ATT_WIDTH:3 * ATT_WIDTH])
        xr_ext[SUBLANES + s * sub:SUBLANES + (s + 1) * sub, :] = _dot(
            n, win_ref[:, 3 * ATT_WIDTH:3 * ATT_WIDTH + LRU_WIDTH])
        gr_sc[rows, :] = _dot(n, win_ref[:, 3 * ATT_WIDTH + LRU_WIDTH:])

    def recur(s, h):
        rows = slice(s * sub, (s + 1) * sub)
        xc = cb_ref[...]
        for back in range(CONV_WIDTH):
            xc = xc + (cw_ref[CONV_WIDTH - 1 - back:CONV_WIDTH - back, :]
                       * xr_ext[pl.ds(SUBLANES + s * sub - back, sub), :])
        _lru_gates(xc, wg_ref, ba_ref, bx_ref, lam_ref, a_sc.at[rows], u_sc.at[rows])
        for g in range(sub // SUBLANES):
            grp = slice(s * sub + g * SUBLANES, s * sub + (g + 1) * SUBLANES)
            a, u = _group_scan(a_sc[grp, :], u_sc[grp, :], row)
            hs = a * h + u
            hs_sc[grp, :] = hs
            h = jnp.broadcast_to(hs[SUBLANES - 1:SUBLANES, :], hs.shape)
        rnn_ref[rows, :] = _rms(hs_sc[rows, :] * jax.nn.gelu(gr_sc[rows, :]), grnn_ref[...]).astype(BF16)
        return h

    h = h_carry[...]
    project(0)
    for s in range(1, n_sub):
        project(s)
        h = recur(s - 1, h)
    h = recur(n_sub - 1, h)
    h_carry[...] = h
    h_ref[...] = h[0:1, :]
    conv_ref[...] = xr_ext[SUBLANES + t - (CONV_WIDTH - 1):SUBLANES + t, :]
    xr_ext[0:SUBLANES, :] = xr_ext[t:t + SUBLANES, :]


def _sample_mix_kernel(x_ref, st_ref, h0_ref, g1_ref, win_ref, cw_ref, cb_ref, wg_ref, ba_ref, bx_ref, lam_ref,
                       grnn_ref, q_ref, k_ref, v_ref, rnn_ref, conv_ref, h_ref,
                       xr_sc, xc_sc, a_sc, u_sc, hs_sc):
    t = x_ref.shape[0]
    n_seq = t // SUBLANES
    n = _rms(x_ref[...], g1_ref[...]).astype(BF16)
    q_ref[...] = _dot(n, win_ref[:, 0:ATT_WIDTH])
    k_ref[...] = _dot(n, win_ref[:, ATT_WIDTH:2 * ATT_WIDTH])
    v_ref[...] = _dot(n, win_ref[:, 2 * ATT_WIDTH:3 * ATT_WIDTH])
    xr_sc[...] = _dot(n, win_ref[:, 3 * ATT_WIDTH:3 * ATT_WIDTH + LRU_WIDTH])

    row = lax.broadcasted_iota(jnp.int32, (SUBLANES, LRU_WIDTH), 0)
    n_state = CONV_WIDTH - 1

    def conv_group(b, carry):
        rows = pl.ds(pl.multiple_of(b * SUBLANES, SUBLANES), SUBLANES)
        xg = xr_sc[rows, :]
        sg = st_ref[rows, :]
        xc = cb_ref[...] + cw_ref[CONV_WIDTH - 1:CONV_WIDTH, :] * xg
        for back in range(1, CONV_WIDTH):
            prev = jnp.where(row >= back, pltpu.roll(xg, back, 0),
                             pltpu.roll(sg, (back - n_state) % SUBLANES, 0))
            xc = xc + cw_ref[CONV_WIDTH - 1 - back:CONV_WIDTH - back, :] * prev
        xc_sc[rows, :] = xc
        conv_ref[b] = xg[SUBLANES - n_state:, :]
        return carry

    lax.fori_loop(0, n_seq, conv_group, 0, unroll=8)

    _lru_gates(xc_sc[...], wg_ref, ba_ref, bx_ref, lam_ref, a_sc, u_sc)

    def scan_group(b, carry):
        rows = pl.ds(pl.multiple_of(b * SUBLANES, SUBLANES), SUBLANES)
        a, u = _group_scan(a_sc[rows, :], u_sc[rows, :], row)
        hs = a * h0_ref[pl.ds(b, 1), :] + u
        hs_sc[rows, :] = hs
        h_ref[pl.ds(b, 1), :] = hs[SUBLANES - 1:SUBLANES, :]
        return carry

    lax.fori_loop(0, n_seq, scan_group, 0, unroll=8)

    gr = _dot(n, win_ref[:, 3 * ATT_WIDTH + LRU_WIDTH:])
    rnn_ref[...] = _rms(hs_sc[...] * jax.nn.gelu(gr), grnn_ref[...]).astype(BF16)


def _mix_weight_specs():
    return [
        _const_spec((1, D_MODEL)),
        _const_spec((D_MODEL, 3 * ATT_WIDTH + 2 * LRU_WIDTH)),
        _const_spec((CONV_WIDTH, LRU_WIDTH)),
        _const_spec((1, LRU_WIDTH)),
        _const_spec((LRU_WIDTH // LANES, LANES, 2 * LANES)),
        _const_spec((1, LRU_WIDTH)),
        _const_spec((1, LRU_WIDTH)),
        _const_spec((1, LRU_WIDTH)),
        _const_spec((1, LRU_WIDTH)),
    ]


def _prompt_mix(x, weights):
    b, s, _ = x.shape
    t = PROMPT_MIX_TILE
    assert s % t == 0 and t % PROMPT_SUB_TILE == 0
    tok = lambda width: pl.BlockSpec((None, t, width), lambda i, j: (i, j, 0))
    per_seq = lambda rows: pl.BlockSpec((None, rows, LRU_WIDTH), lambda i, j: (i, 0, 0))
    return pl.pallas_call(
        _prompt_mix_kernel,
        grid=(b, s // t),
        in_specs=[tok(D_MODEL)] + _mix_weight_specs(),
        out_specs=[tok(ATT_WIDTH), tok(ATT_WIDTH), tok(ATT_WIDTH), tok(LRU_WIDTH),
                   per_seq(CONV_WIDTH - 1), per_seq(1)],
        out_shape=[jax.ShapeDtypeStruct((b, s, ATT_WIDTH), F32)] * 3
        + [jax.ShapeDtypeStruct((b, s, LRU_WIDTH), BF16),
           jax.ShapeDtypeStruct((b, CONV_WIDTH - 1, LRU_WIDTH), F32),
           jax.ShapeDtypeStruct((b, 1, LRU_WIDTH), F32)],
        scratch_shapes=[pltpu.VMEM((t + SUBLANES, LRU_WIDTH), F32)] + [pltpu.VMEM((t, LRU_WIDTH), F32)] * 4
        + [pltpu.VMEM((SUBLANES, LRU_WIDTH), F32)],
        compiler_params=pltpu.CompilerParams(dimension_semantics=("arbitrary", "arbitrary"),
                                             vmem_limit_bytes=VMEM_LIMIT),
        name="prompt_mix",
    )(x, *weights)


def _sample_mix(x, state_rows, h0, weights):
    rows = x.shape[0]
    t = min(TOKEN_TILE, rows)
    n_seq = t // SUBLANES
    tok = lambda width: pl.BlockSpec((t, width), lambda i: (i, 0))
    return pl.pallas_call(
        _sample_mix_kernel,
        grid=(rows // t,),
        in_specs=[tok(D_MODEL), tok(LRU_WIDTH), pl.BlockSpec((n_seq, LRU_WIDTH), lambda i: (i, 0))]
        + _mix_weight_specs(),
        out_specs=[tok(ATT_WIDTH), tok(ATT_WIDTH), tok(ATT_WIDTH), tok(LRU_WIDTH),
                   pl.BlockSpec((n_seq, CONV_WIDTH - 1, LRU_WIDTH), lambda i: (i, 0, 0)),
                   pl.BlockSpec((n_seq, LRU_WIDTH), lambda i: (i, 0))],
        out_shape=[jax.ShapeDtypeStruct((rows, ATT_WIDTH), F32)] * 3
        + [jax.ShapeDtypeStruct((rows, LRU_WIDTH), BF16),
           jax.ShapeDtypeStruct((rows // SUBLANES, CONV_WIDTH - 1, LRU_WIDTH), F32),
           jax.ShapeDtypeStruct((rows // SUBLANES, LRU_WIDTH), F32)],
        scratch_shapes=[pltpu.VMEM((t, LRU_WIDTH), F32)] * 5,
        compiler_params=pltpu.CompilerParams(dimension_semantics=("arbitrary",), vmem_limit_bytes=VMEM_LIMIT),
        name="sample_mix",
    )(x, state_rows, h0, *weights)


def _prompt_attn_kernel(q_ref, k_ref, v_ref, brow_ref, o_ref, tab_sc, qp_sc, kp_sc, vp_sc, pv_sc, m_sc, den_sc):
    s_len = q_ref.shape[0]
    n_branch = len(DILATIONS)
    qi = lax.broadcasted_iota(jnp.int32, (SPAN, SPAN), 0)
    ki = lax.broadcasted_iota(jnp.int32, (SPAN, SPAN), 1)
    low_head = ki < HEAD_DIM

    for g in range(n_branch):
        for h in range(HEADS_PER_TILE):
            rows = slice(h * SPAN, (h + 1) * SPAN)
            prev = pltpu.roll(jnp.broadcast_to(brow_ref[g, h, 0:1, :], (SPAN, SPAN)), 0, 1, stride=1, stride_axis=0)
            cur = pltpu.roll(jnp.broadcast_to(brow_ref[g, h, 1:2, :], (SPAN, SPAN)), 0, 1, stride=1, stride_axis=0)
            tab_sc[g, rows, 0:SPAN] = jnp.where(ki >= qi, prev, NEG_INF)
            tab_sc[g, rows, SPAN:] = jnp.where(ki <= qi, cur, NEG_INF)

    perm = DILATIONS[1]
    sub_len = s_len // perm
    assert DILATIONS == (1, perm, perm * perm) and sub_len % (2 * SPAN) == 0
    natural = (q_ref, k_ref, v_ref)
    permuted = (qp_sc, kp_sc, vp_sc)
    for src, dst in zip(natural, permuted):
        for c in range(perm):
            for i0 in range(0, sub_len, 2 * SPAN):
                dst[c * sub_len + i0:c * sub_len + i0 + 2 * SPAN, :] = src[pl.ds(c + perm * i0, 2 * SPAN, stride=perm), :]

    def attend(branch, refs, q_rows, k_rows, first):
        qr, kr, vr = refs
        n_keys = SPAN if first else 2 * SPAN
        q = qr[q_rows, :] * Q_SCALE
        q2 = jnp.concatenate([jnp.where(low_head, q, 0.0), jnp.where(low_head, 0.0, q)], axis=0).astype(BF16)
        kk = kr[k_rows, :].astype(BF16)
        v_aug = jnp.concatenate([vr[k_rows, :], jnp.ones((n_keys, LANES), F32)], axis=1).astype(BF16)
        table = tab_sc[branch, :, SPAN:] if first else tab_sc[branch]
        logits = _dot_nt(q2, kk) + table
        m = jnp.max(logits, axis=-1, keepdims=True)
        p = jnp.exp(logits - m).astype(BF16)
        r = _dot(p, v_aug)
        m_b = jnp.broadcast_to(m, (2 * SPAN, LANES))
        pv_sc[branch, q_rows, :] = jnp.where(low_head, r[:SPAN, :LANES], r[SPAN:, :LANES])
        den_sc[branch, q_rows, :] = jnp.where(low_head, r[:SPAN, LANES:], r[SPAN:, LANES:])
        m_sc[branch, q_rows, :] = jnp.where(low_head, m_b[:SPAN], m_b[SPAN:])

    for i in range(s_len // SPAN):
        if i == 0:
            attend(0, natural, pl.ds(0, SPAN), pl.ds(0, SPAN), True)
        else:
            attend(0, natural, pl.ds(i * SPAN, SPAN), pl.ds((i - 1) * SPAN, 2 * SPAN), False)
    for c in range(perm):
        for i in range(sub_len // SPAN):
            q0 = c * sub_len + i * SPAN
            if i == 0:
                attend(1, permuted, pl.ds(q0, SPAN), pl.ds(q0, SPAN), True)
            else:
                attend(1, permuted, pl.ds(q0, SPAN), pl.ds(q0 - SPAN, 2 * SPAN), False)
    assert sub_len // perm == SPAN
    for c in range(perm):
        for c_hi in range(perm):
            rows = pl.ds(c * sub_len + c_hi, SPAN, stride=perm)
            attend(2, permuted, rows, rows, True)

    chunk = 2 * SPAN
    for c in range(perm):
        for i0 in range(0, sub_len, chunk):
            rows = (pl.ds(c + perm * i0, chunk, stride=perm),) + (pl.ds(c * sub_len + i0, chunk),) * (n_branch - 1)
            ms = [m_sc[g, rows[g], :] for g in range(n_branch)]
            m_all = jnp.maximum(jnp.maximum(ms[0], ms[1]), ms[2])
            num = jnp.zeros((chunk, LANES), F32)
            den = jnp.zeros((chunk, LANES), F32)
            for g in range(n_branch):
                w = jnp.exp(ms[g] - m_all)
                num = num + w * pv_sc[g, rows[g], :]
                den = den + w * den_sc[g, rows[g], :]
            o_ref[rows[0], :] = num / den


def _prompt_attn(q, k, v, bias_rows):
    b, s, _ = q.shape
    n_branch = len(DILATIONS)
    head_pair = pl.BlockSpec((None, s, LANES), lambda i, j: (i, 0, j))
    return pl.pallas_call(
        _prompt_attn_kernel,
        grid=(b, ATT_WIDTH // LANES),
        in_specs=[head_pair, head_pair, head_pair,
                  pl.BlockSpec((n_branch, HEADS_PER_TILE, 2, SPAN), lambda i, j: (0, j, 0, 0))],
        out_specs=head_pair,
        out_shape=jax.ShapeDtypeStruct((b, s, ATT_WIDTH), F32),
        scratch_shapes=[pltpu.VMEM((n_branch, HEADS_PER_TILE * SPAN, 2 * SPAN), F32)]
        + [pltpu.VMEM((s, LANES), F32)] * 3 + [pltpu.VMEM((n_branch, s, LANES), F32)] * 3,
        compiler_params=pltpu.CompilerParams(dimension_semantics=("arbitrary", "arbitrary"),
                                             vmem_limit_bytes=VMEM_LIMIT),
        name="prompt_attn",
    )(q, k, v, bias_rows)


def _sample_attn_stages(q_ref, kn_ref, vn_ref, ckt_ref, cvt_ref, tc_ref, tn_ref, o_ref):
    t = q_ref.shape[0]
    n_buf = ckt_ref.shape[-1]
    n_branch = len(DILATIONS)
    head_of_lane = lax.broadcasted_iota(jnp.int32, (t, ATT_WIDTH), 1) // HEAD_DIM
    q = q_ref[...] * Q_SCALE
    q_bd = jnp.concatenate([jnp.where(head_of_lane == h, q, 0.0) for h in range(N_HEADS)], axis=0).astype(BF16)
    pad = jnp.zeros((SAMPLE_PAD - t, ATT_WIDTH), F32)
    k_new = jnp.concatenate([kn_ref[...], pad], axis=0).astype(BF16)
    v_new = jnp.concatenate([vn_ref[...], pad], axis=0).astype(BF16)

    kt = ckt_ref[...].reshape(ATT_WIDTH, n_buf).astype(BF16)
    half = n_buf // 2
    s_c = jnp.concatenate([_dot(q_bd, kt[:, :half]), _dot(q_bd, kt[:, half:])], axis=1)
    s_n = _dot_nt(q_bd, k_new)
    yield
    m = None
    for g in range(n_branch):
        m_g = jnp.maximum(jnp.max(s_c + tc_ref[g], axis=-1, keepdims=True),
                          jnp.max(s_n + tn_ref[g], axis=-1, keepdims=True))
        m = m_g if m is None else jnp.maximum(m, m_g)
    p_c = jnp.exp(s_c + tc_ref[0] - m)
    p_n = jnp.exp(s_n + tn_ref[0] - m)
    for g in range(1, n_branch):
        p_c = p_c + jnp.exp(s_c + tc_ref[g] - m)
        p_n = p_n + jnp.exp(s_n + tn_ref[g] - m)
    den = jnp.sum(p_c, axis=-1, keepdims=True) + jnp.sum(p_n, axis=-1, keepdims=True)
    yield
    vt = cvt_ref[...].reshape(ATT_WIDTH, n_buf).astype(BF16)
    p_cb = p_c.astype(BF16)
    out = (_dot_nt(p_cb[:, :half], vt[:, :half]) + _dot_nt(p_cb[:, half:], vt[:, half:])
           + _dot(p_n.astype(BF16), v_new)) / den
    att = jnp.zeros((t, ATT_WIDTH), F32)
    for h in range(N_HEADS):
        att = att + jnp.where(head_of_lane == h, out[h * t:(h + 1) * t, :], 0.0)
    o_ref[...] = att


def _sample_attn_specs(n_buf, steps, cached_tables, new_tables):
    tok = pl.BlockSpec((steps, ATT_WIDTH), lambda i: (i, 0))
    cache = pl.BlockSpec((None, N_HEADS, HEAD_DIM, n_buf), lambda i: (i, 0, 0, 0))
    return [tok, tok, tok, cache, cache, _const_spec(cached_tables.shape), _const_spec(new_tables.shape)], tok


def _dense_stages(x_ref, att_ref, rnn_ref, gatt_ref, woa_ref, wor_ref, g2_ref, w1_ref, w2_ref, gf_ref, y_ref):
    att_n = _rms(att_ref[...], gatt_ref[...]).astype(BF16)
    x1 = x_ref[...] + _dot(att_n, woa_ref[...]) + _dot(rnn_ref[...], wor_ref[...])
    n2 = _rms(x1, g2_ref[...]).astype(BF16)
    yield
    mlp = None
    for c in range(D_FF // FF_CHUNK):
        cols = slice(c * FF_CHUNK, (c + 1) * FF_CHUNK)
        hmid = jnp.maximum(_dot(n2, w1_ref[:, cols]), 0.0)
        part = _dot((hmid * hmid).astype(BF16), w2_ref[cols, :])
        mlp = part if mlp is None else mlp + part
        yield
    y_ref[...] = _rms(x1 + mlp, gf_ref[...])


def _interleave(*stage_generators):
    pending = list(stage_generators)
    while pending:
        pending = [g for g in pending if next(g, StopIteration) is not StopIteration]


def _dense_kernel(*refs):
    _interleave(_dense_stages(*refs))


N_DENSE_IN = 10


def _dense_and_sample_attn_kernel(*refs):
    dense_in, attn_in = refs[:N_DENSE_IN], refs[N_DENSE_IN:-2]
    y_ref, att_s_ref = refs[-2:]
    _interleave(_sample_attn_stages(*attn_in, att_s_ref), _dense_stages(*dense_in, y_ref))


def _dense_specs(t):
    tok = lambda width: pl.BlockSpec((t, width), lambda i: (i, 0))
    in_specs = [tok(D_MODEL), tok(ATT_WIDTH), tok(LRU_WIDTH),
                _const_spec((1, ATT_WIDTH)), _const_spec((ATT_WIDTH, D_MODEL)), _const_spec((LRU_WIDTH, D_MODEL)),
                _const_spec((1, D_MODEL)), _const_spec((D_MODEL, D_FF)), _const_spec((D_FF, D_MODEL)),
                _const_spec((1, D_MODEL))]
    assert len(in_specs) == N_DENSE_IN
    return in_specs, tok(D_MODEL)


def _dense(x, att, rnn, weights):
    rows = x.shape[0]
    t = min(TOKEN_TILE, rows)
    in_specs, out_spec = _dense_specs(t)
    return pl.pallas_call(
        _dense_kernel,
        grid=(rows // t,),
        in_specs=in_specs,
        out_specs=out_spec,
        out_shape=jax.ShapeDtypeStruct((rows, D_MODEL), F32),
        compiler_params=pltpu.CompilerParams(dimension_semantics=("arbitrary",), vmem_limit_bytes=VMEM_LIMIT),
        name="dense_tail",
    )(x, att, rnn, *weights)


def _dense_and_sample_attn(x, att, rnn, weights, q_s, k_new, v_new, cache_kt, cache_vt, cached_tables, new_tables,
                           steps):
    rows = x.shape[0]
    n_seq = q_s.shape[0] // steps
    t = rows // n_seq
    assert t * n_seq == rows and t % SUBLANES == 0
    dense_in, dense_out = _dense_specs(t)
    attn_in, attn_out = _sample_attn_specs(cache_kt.shape[-1], steps, cached_tables, new_tables)
    return pl.pallas_call(
        _dense_and_sample_attn_kernel,
        grid=(n_seq,),
        in_specs=dense_in + attn_in,
        out_specs=[dense_out, attn_out],
        out_shape=[jax.ShapeDtypeStruct((rows, D_MODEL), F32), jax.ShapeDtypeStruct(q_s.shape, F32)],
        compiler_params=pltpu.CompilerParams(dimension_semantics=("arbitrary",), vmem_limit_bytes=VMEM_LIMIT),
        name="dense_tail_sample_attn",
    )(x, att, rnn, *weights, q_s, k_new, v_new, cache_kt, cache_vt, cached_tables, new_tables)


def _t5_bucket(dist):
    max_exact = N_BUCKETS // 2
    d_f = jnp.maximum(dist, max_exact).astype(F32)
    large = max_exact + (jnp.log(d_f / max_exact) / math.log(WIN_MAX / max_exact)
                         * (N_BUCKETS - max_exact)).astype(jnp.int32)
    large = jnp.minimum(large, N_BUCKETS - 1)
    return jnp.where(dist < max_exact, dist, large)


def _branch_bias(rel_bias, dil):
    dist = jnp.arange(SPAN + 1, dtype=jnp.int32) * dil
    return rel_bias[_t5_bucket(dist)].astype(F32).T


def _prompt_bias_rows(rel_bias):
    rows = []
    for dil in DILATIONS:
        bias = _branch_bias(rel_bias, dil)
        prev = bias[:, SPAN:0:-1]
        cur = jnp.concatenate([bias[:, 0:1], bias[:, SPAN - 1:0:-1]], axis=1)
        rows.append(jnp.stack([prev, cur], axis=1))
    return jnp.stack(rows)


def _sample_bias_tables(rel_bias, n_buf, steps):
    cached, new = [], []
    pos = np.arange(n_buf)
    new_row = np.arange(SAMPLE_PAD)
    delta_new = np.arange(steps)[:, None] - new_row[None, :]
    for dil in DILATIONS:
        bias = _branch_bias(rel_bias, dil)
        every = [bias[:, SPAN:0:-1]] + [jnp.full((N_HEADS, SPAN), NEG_INF, F32)] * (dil - 1)
        row0 = jnp.concatenate([jnp.full((N_HEADS, n_buf - SPAN * dil), NEG_INF, F32),
                                jnp.stack(every, axis=-1).reshape(N_HEADS, SPAN * dil)], axis=1)
        rows = [jnp.where(pos[None, :] >= t, jnp.roll(row0, t, axis=1), NEG_INF) for t in range(steps)]
        cached.append(jnp.stack(rows, axis=1).reshape(N_HEADS * steps, n_buf))
        valid = (new_row[None, :] < steps) & (delta_new >= 0) & (delta_new % dil == 0)
        j_of = np.where(valid, delta_new // dil, -1)
        one_hot = (j_of[None] == np.arange(steps)[:, None, None]).astype(np.float32)
        vals = jnp.sum(bias[:, :steps, None, None] * one_hot[None], axis=1)
        new.append(jnp.where(valid[None], vals, NEG_INF).reshape(N_HEADS * steps, SAMPLE_PAD))
    return jnp.stack(cached), jnp.stack(new)


def _paired_gate_weights(gate_a_w, gate_x_w):
    def pair_diag(w):
        z = jnp.zeros((LRU_BLOCK, LRU_BLOCK), w.dtype)
        return jnp.stack([jnp.block([[w[2 * p], z], [z, w[2 * p + 1]]]) for p in range(N_LRU_BLOCKS // 2)])
    return jnp.concatenate([pair_diag(gate_a_w), pair_diag(gate_x_w)], axis=-1).astype(BF16)


def kernel(x_prompt, x_sample, cache_k, cache_v, state_conv, state_h, norm1_g, w_in, rel_bias, conv_w, conv_b,
           gate_a_w, gate_a_b, gate_x_w, gate_x_b, lru_lambda, att_out_g, rnn_out_g, w_out, norm2_g, w_mlp_in,
           w_mlp_out, final_g):
    depth = w_in.shape[0]
    assert depth == 1, "the final norm is fused into the single layer's dense kernel"
    b, s, _ = x_prompt.shape
    db, steps, _ = x_sample.shape
    n_buf = cache_k.shape[2]
    assert steps == SUBLANES and s == WIN_MAX and n_buf == WIN_MAX
    l = 0
    row = lambda v: v.reshape(1, -1).astype(F32)

    mix_w = (row(norm1_g[l]), w_in[l].astype(BF16), conv_w[l], row(conv_b[l]),
             _paired_gate_weights(0.5 * gate_a_w[l], 0.5 * gate_x_w[l]), 0.5 * row(gate_a_b[l]), 0.5 * row(gate_x_b[l]),
             row(lru_lambda[l]), row(rnn_out_g[l]))
    dense_w = (row(att_out_g[l]), w_out[l, :ATT_WIDTH].astype(BF16), w_out[l, ATT_WIDTH:].astype(BF16),
               row(norm2_g[l]), w_mlp_in[l].astype(BF16), w_mlp_out[l].astype(BF16), row(final_g))

    xs = x_sample.reshape(db * steps, D_MODEL)
    state_rows = jnp.pad(state_conv[l], ((0, 0), (0, SUBLANES - (CONV_WIDTH - 1)), (0, 0))).reshape(db * steps, LRU_WIDTH)
    qs, ks, vs, rnn_s, conv_s, h_s = _sample_mix(xs, state_rows, state_h[l], mix_w)
    cache_kt = jnp.transpose(cache_k[l], (0, 2, 3, 1))
    cache_vt = jnp.transpose(cache_v[l], (0, 2, 3, 1))

    q, k, v, rnn, conv_p, h_p = _prompt_mix(x_prompt, mix_w)
    att = _prompt_attn(q, k, v, _prompt_bias_rows(rel_bias))
    y_prompt, att_s = _dense_and_sample_attn(
        x_prompt.reshape(b * s, D_MODEL), att.reshape(b * s, ATT_WIDTH), rnn.reshape(b * s, LRU_WIDTH), dense_w,
        qs, ks, vs, cache_kt, cache_vt, *_sample_bias_tables(rel_bias, n_buf, steps), steps)
    y_prompt = y_prompt.reshape(b, s, D_MODEL)
    y_sample = _dense(xs, att_s, rnn_s, dense_w).reshape(db, steps, D_MODEL)

    kv_p = (1, b, s, N_HEADS, HEAD_DIM)
    kv_s = (1, db, steps, N_HEADS, HEAD_DIM)
    return (y_prompt, y_sample,
            k.reshape(kv_p), v.reshape(kv_p), conv_p[None], h_p.reshape(1, b, LRU_WIDTH),
            ks.reshape(kv_s), vs.reshape(kv_s), conv_s[None], h_s[None])
```

```python
import functools
import math

import numpy as np
import jax
import jax.numpy as jnp
from jax import lax
from jax.experimental import pallas as pl
from jax.experimental.pallas import tpu as pltpu

F32 = jnp.float32
BF16 = jnp.bfloat16

D_MODEL = 1024
ATT_WIDTH = 512
LRU_WIDTH = 512
HEAD_DIM = 64
N_HEADS = 8
N_LRU_BLOCKS = 8
LRU_BLOCK = 64
CONV_WIDTH = 4
LRU_C = 8.0
D_FF = 4096
SPAN = 128
DILATIONS = (1, 4, 16)
WIN_MAX = 2048
N_BUCKETS = 32
NORM_EPS = 1e-6
NEG_INF = -1e30
Q_SCALE = HEAD_DIM ** -0.5

SUBLANES = 8
LANES = 128
HEADS_PER_TILE = LANES // HEAD_DIM
VMEM_LIMIT = 56 * 1024 * 1024

TOKEN_TILE = 512
PROMPT_MIX_TILE = 1024
PROMPT_SUB_TILE = 256
FF_CHUNK = 1024
SAMPLE_PAD = 128


def _dot(a, b):
    return jnp.dot(a, b, preferred_element_type=F32)


def _dot_nt(a, b):
    return lax.dot_general(a, b, (((1,), (1,)), ((), ())), preferred_element_type=F32)


def _rms(x, g):
    return x * lax.rsqrt(jnp.mean(x * x, axis=-1, keepdims=True) + NORM_EPS) * g


def _const_spec(shape):
    nd = len(shape)
    return pl.BlockSpec(shape, lambda *_: (0,) * nd, pipeline_mode=pl.Buffered(1))


def _sigmoid_of_twice(half_x):
    return 0.5 * jnp.tanh(half_x) + 0.5


def _lru_gates(xc, wg_ref, ba_ref, bx_ref, lam_ref, a_sc, u_sc):
    xcb = xc.astype(BF16)
    lam = lam_ref[...]
    softplus_neg_lam = jnp.maximum(-lam, 0.0) + jnp.log1p(jnp.exp(-jnp.abs(lam)))
    log_a_per_r = (-LRU_C) * softplus_neg_lam
    for p in range(LRU_WIDTH // LANES):
        cols = slice(p * LANES, (p + 1) * LANES)
        g = _dot(xcb[:, cols], wg_ref[p])
        r = _sigmoid_of_twice(g[:, :LANES] + ba_ref[:, cols])
        gi = _sigmoid_of_twice(g[:, LANES:] + bx_ref[:, cols])
        log_a = r * log_a_per_r[:, cols]
        a = jnp.exp(log_a)
        a_sc[:, cols] = a
        u_sc[:, cols] = jnp.sqrt(-jnp.tanh(log_a) * (a * a + 1.0)) * (gi * xc[:, cols])


def _group_scan(a, u, row):
    for s in (1, 2, 4):
        keep = row >= s
        u = jnp.where(keep, a * pltpu.roll(u, s, 0) + u, u)
        a = jnp.where(keep, a * pltpu.roll(a, s, 0), a)
    return a, u


def _prompt_mix_kernel(x_ref, g1_ref, win_ref, cw_ref, cb_ref, wg_ref, ba_ref, bx_ref, lam_ref, grnn_ref,
                       q_ref, k_ref, v_ref, rnn_ref, conv_ref, h_ref,
                       xr_ext, gr_sc, a_sc, u_sc, hs_sc, h_carry):
    t = x_ref.shape[0]
    sub = PROMPT_SUB_TILE
    n_sub = t // sub

    @pl.when(pl.program_id(1) == 0)
    def _():
        xr_ext[0:SUBLANES, :] = jnp.zeros((SUBLANES, LRU_WIDTH), F32)
        h_carry[...] = jnp.zeros_like(h_carry)

    row = lax.broadcasted_iota(jnp.int32, (SUBLANES, LRU_WIDTH), 0)

    def project(s):
        rows = slice(s * sub, (s + 1) * sub)
        n = _rms(x_ref[rows, :], g1_ref[...]).astype(BF16)
        q_ref[rows, :] = _dot(n, win_ref[:, 0:ATT_WIDTH])
        k_ref[rows, :] = _dot(n, win_ref[:, ATT_WIDTH:2 * ATT_WIDTH])
        v_ref[rows, :] = _dot(n, win_ref[:, 2 * ATT_WIDTH:3 * ATT_WIDTH])
        xr_ext[SUBLANES + s * sub:SUBLANES + (s + 1) * sub, :] = _dot(
            n, win_ref[:, 3 * ATT_WIDTH:3 * ATT_WIDTH + LRU_WIDTH])
        gr_sc[rows, :] = _dot(n, win_ref[:, 3 * ATT_WIDTH + LRU_WIDTH:])

    def recur(s, h):
        rows = slice(s * sub, (s + 1) * sub)
        xc = cb_ref[...]
        for back in range(CONV_WIDTH):
            xc = xc + (cw_ref[CONV_WIDTH - 1 - back:CONV_WIDTH - back, :]
                       * xr_ext[pl.ds(SUBLANES + s * sub - back, sub), :])
        _lru_gates(xc, wg_ref, ba_ref, bx_ref, lam_ref, a_sc.at[rows], u_sc.at[rows])
        for g in range(sub // SUBLANES):
            grp = slice(s * sub + g * SUBLANES, s * sub + (g + 1) * SUBLANES)
            a, u = _group_scan(a_sc[grp, :], u_sc[grp, :], row)
            hs = a * h + u
            hs_sc[grp, :] = hs
            h = jnp.broadcast_to(hs[SUBLANES - 1:SUBLANES, :], hs.shape)
        rnn_ref[rows, :] = _rms(hs_sc[rows, :] * jax.nn.gelu(gr_sc[rows, :]), grnn_ref[...]).astype(BF16)
        return h

    h = h_carry[...]
    project(0)
    for s in range(1, n_sub):
        project(s)
        h = recur(s - 1, h)
    h = recur(n_sub - 1, h)
    h_carry[...] = h
    h_ref[...] = h[0:1, :]
    conv_ref[...] = xr_ext[SUBLANES + t - (CONV_WIDTH - 1):SUBLANES + t, :]
    xr_ext[0:SUBLANES, :] = xr_ext[t:t + SUBLANES, :]


def _sample_mix_kernel(x_ref, st_ref, h0_ref, g1_ref, win_ref, cw_ref, cb_ref, wg_ref, ba_ref, bx_ref, lam_ref,
                       grnn_ref, q_ref, k_ref, v_ref, rnn_ref, conv_ref, h_ref,
                       xr_sc, xc_sc, a_sc, u_sc, hs_sc):
    t = x_ref.shape[0]
    n_seq = t // SUBLANES
    n = _rms(x_ref[...], g1_ref[...]).astype(BF16)
    q_ref[...] = _dot(n, win_ref[:, 0:ATT_WIDTH])
    k_ref[...] = _dot(n, win_ref[:, ATT_WIDTH:2 * ATT_WIDTH])
    v_ref[...] = _dot(n, win_ref[:, 2 * ATT_WIDTH:3 * ATT_WIDTH])
    xr_sc[...] = _dot(n, win_ref[:, 3 * ATT_WIDTH:3 * ATT_WIDTH + LRU_WIDTH])

    row = lax.broadcasted_iota(jnp.int32, (SUBLANES, LRU_WIDTH), 0)
    n_state = CONV_WIDTH - 1

    def conv_group(b, carry):
        rows = pl.ds(pl.multiple_of(b * SUBLANES, SUBLANES), SUBLANES)
        xg = xr_sc[rows, :]
        sg = st_ref[rows, :]
        xc = cb_ref[...] + cw_ref[CONV_WIDTH - 1:CONV_WIDTH, :] * xg
        for back in range(1, CONV_WIDTH):
            prev = jnp.where(row >= back, pltpu.roll(xg, back, 0),
                             pltpu.roll(sg, (back - n_state) % SUBLANES, 0))
            xc = xc + cw_ref[CONV_WIDTH - 1 - back:CONV_WIDTH - back, :] * prev
        xc_sc[rows, :] = xc
        conv_ref[b] = xg[SUBLANES - n_state:, :]
        return carry

    lax.fori_loop(0, n_seq, conv_group, 0, unroll=8)

    _lru_gates(xc_sc[...], wg_ref, ba_ref, bx_ref, lam_ref, a_sc, u_sc)

    def scan_group(b, carry):
        rows = pl.ds(pl.multiple_of(b * SUBLANES, SUBLANES), SUBLANES)
        a, u = _group_scan(a_sc[rows, :], u_sc[rows, :], row)
        hs = a * h0_ref[pl.ds(b, 1), :] + u
        hs_sc[rows, :] = hs
        h_ref[pl.ds(b, 1), :] = hs[SUBLANES - 1:SUBLANES, :]
        return carry

    lax.fori_loop(0, n_seq, scan_group, 0, unroll=8)

    gr = _dot(n, win_ref[:, 3 * ATT_WIDTH + LRU_WIDTH:])
    rnn_ref[...] = _rms(hs_sc[...] * jax.nn.gelu(gr), grnn_ref[...]).astype(BF16)


def _mix_weight_specs():
    return [
        _const_spec((1, D_MODEL)),
        _const_spec((D_MODEL, 3 * ATT_WIDTH + 2 * LRU_WIDTH)),
        _const_spec((CONV_WIDTH, LRU_WIDTH)),
        _const_spec((1, LRU_WIDTH)),
        _const_spec((LRU_WIDTH // LANES, LANES, 2 * LANES)),
        _const_spec((1, LRU_WIDTH)),
        _const_spec((1, LRU_WIDTH)),
        _const_spec((1, LRU_WIDTH)),
        _const_spec((1, LRU_WIDTH)),
    ]


def _prompt_mix(x, weights):
    b, s, _ = x.shape
    t = PROMPT_MIX_TILE
    assert s % t == 0 and t % PROMPT_SUB_TILE == 0
    tok = lambda width: pl.BlockSpec((None, t, width), lambda i, j: (i, j, 0))
    per_seq = lambda rows: pl.BlockSpec((None, rows, LRU_WIDTH), lambda i, j: (i, 0, 0))
    return pl.pallas_call(
        _prompt_mix_kernel,
        grid=(b, s // t),
        in_specs=[tok(D_MODEL)] + _mix_weight_specs(),
        out_specs=[tok(ATT_WIDTH), tok(ATT_WIDTH), tok(ATT_WIDTH), tok(LRU_WIDTH),
                   per_seq(CONV_WIDTH - 1), per_seq(1)],
        out_shape=[jax.ShapeDtypeStruct((b, s, ATT_WIDTH), F32)] * 3
        + [jax.ShapeDtypeStruct((b, s, LRU_WIDTH), BF16),
           jax.ShapeDtypeStruct((b, CONV_WIDTH - 1, LRU_WIDTH), F32),
           jax.ShapeDtypeStruct((b, 1, LRU_WIDTH), F32)],
        scratch_shapes=[pltpu.VMEM((t + SUBLANES, LRU_WIDTH), F32)] + [pltpu.VMEM((t, LRU_WIDTH), F32)] * 4
        + [pltpu.VMEM((SUBLANES, LRU_WIDTH), F32)],
        compiler_params=pltpu.CompilerParams(dimension_semantics=("arbitrary", "arbitrary"),
                                             vmem_limit_bytes=VMEM_LIMIT),
        name="prompt_mix",
    )(x, *weights)


def _sample_mix(x, state_rows, h0, weights):
    rows = x.shape[0]
    t = min(TOKEN_TILE, rows)
    n_seq = t // SUBLANES
    tok = lambda width: pl.BlockSpec((t, width), lambda i: (i, 0))
    return pl.pallas_call(
        _sample_mix_kernel,
        grid=(rows // t,),
        in_specs=[tok(D_MODEL), tok(LRU_WIDTH), pl.BlockSpec((n_seq, LRU_WIDTH), lambda i: (i, 0))]
        + _mix_weight_specs(),
        out_specs=[tok(ATT_WIDTH), tok(ATT_WIDTH), tok(ATT_WIDTH), tok(LRU_WIDTH),
                   pl.BlockSpec((n_seq, CONV_WIDTH - 1, LRU_WIDTH), lambda i: (i, 0, 0)),
                   pl.BlockSpec((n_seq, LRU_WIDTH), lambda i: (i, 0))],
        out_shape=[jax.ShapeDtypeStruct((rows, ATT_WIDTH), F32)] * 3
        + [jax.ShapeDtypeStruct((rows, LRU_WIDTH), BF16),
           jax.ShapeDtypeStruct((rows // SUBLANES, CONV_WIDTH - 1, LRU_WIDTH), F32),
           jax.ShapeDtypeStruct((rows // SUBLANES, LRU_WIDTH), F32)],
        scratch_shapes=[pltpu.VMEM((t, LRU_WIDTH), F32)] * 5,
        compiler_params=pltpu.CompilerParams(dimension_semantics=("arbitrary",), vmem_limit_bytes=VMEM_LIMIT),
        name="sample_mix",
    )(x, state_rows, h0, *weights)


def _prompt_attn_kernel(q_ref, k_ref, v_ref, brow_ref, o_ref, tab_sc, qp_sc, kp_sc, vp_sc, pv_sc, m_sc, den_sc):
    s_len = q_ref.shape[0]
    n_branch = len(DILATIONS)
    qi = lax.broadcasted_iota(jnp.int32, (SPAN, SPAN), 0)
    ki = lax.broadcasted_iota(jnp.int32, (SPAN, SPAN), 1)
    low_head = ki < HEAD_DIM

    for g in range(n_branch):
        for h in range(HEADS_PER_TILE):
            rows = slice(h * SPAN, (h + 1) * SPAN)
            prev = pltpu.roll(jnp.broadcast_to(brow_ref[g, h, 0:1, :], (SPAN, SPAN)), 0, 1, stride=1, stride_axis=0)
            cur = pltpu.roll(jnp.broadcast_to(brow_ref[g, h, 1:2, :], (SPAN, SPAN)), 0, 1, stride=1, stride_axis=0)
            tab_sc[g, rows, 0:SPAN] = jnp.where(ki >= qi, prev, NEG_INF)
            tab_sc[g, rows, SPAN:] = jnp.where(ki <= qi, cur, NEG_INF)

    perm = DILATIONS[1]
    sub_len = s_len // perm
    assert DILATIONS == (1, perm, perm * perm) and sub_len % (2 * SPAN) == 0
    natural = (q_ref, k_ref, v_ref)
    permuted = (qp_sc, kp_sc, vp_sc)
    for src, dst in zip(natural, permuted):
        for c in range(perm):
            for i0 in range(0, sub_len, 2 * SPAN):
                dst[c * sub_len + i0:c * sub_len + i0 + 2 * SPAN, :] = src[pl.ds(c + perm * i0, 2 * SPAN, stride=perm), :]

    def attend(branch, refs, q_rows, k_rows, first):
        qr, kr, vr = refs
        n_keys = SPAN if first else 2 * SPAN
        q = qr[q_rows, :] * Q_SCALE
        q2 = jnp.concatenate([jnp.where(low_head, q, 0.0), jnp.where(low_head, 0.0, q)], axis=0).astype(BF16)
        kk = kr[k_rows, :].astype(BF16)
        vv = vr[k_rows, :].astype(BF16)
        table = tab_sc[branch, :, SPAN:] if first else tab_sc[branch]
        logits = _dot_nt(q2, kk) + table
        m = jnp.max(logits, axis=-1, keepdims=True)
        p32 = jnp.exp(logits - m)
        den_b = jnp.broadcast_to(jnp.sum(p32, axis=-1, keepdims=True), (2 * SPAN, LANES))
        r = _dot(p32.astype(BF16), vv)
        m_b = jnp.broadcast_to(m, (2 * SPAN, LANES))
        pv_sc[branch, q_rows, :] = jnp.where(low_head, r[:SPAN], r[SPAN:])
        den_sc[branch, q_rows, :] = jnp.where(low_head, den_b[:SPAN], den_b[SPAN:])
        m_sc[branch, q_rows, :] = jnp.where(low_head, m_b[:SPAN], m_b[SPAN:])

    for i in range(s_len // SPAN):
        if i == 0:
            attend(0, natural, pl.ds(0, SPAN), pl.ds(0, SPAN), True)
        else:
            attend(0, natural, pl.ds(i * SPAN, SPAN), pl.ds((i - 1) * SPAN, 2 * SPAN), False)
    for c in range(perm):
        for i in range(sub_len // SPAN):
            q0 = c * sub_len + i * SPAN
            if i == 0:
                attend(1, permuted, pl.ds(q0, SPAN), pl.ds(q0, SPAN), True)
            else:
                attend(1, permuted, pl.ds(q0, SPAN), pl.ds(q0 - SPAN, 2 * SPAN), False)
    assert sub_len // perm == SPAN
    for c in range(perm):
        for c_hi in range(perm):
            rows = pl.ds(c * sub_len + c_hi, SPAN, stride=perm)
            attend(2, permuted, rows, rows, True)

    chunk = 2 * SPAN
    for c in range(perm):
        for i0 in range(0, sub_len, chunk):
            rows = (pl.ds(c + perm * i0, chunk, stride=perm),) + (pl.ds(c * sub_len + i0, chunk),) * (n_branch - 1)
            ms = [m_sc[g, rows[g], :] for g in range(n_branch)]
            m_all = jnp.maximum(jnp.maximum(ms[0], ms[1]), ms[2])
            num = jnp.zeros((chunk, LANES), F32)
            den = jnp.zeros((chunk, LANES), F32)
            for g in range(n_branch):
                w = jnp.exp(ms[g] - m_all)
                num = num + w * pv_sc[g, rows[g], :]
                den = den + w * den_sc[g, rows[g], :]
            o_ref[rows[0], :] = num / den


def _prompt_attn(q, k, v, bias_rows):
    b, s, _ = q.shape
    n_branch = len(DILATIONS)
    head_pair = pl.BlockSpec((None, s, LANES), lambda i, j: (i, 0, j))
    return pl.pallas_call(
        _prompt_attn_kernel,
        grid=(b, ATT_WIDTH // LANES),
        in_specs=[head_pair, head_pair, head_pair,
                  pl.BlockSpec((n_branch, HEADS_PER_TILE, 2, SPAN), lambda i, j: (0, j, 0, 0))],
        out_specs=head_pair,
        out_shape=jax.ShapeDtypeStruct((b, s, ATT_WIDTH), F32),
        scratch_shapes=[pltpu.VMEM((n_branch, HEADS_PER_TILE * SPAN, 2 * SPAN), F32)]
        + [pltpu.VMEM((s, LANES), F32)] * 3 + [pltpu.VMEM((n_branch, s, LANES), F32)] * 3,
        compiler_params=pltpu.CompilerParams(dimension_semantics=("arbitrary", "arbitrary"),
                                             vmem_limit_bytes=VMEM_LIMIT),
        name="prompt_attn",
    )(q, k, v, bias_rows)


def _sample_attn_stages(q_ref, kn_ref, vn_ref, ckt_ref, cvt_ref, tc_ref, tn_ref, o_ref):
    t = q_ref.shape[0]
    n_buf = ckt_ref.shape[-1]
    n_branch = len(DILATIONS)
    head_of_lane = lax.broadcasted_iota(jnp.int32, (t, ATT_WIDTH), 1) // HEAD_DIM
    q = q_ref[...] * Q_SCALE
    q_bd = jnp.concatenate([jnp.where(head_of_lane == h, q, 0.0) for h in range(N_HEADS)], axis=0).astype(BF16)
    pad = jnp.zeros((SAMPLE_PAD - t, ATT_WIDTH), F32)
    k_new = jnp.concatenate([kn_ref[...], pad], axis=0).astype(BF16)
    v_new = jnp.concatenate([vn_ref[...], pad], axis=0).astype(BF16)

    kt = ckt_ref[...].reshape(ATT_WIDTH, n_buf).astype(BF16)
    half = n_buf // 2
    s_c = jnp.concatenate([_dot(q_bd, kt[:, :half]), _dot(q_bd, kt[:, half:])], axis=1)
    s_n = _dot_nt(q_bd, k_new)
    yield
    m = None
    for g in range(n_branch):
        m_g = jnp.maximum(jnp.max(s_c + tc_ref[g], axis=-1, keepdims=True),
                          jnp.max(s_n + tn_ref[g], axis=-1, keepdims=True))
        m = m_g if m is None else jnp.maximum(m, m_g)
    p_c = jnp.exp(s_c + tc_ref[0] - m)
    p_n = jnp.exp(s_n + tn_ref[0] - m)
    for g in range(1, n_branch):
        p_c = p_c + jnp.exp(s_c + tc_ref[g] - m)
        p_n = p_n + jnp.exp(s_n + tn_ref[g] - m)
    den = jnp.sum(p_c, axis=-1, keepdims=True) + jnp.sum(p_n, axis=-1, keepdims=True)
    yield
    vt = cvt_ref[...].reshape(ATT_WIDTH, n_buf).astype(BF16)
    p_cb = p_c.astype(BF16)
    out = (_dot_nt(p_cb[:, :half], vt[:, :half]) + _dot_nt(p_cb[:, half:], vt[:, half:])
           + _dot(p_n.astype(BF16), v_new)) / den
    att = jnp.zeros((t, ATT_WIDTH), F32)
    for h in range(N_HEADS):
        att = att + jnp.where(head_of_lane == h, out[h * t:(h + 1) * t, :], 0.0)
    o_ref[...] = att


def _sample_attn_specs(n_buf, steps, cached_tables, new_tables):
    tok = pl.BlockSpec((steps, ATT_WIDTH), lambda i: (i, 0))
    cache = pl.BlockSpec((None, N_HEADS, HEAD_DIM, n_buf), lambda i: (i, 0, 0, 0))
    return [tok, tok, tok, cache, cache, _const_spec(cached_tables.shape), _const_spec(new_tables.shape)], tok


def _dense_stages(x_ref, att_ref, rnn_ref, gatt_ref, woa_ref, wor_ref, g2_ref, w1_ref, w2_ref, gf_ref, y_ref):
    att_n = _rms(att_ref[...], gatt_ref[...]).astype(BF16)
    x1 = x_ref[...] + _dot(att_n, woa_ref[...]) + _dot(rnn_ref[...], wor_ref[...])
    n2 = _rms(x1, g2_ref[...]).astype(BF16)
    yield
    mlp = None
    for c in range(D_FF // FF_CHUNK):
        cols = slice(c * FF_CHUNK, (c + 1) * FF_CHUNK)
        hmid = jnp.maximum(_dot(n2, w1_ref[:, cols]), 0.0)
        part = _dot((hmid * hmid).astype(BF16), w2_ref[cols, :])
        mlp = part if mlp is None else mlp + part
        yield
    y_ref[...] = _rms(x1 + mlp, gf_ref[...])


def _interleave(*stage_generators):
    pending = list(stage_generators)
    while pending:
        pending = [g for g in pending if next(g, StopIteration) is not StopIteration]


def _dense_kernel(*refs):
    _interleave(_dense_stages(*refs))


N_DENSE_IN = 10


def _dense_and_sample_attn_kernel(*refs):
    dense_in, attn_in = refs[:N_DENSE_IN], refs[N_DENSE_IN:-2]
    y_ref, att_s_ref = refs[-2:]
    _interleave(_sample_attn_stages(*attn_in, att_s_ref), _dense_stages(*dense_in, y_ref))


def _dense_specs(t):
    tok = lambda width: pl.BlockSpec((t, width), lambda i: (i, 0))
    in_specs = [tok(D_MODEL), tok(ATT_WIDTH), tok(LRU_WIDTH),
                _const_spec((1, ATT_WIDTH)), _const_spec((ATT_WIDTH, D_MODEL)), _const_spec((LRU_WIDTH, D_MODEL)),
                _const_spec((1, D_MODEL)), _const_spec((D_MODEL, D_FF)), _const_spec((D_FF, D_MODEL)),
                _const_spec((1, D_MODEL))]
    assert len(in_specs) == N_DENSE_IN
    return in_specs, tok(D_MODEL)


def _dense(x, att, rnn, weights):
    rows = x.shape[0]
    t = min(TOKEN_TILE, rows)
    in_specs, out_spec = _dense_specs(t)
    return pl.pallas_call(
        _dense_kernel,
        grid=(rows // t,),
        in_specs=in_specs,
        out_specs=out_spec,
        out_shape=jax.ShapeDtypeStruct((rows, D_MODEL), F32),
        compiler_params=pltpu.CompilerParams(dimension_semantics=("arbitrary",), vmem_limit_bytes=VMEM_LIMIT),
        name="dense_tail",
    )(x, att, rnn, *weights)


def _dense_and_sample_attn(x, att, rnn, weights, q_s, k_new, v_new, cache_kt, cache_vt, cached_tables, new_tables,
                           steps):
    rows = x.shape[0]
    n_seq = q_s.shape[0] // steps
    t = rows // n_seq
    assert t * n_seq == rows and t % SUBLANES == 0
    dense_in, dense_out = _dense_specs(t)
    attn_in, attn_out = _sample_attn_specs(cache_kt.shape[-1], steps, cached_tables, new_tables)
    return pl.pallas_call(
        _dense_and_sample_attn_kernel,
        grid=(n_seq,),
        in_specs=dense_in + attn_in,
        out_specs=[dense_out, attn_out],
        out_shape=[jax.ShapeDtypeStruct((rows, D_MODEL), F32), jax.ShapeDtypeStruct(q_s.shape, F32)],
        compiler_params=pltpu.CompilerParams(dimension_semantics=("arbitrary",), vmem_limit_bytes=VMEM_LIMIT),
        name="dense_tail_sample_attn",
    )(x, att, rnn, *weights, q_s, k_new, v_new, cache_kt, cache_vt, cached_tables, new_tables)


def _t5_bucket(dist):
    max_exact = N_BUCKETS // 2
    d_f = jnp.maximum(dist, max_exact).astype(F32)
    large = max_exact + (jnp.log(d_f / max_exact) / math.log(WIN_MAX / max_exact)
                         * (N_BUCKETS - max_exact)).astype(jnp.int32)
    large = jnp.minimum(large, N_BUCKETS - 1)
    return jnp.where(dist < max_exact, dist, large)


def _branch_bias(rel_bias, dil):
    dist = jnp.arange(SPAN + 1, dtype=jnp.int32) * dil
    return rel_bias[_t5_bucket(dist)].astype(F32).T


def _prompt_bias_rows(rel_bias):
    rows = []
    for dil in DILATIONS:
        bias = _branch_bias(rel_bias, dil)
        prev = bias[:, SPAN:0:-1]
        cur = jnp.concatenate([bias[:, 0:1], bias[:, SPAN - 1:0:-1]], axis=1)
        rows.append(jnp.stack([prev, cur], axis=1))
    return jnp.stack(rows)


def _sample_bias_tables(rel_bias, n_buf, steps):
    cached, new = [], []
    pos = np.arange(n_buf)
    new_row = np.arange(SAMPLE_PAD)
    delta_new = np.arange(steps)[:, None] - new_row[None, :]
    for dil in DILATIONS:
        bias = _branch_bias(rel_bias, dil)
        every = [bias[:, SPAN:0:-1]] + [jnp.full((N_HEADS, SPAN), NEG_INF, F32)] * (dil - 1)
        row0 = jnp.concatenate([jnp.full((N_HEADS, n_buf - SPAN * dil), NEG_INF, F32),
                                jnp.stack(every, axis=-1).reshape(N_HEADS, SPAN * dil)], axis=1)
        rows = [jnp.where(pos[None, :] >= t, jnp.roll(row0, t, axis=1), NEG_INF) for t in range(steps)]
        cached.append(jnp.stack(rows, axis=1).reshape(N_HEADS * steps, n_buf))
        valid = (new_row[None, :] < steps) & (delta_new >= 0) & (delta_new % dil == 0)
        j_of = np.where(valid, delta_new // dil, -1)
        one_hot = (j_of[None] == np.arange(steps)[:, None, None]).astype(np.float32)
        vals = jnp.sum(bias[:, :steps, None, None] * one_hot[None], axis=1)
        new.append(jnp.where(valid[None], vals, NEG_INF).reshape(N_HEADS * steps, SAMPLE_PAD))
    return jnp.stack(cached), jnp.stack(new)


def _paired_gate_weights(gate_a_w, gate_x_w):
    def pair_diag(w):
        z = jnp.zeros((LRU_BLOCK, LRU_BLOCK), w.dtype)
        return jnp.stack([jnp.block([[w[2 * p], z], [z, w[2 * p + 1]]]) for p in range(N_LRU_BLOCKS // 2)])
    return jnp.concatenate([pair_diag(gate_a_w), pair_diag(gate_x_w)], axis=-1).astype(BF16)


def kernel(x_prompt, x_sample, cache_k, cache_v, state_conv, state_h, norm1_g, w_in, rel_bias, conv_w, conv_b,
           gate_a_w, gate_a_b, gate_x_w, gate_x_b, lru_lambda, att_out_g, rnn_out_g, w_out, norm2_g, w_mlp_in,
           w_mlp_out, final_g):
    depth = w_in.shape[0]
    assert depth == 1, "the final norm is fused into the single layer's dense kernel"
    b, s, _ = x_prompt.shape
    db, steps, _ = x_sample.shape
    n_buf = cache_k.shape[2]
    assert steps == SUBLANES and s == WIN_MAX and n_buf == WIN_MAX
    l = 0
    row = lambda v: v.reshape(1, -1).astype(F32)

    mix_w = (row(norm1_g[l]), w_in[l].astype(BF16), conv_w[l], row(conv_b[l]),
             _paired_gate_weights(0.5 * gate_a_w[l], 0.5 * gate_x_w[l]), 0.5 * row(gate_a_b[l]), 0.5 * row(gate_x_b[l]),
             row(lru_lambda[l]), row(rnn_out_g[l]))
    dense_w = (row(att_out_g[l]), w_out[l, :ATT_WIDTH].astype(BF16), w_out[l, ATT_WIDTH:].astype(BF16),
               row(norm2_g[l]), w_mlp_in[l].astype(BF16), w_mlp_out[l].astype(BF16), row(final_g))

    xs = x_sample.reshape(db * steps, D_MODEL)
    state_rows = jnp.pad(state_conv[l], ((0, 0), (0, SUBLANES - (CONV_WIDTH - 1)), (0, 0))).reshape(db * steps, LRU_WIDTH)
    qs, ks, vs, rnn_s, conv_s, h_s = _sample_mix(xs, state_rows, state_h[l], mix_w)
    cache_kt = jnp.transpose(cache_k[l], (0, 2, 3, 1))
    cache_vt = jnp.transpose(cache_v[l], (0, 2, 3, 1))

    q, k, v, rnn, conv_p, h_p = _prompt_mix(x_prompt, mix_w)
    att = _prompt_attn(q, k, v, _prompt_bias_rows(rel_bias))
    y_prompt, att_s = _dense_and_sample_attn(
        x_prompt.reshape(b * s, D_MODEL), att.reshape(b * s, ATT_WIDTH), rnn.reshape(b * s, LRU_WIDTH), dense_w,
        qs, ks, vs, cache_kt, cache_vt, *_sample_bias_tables(rel_bias, n_buf, steps), steps)
    y_prompt = y_prompt.reshape(b, s, D_MODEL)
    y_sample = _dense(xs, att_s, rnn_s, dense_w).reshape(db, steps, D_MODEL)

    kv_p = (1, b, s, N_HEADS, HEAD_DIM)
    kv_s = (1, db, steps, N_HEADS, HEAD_DIM)
    return (y_prompt, y_sample,
            k.reshape(kv_p), v.reshape(kv_p), conv_p[None], h_p.reshape(1, b, LRU_WIDTH),
            ks.reshape(kv_s), vs.reshape(kv_s), conv_s[None], h_s[None])
```

```python
import functools
import math

import numpy as np
import jax
import jax.numpy as jnp
from jax import lax
from jax.experimental import pallas as pl
from jax.experimental.pallas import tpu as pltpu

F32 = jnp.float32
BF16 = jnp.bfloat16

D_MODEL = 1024
ATT_WIDTH = 512
LRU_WIDTH = 512
HEAD_DIM = 64
N_HEADS = 8
N_LRU_BLOCKS = 8
LRU_BLOCK = 64
CONV_WIDTH = 4
LRU_C = 8.0
D_FF = 4096
SPAN = 128
DILATIONS = (1, 4, 16)
WIN_MAX = 2048
N_BUCKETS = 32
NORM_EPS = 1e-6
NEG_INF = -1e30
Q_SCALE = HEAD_DIM ** -0.5
LOG2_E = math.log2(math.e)

SUBLANES = 8
LANES = 128
HEADS_PER_TILE = LANES // HEAD_DIM
VMEM_LIMIT = 56 * 1024 * 1024

TOKEN_TILE = 512
PROMPT_MIX_TILE = 1024
PROMPT_SUB_TILE = 256
FF_CHUNK = 1024
SAMPLE_PAD = 128


def _dot(a, b):
    return jnp.dot(a, b, preferred_element_type=F32)


def _dot_nt(a, b):
    return lax.dot_general(a, b, (((1,), (1,)), ((), ())), preferred_element_type=F32)


def _rms(x, g):
    return x * lax.rsqrt(jnp.mean(x * x, axis=-1, keepdims=True) + NORM_EPS) * g


def _const_spec(shape):
    nd = len(shape)
    return pl.BlockSpec(shape, lambda *_: (0,) * nd, pipeline_mode=pl.Buffered(1))


def _sigmoid_of_twice(half_x):
    return 0.5 * jnp.tanh(half_x) + 0.5


def _lru_gates(xc, wg_ref, ba_ref, bx_ref, lam_ref, a_sc, u_sc):
    xcb = xc.astype(BF16)
    lam = lam_ref[...]
    softplus_neg_lam = jnp.maximum(-lam, 0.0) + jnp.log1p(jnp.exp(-jnp.abs(lam)))
    log_a_per_r = (-LRU_C) * softplus_neg_lam
    for p in range(LRU_WIDTH // LANES):
        cols = slice(p * LANES, (p + 1) * LANES)
        g = _dot(xcb[:, cols], wg_ref[p])
        r = _sigmoid_of_twice(g[:, :LANES] + ba_ref[:, cols])
        gi = _sigmoid_of_twice(g[:, LANES:] + bx_ref[:, cols])
        log_a = r * log_a_per_r[:, cols]
        a = jnp.exp(log_a)
        a_sc[:, cols] = a
        u_sc[:, cols] = jnp.sqrt(-jnp.tanh(log_a) * (a * a + 1.0)) * (gi * xc[:, cols])


def _group_scan(a, u, row):
    for s in (1, 2, 4):
        keep = row >= s
        u = jnp.where(keep, a * pltpu.roll(u, s, 0) + u, u)
        a = jnp.where(keep, a * pltpu.roll(a, s, 0), a)
    return a, u


def _prompt_mix_kernel(x_ref, g1_ref, win_ref, cw_ref, cb_ref, wg_ref, ba_ref, bx_ref, lam_ref, grnn_ref,
                       q_ref, k_ref, v_ref, rnn_ref, conv_ref, h_ref,
                       xr_ext, gr_sc, a_sc, u_sc, hs_sc, h_carry):
    t = x_ref.shape[0]
    sub = PROMPT_SUB_TILE
    n_sub = t // sub

    @pl.when(pl.program_id(1) == 0)
    def _():
        xr_ext[0:SUBLANES, :] = jnp.zeros((SUBLANES, LRU_WIDTH), F32)
        h_carry[...] = jnp.zeros_like(h_carry)

    row = lax.broadcasted_iota(jnp.int32, (SUBLANES, LRU_WIDTH), 0)

    def project(s):
        rows = slice(s * sub, (s + 1) * sub)
        n = _rms(x_ref[rows, :], g1_ref[...]).astype(BF16)
        q_ref[rows, :] = _dot(n, win_ref[:, 0:ATT_WIDTH])
        k_ref[rows, :] = _dot(n, win_ref[:, ATT_WIDTH:2 * ATT_WIDTH])
        v_ref[rows, :] = _dot(n, win_ref[:, 2 * ATT_WIDTH:3 * ATT_WIDTH])
        xr_ext[SUBLANES + s * sub:SUBLANES + (s + 1) * sub, :] = _dot(
            n, win_ref[:, 3 * ATT_WIDTH:3 * ATT_WIDTH + LRU_WIDTH])
        gr_sc[rows, :] = _dot(n, win_ref[:, 3 * ATT_WIDTH + LRU_WIDTH:])

    def recur(s, h):
        rows = slice(s * sub, (s + 1) * sub)
        xc = cb_ref[...]
        for back in range(CONV_WIDTH):
            xc = xc + (cw_ref[CONV_WIDTH - 1 - back:CONV_WIDTH - back, :]
                       * xr_ext[pl.ds(SUBLANES + s * sub - back, sub), :])
        _lru_gates(xc, wg_ref, ba_ref, bx_ref, lam_ref, a_sc.at[rows], u_sc.at[rows])
        for g in range(sub // SUBLANES):
            grp = slice(s * sub + g * SUBLANES, s * sub + (g + 1) * SUBLANES)
            a, u = _group_scan(a_sc[grp, :], u_sc[grp, :], row)
            hs = a * h + u
            hs_sc[grp, :] = hs
            h = jnp.broadcast_to(hs[SUBLANES - 1:SUBLANES, :], hs.shape)
        rnn_ref[rows, :] = _rms(hs_sc[rows, :] * jax.nn.gelu(gr_sc[rows, :]), grnn_ref[...]).astype(BF16)
        return h

    h = h_carry[...]
    project(0)
    for s in range(1, n_sub):
        project(s)
        h = recur(s - 1, h)
    h = recur(n_sub - 1, h)
    h_carry[...] = h
    h_ref[...] = h[0:1, :]
    conv_ref[...] = xr_ext[SUBLANES + t - (CONV_WIDTH - 1):SUBLANES + t, :]
    xr_ext[0:SUBLANES, :] = xr_ext[t:t + SUBLANES, :]


def _sample_mix_kernel(x_ref, st_ref, h0_ref, g1_ref, win_ref, cw_ref, cb_ref, wg_ref, ba_ref, bx_ref, lam_ref,
                       grnn_ref, q_ref, k_ref, v_ref, rnn_ref, conv_ref, h_ref,
                       xr_sc, xc_sc, a_sc, u_sc, hs_sc):
    t = x_ref.shape[0]
    n_seq = t // SUBLANES
    n = _rms(x_ref[...], g1_ref[...]).astype(BF16)
    q_ref[...] = _dot(n, win_ref[:, 0:ATT_WIDTH])
    k_ref[...] = _dot(n, win_ref[:, ATT_WIDTH:2 * ATT_WIDTH])
    v_ref[...] = _dot(n, win_ref[:, 2 * ATT_WIDTH:3 * ATT_WIDTH])
    xr_sc[...] = _dot(n, win_ref[:, 3 * ATT_WIDTH:3 * ATT_WIDTH + LRU_WIDTH])

    row = lax.broadcasted_iota(jnp.int32, (SUBLANES, LRU_WIDTH), 0)
    n_state = CONV_WIDTH - 1

    def conv_group(b, carry):
        rows = pl.ds(pl.multiple_of(b * SUBLANES, SUBLANES), SUBLANES)
        xg = xr_sc[rows, :]
        sg = st_ref[rows, :]
        xc = cb_ref[...] + cw_ref[CONV_WIDTH - 1:CONV_WIDTH, :] * xg
        for back in range(1, CONV_WIDTH):
            prev = jnp.where(row >= back, pltpu.roll(xg, back, 0),
                             pltpu.roll(sg, (back - n_state) % SUBLANES, 0))
            xc = xc + cw_ref[CONV_WIDTH - 1 - back:CONV_WIDTH - back, :] * prev
        xc_sc[rows, :] = xc
        conv_ref[b] = xg[SUBLANES - n_state:, :]
        return carry

    lax.fori_loop(0, n_seq, conv_group, 0, unroll=8)

    _lru_gates(xc_sc[...], wg_ref, ba_ref, bx_ref, lam_ref, a_sc, u_sc)

    def scan_group(b, carry):
        rows = pl.ds(pl.multiple_of(b * SUBLANES, SUBLANES), SUBLANES)
        a, u = _group_scan(a_sc[rows, :], u_sc[rows, :], row)
        hs = a * h0_ref[pl.ds(b, 1), :] + u
        hs_sc[rows, :] = hs
        h_ref[pl.ds(b, 1), :] = hs[SUBLANES - 1:SUBLANES, :]
        return carry

    lax.fori_loop(0, n_seq, scan_group, 0, unroll=8)

    gr = _dot(n, win_ref[:, 3 * ATT_WIDTH + LRU_WIDTH:])
    rnn_ref[...] = _rms(hs_sc[...] * jax.nn.gelu(gr), grnn_ref[...]).astype(BF16)


def _mix_weight_specs():
    return [
        _const_spec((1, D_MODEL)),
        _const_spec((D_MODEL, 3 * ATT_WIDTH + 2 * LRU_WIDTH)),
        _const_spec((CONV_WIDTH, LRU_WIDTH)),
        _const_spec((1, LRU_WIDTH)),
        _const_spec((LRU_WIDTH // LANES, LANES, 2 * LANES)),
        _const_spec((1, LRU_WIDTH)),
        _const_spec((1, LRU_WIDTH)),
        _const_spec((1, LRU_WIDTH)),
        _const_spec((1, LRU_WIDTH)),
    ]


def _prompt_mix(x, weights):
    b, s, _ = x.shape
    t = PROMPT_MIX_TILE
    assert s % t == 0 and t % PROMPT_SUB_TILE == 0
    tok = lambda width: pl.BlockSpec((None, t, width), lambda i, j: (i, j, 0))
    per_seq = lambda rows: pl.BlockSpec((None, rows, LRU_WIDTH), lambda i, j: (i, 0, 0))
    return pl.pallas_call(
        _prompt_mix_kernel,
        grid=(b, s // t),
        in_specs=[tok(D_MODEL)] + _mix_weight_specs(),
        out_specs=[tok(ATT_WIDTH), tok(ATT_WIDTH), tok(ATT_WIDTH), tok(LRU_WIDTH),
                   per_seq(CONV_WIDTH - 1), per_seq(1)],
        out_shape=[jax.ShapeDtypeStruct((b, s, ATT_WIDTH), F32)] * 3
        + [jax.ShapeDtypeStruct((b, s, LRU_WIDTH), BF16),
           jax.ShapeDtypeStruct((b, CONV_WIDTH - 1, LRU_WIDTH), F32),
           jax.ShapeDtypeStruct((b, 1, LRU_WIDTH), F32)],
        scratch_shapes=[pltpu.VMEM((t + SUBLANES, LRU_WIDTH), F32)] + [pltpu.VMEM((t, LRU_WIDTH), F32)] * 4
        + [pltpu.VMEM((SUBLANES, LRU_WIDTH), F32)],
        compiler_params=pltpu.CompilerParams(dimension_semantics=("arbitrary", "arbitrary"),
                                             vmem_limit_bytes=VMEM_LIMIT),
        name="prompt_mix",
    )(x, *weights)


def _sample_mix(x, state_rows, h0, weights):
    rows = x.shape[0]
    t = min(TOKEN_TILE, rows)
    n_seq = t // SUBLANES
    tok = lambda width: pl.BlockSpec((t, width), lambda i: (i, 0))
    return pl.pallas_call(
        _sample_mix_kernel,
        grid=(rows // t,),
        in_specs=[tok(D_MODEL), tok(LRU_WIDTH), pl.BlockSpec((n_seq, LRU_WIDTH), lambda i: (i, 0))]
        + _mix_weight_specs(),
        out_specs=[tok(ATT_WIDTH), tok(ATT_WIDTH), tok(ATT_WIDTH), tok(LRU_WIDTH),
                   pl.BlockSpec((n_seq, CONV_WIDTH - 1, LRU_WIDTH), lambda i: (i, 0, 0)),
                   pl.BlockSpec((n_seq, LRU_WIDTH), lambda i: (i, 0))],
        out_shape=[jax.ShapeDtypeStruct((rows, ATT_WIDTH), F32)] * 3
        + [jax.ShapeDtypeStruct((rows, LRU_WIDTH), BF16),
           jax.ShapeDtypeStruct((rows // SUBLANES, CONV_WIDTH - 1, LRU_WIDTH), F32),
           jax.ShapeDtypeStruct((rows // SUBLANES, LRU_WIDTH), F32)],
        scratch_shapes=[pltpu.VMEM((t, LRU_WIDTH), F32)] * 5,
        compiler_params=pltpu.CompilerParams(dimension_semantics=("arbitrary",), vmem_limit_bytes=VMEM_LIMIT),
        name="sample_mix",
    )(x, state_rows, h0, *weights)


def _prompt_attn_kernel(q_ref, k_ref, v_ref, brow_ref, o_ref, tab_sc, qp_sc, kp_sc, vp_sc, pv_sc, m_sc, den_sc):
    s_len = q_ref.shape[0]
    n_branch = len(DILATIONS)
    qi = lax.broadcasted_iota(jnp.int32, (SPAN, SPAN), 0)
    ki = lax.broadcasted_iota(jnp.int32, (SPAN, SPAN), 1)
    low_head = ki < HEAD_DIM

    for g in range(n_branch):
        for h in range(HEADS_PER_TILE):
            rows = slice(h * SPAN, (h + 1) * SPAN)
            prev = pltpu.roll(jnp.broadcast_to(brow_ref[g, h, 0:1, :], (SPAN, SPAN)), 0, 1, stride=1, stride_axis=0)
            cur = pltpu.roll(jnp.broadcast_to(brow_ref[g, h, 1:2, :], (SPAN, SPAN)), 0, 1, stride=1, stride_axis=0)
            tab_sc[g, rows, 0:SPAN] = jnp.where(ki >= qi, prev, NEG_INF)
            tab_sc[g, rows, SPAN:] = jnp.where(ki <= qi, cur, NEG_INF)

    perm = DILATIONS[1]
    sub_len = s_len // perm
    assert DILATIONS == (1, perm, perm * perm) and sub_len % (2 * SPAN) == 0
    natural = (q_ref, k_ref, v_ref)
    permuted = (qp_sc, kp_sc, vp_sc)
    for src, dst in zip(natural, permuted):
        for c in range(perm):
            for i0 in range(0, sub_len, 2 * SPAN):
                dst[c * sub_len + i0:c * sub_len + i0 + 2 * SPAN, :] = src[pl.ds(c + perm * i0, 2 * SPAN, stride=perm), :]

    def attend(branch, refs, q_rows, k_rows, first):
        qr, kr, vr = refs
        n_keys = SPAN if first else 2 * SPAN
        q = qr[q_rows, :]
        q2 = jnp.concatenate([jnp.where(low_head, q, 0.0), jnp.where(low_head, 0.0, q)], axis=0).astype(BF16)
        kk = kr[k_rows, :].astype(BF16)
        vv = vr[k_rows, :].astype(BF16)
        table = tab_sc[branch, :, SPAN:] if first else tab_sc[branch]
        logits = _dot_nt(q2, kk) + table
        m = jnp.max(logits, axis=-1, keepdims=True)
        p32 = jnp.exp2(logits - m)
        den_b = jnp.broadcast_to(jnp.sum(p32, axis=-1, keepdims=True), (2 * SPAN, LANES))
        r = _dot(p32.astype(BF16), vv)
        m_b = jnp.broadcast_to(m, (2 * SPAN, LANES))
        pv_sc[branch, q_rows, :] = jnp.where(low_head, r[:SPAN], r[SPAN:])
        den_sc[branch, q_rows, :] = jnp.where(low_head, den_b[:SPAN], den_b[SPAN:])
        m_sc[branch, q_rows, :] = jnp.where(low_head, m_b[:SPAN], m_b[SPAN:])

    for i in range(s_len // SPAN):
        if i == 0:
            attend(0, natural, pl.ds(0, SPAN), pl.ds(0, SPAN), True)
        else:
            attend(0, natural, pl.ds(i * SPAN, SPAN), pl.ds((i - 1) * SPAN, 2 * SPAN), False)
    for c in range(perm):
        for i in range(sub_len // SPAN):
            q0 = c * sub_len + i * SPAN
            if i == 0:
                attend(1, permuted, pl.ds(q0, SPAN), pl.ds(q0, SPAN), True)
            else:
                attend(1, permuted, pl.ds(q0, SPAN), pl.ds(q0 - SPAN, 2 * SPAN), False)
    assert sub_len // perm == SPAN
    for c in range(perm):
        for c_hi in range(perm):
            rows = pl.ds(c * sub_len + c_hi, SPAN, stride=perm)
            attend(2, permuted, rows, rows, True)

    chunk = 2 * SPAN
    for c in range(perm):
        for i0 in range(0, sub_len, chunk):
            rows = (pl.ds(c + perm * i0, chunk, stride=perm),) + (pl.ds(c * sub_len + i0, chunk),) * (n_branch - 1)
            ms = [m_sc[g, rows[g], :] for g in range(n_branch)]
            m_all = jnp.maximum(jnp.maximum(ms[0], ms[1]), ms[2])
            num = jnp.zeros((chunk, LANES), F32)
            den = jnp.zeros((chunk, LANES), F32)
            for g in range(n_branch):
                w = jnp.exp2(ms[g] - m_all)
                num = num + w * pv_sc[g, rows[g], :]
                den = den + w * den_sc[g, rows[g], :]
            o_ref[rows[0], :] = num / den


def _prompt_attn(q, k, v, bias_rows):
    b, s, _ = q.shape
    n_branch = len(DILATIONS)
    head_pair = pl.BlockSpec((None, s, LANES), lambda i, j: (i, 0, j))
    return pl.pallas_call(
        _prompt_attn_kernel,
        grid=(b, ATT_WIDTH // LANES),
        in_specs=[head_pair, head_pair, head_pair,
                  pl.BlockSpec((n_branch, HEADS_PER_TILE, 2, SPAN), lambda i, j: (0, j, 0, 0))],
        out_specs=head_pair,
        out_shape=jax.ShapeDtypeStruct((b, s, ATT_WIDTH), F32),
        scratch_shapes=[pltpu.VMEM((n_branch, HEADS_PER_TILE * SPAN, 2 * SPAN), F32)]
        + [pltpu.VMEM((s, LANES), F32)] * 3 + [pltpu.VMEM((n_branch, s, LANES), F32)] * 3,
        compiler_params=pltpu.CompilerParams(dimension_semantics=("arbitrary", "arbitrary"),
                                             vmem_limit_bytes=VMEM_LIMIT),
        name="prompt_attn",
    )(q, k, v, bias_rows)


def _sample_attn_stages(q_ref, kn_ref, vn_ref, ckt_ref, cvt_ref, tc_ref, tn_ref, o_ref):
    t = q_ref.shape[0]
    n_buf = ckt_ref.shape[-1]
    n_branch = len(DILATIONS)
    head_of_lane = lax.broadcasted_iota(jnp.int32, (t, ATT_WIDTH), 1) // HEAD_DIM
    q = q_ref[...]
    q_bd = jnp.concatenate([jnp.where(head_of_lane == h, q, 0.0) for h in range(N_HEADS)], axis=0).astype(BF16)
    pad = jnp.zeros((SAMPLE_PAD - t, ATT_WIDTH), F32)
    k_new = jnp.concatenate([kn_ref[...], pad], axis=0).astype(BF16)
    v_new = jnp.concatenate([vn_ref[...], pad], axis=0).astype(BF16)

    kt = ckt_ref[...].reshape(ATT_WIDTH, n_buf).astype(BF16)
    half = n_buf // 2
    s_c = jnp.concatenate([_dot(q_bd, kt[:, :half]), _dot(q_bd, kt[:, half:])], axis=1)
    s_n = _dot_nt(q_bd, k_new)
    yield
    m = None
    for g in range(n_branch):
        m_g = jnp.maximum(jnp.max(s_c + tc_ref[g], axis=-1, keepdims=True),
                          jnp.max(s_n + tn_ref[g], axis=-1, keepdims=True))
        m = m_g if m is None else jnp.maximum(m, m_g)
    p_c = jnp.exp2(s_c + tc_ref[0] - m)
    p_n = jnp.exp2(s_n + tn_ref[0] - m)
    for g in range(1, n_branch):
        p_c = p_c + jnp.exp2(s_c + tc_ref[g] - m)
        p_n = p_n + jnp.exp2(s_n + tn_ref[g] - m)
    den = jnp.sum(p_c, axis=-1, keepdims=True) + jnp.sum(p_n, axis=-1, keepdims=True)
    yield
    vt = cvt_ref[...].reshape(ATT_WIDTH, n_buf).astype(BF16)
    p_cb = p_c.astype(BF16)
    out = (_dot_nt(p_cb[:, :half], vt[:, :half]) + _dot_nt(p_cb[:, half:], vt[:, half:])
           + _dot(p_n.astype(BF16), v_new)) / den
    att = jnp.zeros((t, ATT_WIDTH), F32)
    for h in range(N_HEADS):
        att = att + jnp.where(head_of_lane == h, out[h * t:(h + 1) * t, :], 0.0)
    o_ref[...] = att


def _sample_attn_specs(n_buf, steps, cached_tables, new_tables):
    tok = pl.BlockSpec((steps, ATT_WIDTH), lambda i: (i, 0))
    cache = pl.BlockSpec((None, N_HEADS, HEAD_DIM, n_buf), lambda i: (i, 0, 0, 0))
    return [tok, tok, tok, cache, cache, _const_spec(cached_tables.shape), _const_spec(new_tables.shape)], tok


def _dense_stages(x_ref, att_ref, rnn_ref, gatt_ref, woa_ref, wor_ref, g2_ref, w1_ref, w2_ref, gf_ref, y_ref):
    att_n = _rms(att_ref[...], gatt_ref[...]).astype(BF16)
    x1 = x_ref[...] + _dot(att_n, woa_ref[...]) + _dot(rnn_ref[...], wor_ref[...])
    n2 = _rms(x1, g2_ref[...]).astype(BF16)
    yield
    mlp = None
    for c in range(D_FF // FF_CHUNK):
        cols = slice(c * FF_CHUNK, (c + 1) * FF_CHUNK)
        hmid = jnp.maximum(_dot(n2, w1_ref[:, cols]), 0.0)
        part = _dot((hmid * hmid).astype(BF16), w2_ref[cols, :])
        mlp = part if mlp is None else mlp + part
        yield
    y_ref[...] = _rms(x1 + mlp, gf_ref[...])


def _interleave(*stage_generators):
    pending = list(stage_generators)
    while pending:
        pending = [g for g in pending if next(g, StopIteration) is not StopIteration]


def _dense_kernel(*refs):
    _interleave(_dense_stages(*refs))


N_DENSE_IN = 10


def _dense_and_sample_attn_kernel(*refs):
    dense_in, attn_in = refs[:N_DENSE_IN], refs[N_DENSE_IN:-2]
    y_ref, att_s_ref = refs[-2:]
    _interleave(_sample_attn_stages(*attn_in, att_s_ref), _dense_stages(*dense_in, y_ref))


def _dense_specs(t):
    tok = lambda width: pl.BlockSpec((t, width), lambda i: (i, 0))
    in_specs = [tok(D_MODEL), tok(ATT_WIDTH), tok(LRU_WIDTH),
                _const_spec((1, ATT_WIDTH)), _const_spec((ATT_WIDTH, D_MODEL)), _const_spec((LRU_WIDTH, D_MODEL)),
                _const_spec((1, D_MODEL)), _const_spec((D_MODEL, D_FF)), _const_spec((D_FF, D_MODEL)),
                _const_spec((1, D_MODEL))]
    assert len(in_specs) == N_DENSE_IN
    return in_specs, tok(D_MODEL)


def _dense(x, att, rnn, weights):
    rows = x.shape[0]
    t = min(TOKEN_TILE, rows)
    in_specs, out_spec = _dense_specs(t)
    return pl.pallas_call(
        _dense_kernel,
        grid=(rows // t,),
        in_specs=in_specs,
        out_specs=out_spec,
        out_shape=jax.ShapeDtypeStruct((rows, D_MODEL), F32),
        compiler_params=pltpu.CompilerParams(dimension_semantics=("arbitrary",), vmem_limit_bytes=VMEM_LIMIT),
        name="dense_tail",
    )(x, att, rnn, *weights)


def _dense_and_sample_attn(x, att, rnn, weights, q_s, k_new, v_new, cache_kt, cache_vt, cached_tables, new_tables,
                           steps):
    rows = x.shape[0]
    n_seq = q_s.shape[0] // steps
    t = rows // n_seq
    assert t * n_seq == rows and t % SUBLANES == 0
    dense_in, dense_out = _dense_specs(t)
    attn_in, attn_out = _sample_attn_specs(cache_kt.shape[-1], steps, cached_tables, new_tables)
    return pl.pallas_call(
        _dense_and_sample_attn_kernel,
        grid=(n_seq,),
        in_specs=dense_in + attn_in,
        out_specs=[dense_out, attn_out],
        out_shape=[jax.ShapeDtypeStruct((rows, D_MODEL), F32), jax.ShapeDtypeStruct(q_s.shape, F32)],
        compiler_params=pltpu.CompilerParams(dimension_semantics=("arbitrary",), vmem_limit_bytes=VMEM_LIMIT),
        name="dense_tail_sample_attn",
    )(x, att, rnn, *weights, q_s, k_new, v_new, cache_kt, cache_vt, cached_tables, new_tables)


def _t5_bucket(dist):
    max_exact = N_BUCKETS // 2
    d_f = jnp.maximum(dist, max_exact).astype(F32)
    large = max_exact + (jnp.log(d_f / max_exact) / math.log(WIN_MAX / max_exact)
                         * (N_BUCKETS - max_exact)).astype(jnp.int32)
    large = jnp.minimum(large, N_BUCKETS - 1)
    return jnp.where(dist < max_exact, dist, large)


def _branch_bias(rel_bias, dil):
    dist = jnp.arange(SPAN + 1, dtype=jnp.int32) * dil
    return rel_bias[_t5_bucket(dist)].astype(F32).T * LOG2_E


def _prompt_bias_rows(rel_bias):
    rows = []
    for dil in DILATIONS:
        bias = _branch_bias(rel_bias, dil)
        prev = bias[:, SPAN:0:-1]
        cur = jnp.concatenate([bias[:, 0:1], bias[:, SPAN - 1:0:-1]], axis=1)
        rows.append(jnp.stack([prev, cur], axis=1))
    return jnp.stack(rows)


def _sample_bias_tables(rel_bias, n_buf, steps):
    cached, new = [], []
    pos = np.arange(n_buf)
    new_row = np.arange(SAMPLE_PAD)
    delta_new = np.arange(steps)[:, None] - new_row[None, :]
    for dil in DILATIONS:
        bias = _branch_bias(rel_bias, dil)
        every = [bias[:, SPAN:0:-1]] + [jnp.full((N_HEADS, SPAN), NEG_INF, F32)] * (dil - 1)
        row0 = jnp.concatenate([jnp.full((N_HEADS, n_buf - SPAN * dil), NEG_INF, F32),
                                jnp.stack(every, axis=-1).reshape(N_HEADS, SPAN * dil)], axis=1)
        rows = [jnp.where(pos[None, :] >= t, jnp.roll(row0, t, axis=1), NEG_INF) for t in range(steps)]
        cached.append(jnp.stack(rows, axis=1).reshape(N_HEADS * steps, n_buf))
        valid = (new_row[None, :] < steps) & (delta_new >= 0) & (delta_new % dil == 0)
        j_of = np.where(valid, delta_new // dil, -1)
        one_hot = (j_of[None] == np.arange(steps)[:, None, None]).astype(np.float32)
        vals = jnp.sum(bias[:, :steps, None, None] * one_hot[None], axis=1)
        new.append(jnp.where(valid[None], vals, NEG_INF).reshape(N_HEADS * steps, SAMPLE_PAD))
    return jnp.stack(cached), jnp.stack(new)


def _paired_gate_weights(gate_a_w, gate_x_w):
    def pair_diag(w):
        z = jnp.zeros((LRU_BLOCK, LRU_BLOCK), w.dtype)
        return jnp.stack([jnp.block([[w[2 * p], z], [z, w[2 * p + 1]]]) for p in range(N_LRU_BLOCKS // 2)])
    return jnp.concatenate([pair_diag(gate_a_w), pair_diag(gate_x_w)], axis=-1).astype(BF16)


def kernel(x_prompt, x_sample, cache_k, cache_v, state_conv, state_h, norm1_g, w_in, rel_bias, conv_w, conv_b,
           gate_a_w, gate_a_b, gate_x_w, gate_x_b, lru_lambda, att_out_g, rnn_out_g, w_out, norm2_g, w_mlp_in,
           w_mlp_out, final_g):
    depth = w_in.shape[0]
    assert depth == 1, "the final norm is fused into the single layer's dense kernel"
    b, s, _ = x_prompt.shape
    db, steps, _ = x_sample.shape
    n_buf = cache_k.shape[2]
    assert steps == SUBLANES and s == WIN_MAX and n_buf == WIN_MAX
    l = 0
    row = lambda v: v.reshape(1, -1).astype(F32)

    w_in_scaled = jnp.concatenate([w_in[l, :, :ATT_WIDTH] * (Q_SCALE * LOG2_E), w_in[l, :, ATT_WIDTH:]], axis=1)
    mix_w = (row(norm1_g[l]), w_in_scaled.astype(BF16), conv_w[l], row(conv_b[l]),
             _paired_gate_weights(0.5 * gate_a_w[l], 0.5 * gate_x_w[l]), 0.5 * row(gate_a_b[l]), 0.5 * row(gate_x_b[l]),
             row(lru_lambda[l]), row(rnn_out_g[l]))
    dense_w = (row(att_out_g[l]), w_out[l, :ATT_WIDTH].astype(BF16), w_out[l, ATT_WIDTH:].astype(BF16),
               row(norm2_g[l]), w_mlp_in[l].astype(BF16), w_mlp_out[l].astype(BF16), row(final_g))

    xs = x_sample.reshape(db * steps, D_MODEL)
    state_rows = jnp.pad(state_conv[l], ((0, 0), (0, SUBLANES - (CONV_WIDTH - 1)), (0, 0))).reshape(db * steps, LRU_WIDTH)
    qs, ks, vs, rnn_s, conv_s, h_s = _sample_mix(xs, state_rows, state_h[l], mix_w)
    cache_kt = jnp.transpose(cache_k[l], (0, 2, 3, 1))
    cache_vt = jnp.transpose(cache_v[l], (0, 2, 3, 1))

    q, k, v, rnn, conv_p, h_p = _prompt_mix(x_prompt, mix_w)
    att = _prompt_attn(q, k, v, _prompt_bias_rows(rel_bias))
    y_prompt, att_s = _dense_and_sample_attn(
        x_prompt.reshape(b * s, D_MODEL), att.reshape(b * s, ATT_WIDTH), rnn.reshape(b * s, LRU_WIDTH), dense_w,
        qs, ks, vs, cache_kt, cache_vt, *_sample_bias_tables(rel_bias, n_buf, steps), steps)
    y_prompt = y_prompt.reshape(b, s, D_MODEL)
    y_sample = _dense(xs, att_s, rnn_s, dense_w).reshape(db, steps, D_MODEL)

    kv_p = (1, b, s, N_HEADS, HEAD_DIM)
    kv_s = (1, db, steps, N_HEADS, HEAD_DIM)
    return (y_prompt, y_sample,
            k.reshape(kv_p), v.reshape(kv_p), conv_p[None], h_p.reshape(1, b, LRU_WIDTH),
            ks.reshape(kv_s), vs.reshape(kv_s), conv_s[None], h_s[None])
```

```python
import functools
import math

import numpy as np
import jax
import jax.numpy as jnp
from jax import lax
from jax.experimental import pallas as pl
from jax.experimental.pallas import tpu as pltpu

F32 = jnp.float32
BF16 = jnp.bfloat16

D_MODEL = 1024
ATT_WIDTH = 512
LRU_WIDTH = 512
HEAD_DIM = 64
N_HEADS = 8
N_LRU_BLOCKS = 8
LRU_BLOCK = 64
CONV_WIDTH = 4
LRU_C = 8.0
D_FF = 4096
SPAN = 128
DILATIONS = (1, 4, 16)
WIN_MAX = 2048
N_BUCKETS = 32
NORM_EPS = 1e-6
NEG_INF = -1e30
Q_SCALE = HEAD_DIM ** -0.5
LOG2_E = math.log2(math.e)

SUBLANES = 8
LANES = 128
HEADS_PER_TILE = LANES // HEAD_DIM
VMEM_LIMIT = 56 * 1024 * 1024

TOKEN_TILE = 512
PROMPT_MIX_TILE = 1024
PROMPT_SUB_TILE = 256
FF_CHUNK = 1024
SAMPLE_PAD = 128


def _dot(a, b):
    return jnp.dot(a, b, preferred_element_type=F32)


def _dot_nt(a, b):
    return lax.dot_general(a, b, (((1,), (1,)), ((), ())), preferred_element_type=F32)


def _rms(x, g):
    return x * lax.rsqrt(jnp.mean(x * x, axis=-1, keepdims=True) + NORM_EPS) * g


def _const_spec(shape):
    nd = len(shape)
    return pl.BlockSpec(shape, lambda *_: (0,) * nd, pipeline_mode=pl.Buffered(1))


def _sigmoid_of_twice(half_x):
    return 0.5 * jnp.tanh(half_x) + 0.5


def _lru_gates(xc, wg_ref, ba_ref, bx_ref, lam_ref, a_sc, u_sc):
    xcb = xc.astype(BF16)
    lam = lam_ref[...]
    softplus_neg_lam = jnp.maximum(-lam, 0.0) + jnp.log1p(jnp.exp(-jnp.abs(lam)))
    log_a_per_r = (-LRU_C) * softplus_neg_lam
    for p in range(LRU_WIDTH // LANES):
        cols = slice(p * LANES, (p + 1) * LANES)
        g = _dot(xcb[:, cols], wg_ref[p])
        r = _sigmoid_of_twice(g[:, :LANES] + ba_ref[:, cols])
        gi = _sigmoid_of_twice(g[:, LANES:] + bx_ref[:, cols])
        log_a = r * log_a_per_r[:, cols]
        a = jnp.exp(log_a)
        a_sc[:, cols] = a
        u_sc[:, cols] = jnp.sqrt(-jnp.tanh(log_a) * (a * a + 1.0)) * (gi * xc[:, cols])


def _group_scan(a, u, row):
    for s in (1, 2, 4):
        keep = row >= s
        u = jnp.where(keep, a * pltpu.roll(u, s, 0) + u, u)
        a = jnp.where(keep, a * pltpu.roll(a, s, 0), a)
    return a, u


def _prompt_mix_kernel(x_ref, g1_ref, win_ref, cw_ref, cb_ref, wg_ref, ba_ref, bx_ref, lam_ref, grnn_ref,
                       q_ref, k_ref, v_ref, rnn_ref, conv_ref, h_ref,
                       xr_ext, gr_sc, a_sc, u_sc, hs_sc, h_carry):
    t = x_ref.shape[0]
    sub = PROMPT_SUB_TILE
    n_sub = t // sub

    @pl.when(pl.program_id(1) == 0)
    def _():
        xr_ext[0:SUBLANES, :] = jnp.zeros((SUBLANES, LRU_WIDTH), F32)
        h_carry[...] = jnp.zeros_like(h_carry)

    row = lax.broadcasted_iota(jnp.int32, (SUBLANES, LRU_WIDTH), 0)

    def project(s):
        rows = slice(s * sub, (s + 1) * sub)
        n = _rms(x_ref[rows, :], g1_ref[...]).astype(BF16)
        q_ref[rows, :] = _dot(n, win_ref[:, 0:ATT_WIDTH])
        k_ref[rows, :] = _dot(n, win_ref[:, ATT_WIDTH:2 * ATT_WIDTH])
        v_ref[rows, :] = _dot(n, win_ref[:, 2 * ATT_WIDTH:3 * ATT_WIDTH])
        xr_ext[SUBLANES + s * sub:SUBLANES + (s + 1) * sub, :] = _dot(
            n, win_ref[:, 3 * ATT_WIDTH:3 * ATT_WIDTH + LRU_WIDTH])
        gr_sc[rows, :] = _dot(n, win_ref[:, 3 * ATT_WIDTH + LRU_WIDTH:])

    def recur(s, h):
        rows = slice(s * sub, (s + 1) * sub)
        xc = cb_ref[...]
        for back in range(CONV_WIDTH):
            xc = xc + (cw_ref[CONV_WIDTH - 1 - back:CONV_WIDTH - back, :]
                       * xr_ext[pl.ds(SUBLANES + s * sub - back, sub), :])
        _lru_gates(xc, wg_ref, ba_ref, bx_ref, lam_ref, a_sc.at[rows], u_sc.at[rows])
        for g in range(sub // SUBLANES):
            grp = slice(s * sub + g * SUBLANES, s * sub + (g + 1) * SUBLANES)
            a, u = _group_scan(a_sc[grp, :], u_sc[grp, :], row)
            hs = a * h + u
            hs_sc[grp, :] = hs
            h = jnp.broadcast_to(hs[SUBLANES - 1:SUBLANES, :], hs.shape)
        rnn_ref[rows, :] = _rms(hs_sc[rows, :] * jax.nn.gelu(gr_sc[rows, :]), grnn_ref[...]).astype(BF16)
        return h

    h = h_carry[...]
    project(0)
    for s in range(1, n_sub):
        project(s)
        h = recur(s - 1, h)
    h = recur(n_sub - 1, h)
    h_carry[...] = h
    h_ref[...] = h[0:1, :]
    conv_ref[...] = xr_ext[SUBLANES + t - (CONV_WIDTH - 1):SUBLANES + t, :]
    xr_ext[0:SUBLANES, :] = xr_ext[t:t + SUBLANES, :]


def _sample_mix_kernel(x_ref, st_ref, h0_ref, g1_ref, win_ref, cw_ref, cb_ref, wg_ref, ba_ref, bx_ref, lam_ref,
                       grnn_ref, q_ref, k_ref, v_ref, rnn_ref, conv_ref, h_ref,
                       xr_sc, xc_sc, a_sc, u_sc, hs_sc):
    t = x_ref.shape[0]
    n_seq = t // SUBLANES
    n = _rms(x_ref[...], g1_ref[...]).astype(BF16)
    q_ref[...] = _dot(n, win_ref[:, 0:ATT_WIDTH])
    k_ref[...] = _dot(n, win_ref[:, ATT_WIDTH:2 * ATT_WIDTH])
    v_ref[...] = _dot(n, win_ref[:, 2 * ATT_WIDTH:3 * ATT_WIDTH])
    xr_sc[...] = _dot(n, win_ref[:, 3 * ATT_WIDTH:3 * ATT_WIDTH + LRU_WIDTH])

    row = lax.broadcasted_iota(jnp.int32, (SUBLANES, LRU_WIDTH), 0)
    n_state = CONV_WIDTH - 1

    def conv_group(b, carry):
        rows = pl.ds(pl.multiple_of(b * SUBLANES, SUBLANES), SUBLANES)
        xg = xr_sc[rows, :]
        sg = st_ref[rows, :]
        xc = cb_ref[...] + cw_ref[CONV_WIDTH - 1:CONV_WIDTH, :] * xg
        for back in range(1, CONV_WIDTH):
            prev = jnp.where(row >= back, pltpu.roll(xg, back, 0),
                             pltpu.roll(sg, (back - n_state) % SUBLANES, 0))
            xc = xc + cw_ref[CONV_WIDTH - 1 - back:CONV_WIDTH - back, :] * prev
        xc_sc[rows, :] = xc
        conv_ref[b] = xg[SUBLANES - n_state:, :]
        return carry

    lax.fori_loop(0, n_seq, conv_group, 0, unroll=8)

    _lru_gates(xc_sc[...], wg_ref, ba_ref, bx_ref, lam_ref, a_sc, u_sc)

    def scan_group(b, carry):
        rows = pl.ds(pl.multiple_of(b * SUBLANES, SUBLANES), SUBLANES)
        a, u = _group_scan(a_sc[rows, :], u_sc[rows, :], row)
        hs = a * h0_ref[pl.ds(b, 1), :] + u
        hs_sc[rows, :] = hs
        h_ref[pl.ds(b, 1), :] = hs[SUBLANES - 1:SUBLANES, :]
        return carry

    lax.fori_loop(0, n_seq, scan_group, 0, unroll=8)

    gr = _dot(n, win_ref[:, 3 * ATT_WIDTH + LRU_WIDTH:])
    rnn_ref[...] = _rms(hs_sc[...] * jax.nn.gelu(gr), grnn_ref[...]).astype(BF16)


def _mix_weight_specs():
    return [
        _const_spec((1, D_MODEL)),
        _const_spec((D_MODEL, 3 * ATT_WIDTH + 2 * LRU_WIDTH)),
        _const_spec((CONV_WIDTH, LRU_WIDTH)),
        _const_spec((1, LRU_WIDTH)),
        _const_spec((LRU_WIDTH // LANES, LANES, 2 * LANES)),
        _const_spec((1, LRU_WIDTH)),
        _const_spec((1, LRU_WIDTH)),
        _const_spec((1, LRU_WIDTH)),
        _const_spec((1, LRU_WIDTH)),
    ]


def _prompt_mix(x, weights):
    b, s, _ = x.shape
    t = PROMPT_MIX_TILE
    assert s % t == 0 and t % PROMPT_SUB_TILE == 0
    tok = lambda width: pl.BlockSpec((None, t, width), lambda i, j: (i, j, 0))
    per_seq = lambda rows: pl.BlockSpec((None, rows, LRU_WIDTH), lambda i, j: (i, 0, 0))
    return pl.pallas_call(
        _prompt_mix_kernel,
        grid=(b, s // t),
        in_specs=[tok(D_MODEL)] + _mix_weight_specs(),
        out_specs=[tok(ATT_WIDTH), tok(ATT_WIDTH), tok(ATT_WIDTH), tok(LRU_WIDTH),
                   per_seq(CONV_WIDTH - 1), per_seq(1)],
        out_shape=[jax.ShapeDtypeStruct((b, s, ATT_WIDTH), F32)] * 3
        + [jax.ShapeDtypeStruct((b, s, LRU_WIDTH), BF16),
           jax.ShapeDtypeStruct((b, CONV_WIDTH - 1, LRU_WIDTH), F32),
           jax.ShapeDtypeStruct((b, 1, LRU_WIDTH), F32)],
        scratch_shapes=[pltpu.VMEM((t + SUBLANES, LRU_WIDTH), F32)] + [pltpu.VMEM((t, LRU_WIDTH), F32)] * 4
        + [pltpu.VMEM((SUBLANES, LRU_WIDTH), F32)],
        compiler_params=pltpu.CompilerParams(dimension_semantics=("arbitrary", "arbitrary"),
                                             vmem_limit_bytes=VMEM_LIMIT),
        name="prompt_mix",
    )(x, *weights)


def _sample_mix(x, state_rows, h0, weights):
    rows = x.shape[0]
    t = min(TOKEN_TILE, rows)
    n_seq = t // SUBLANES
    tok = lambda width: pl.BlockSpec((t, width), lambda i: (i, 0))
    return pl.pallas_call(
        _sample_mix_kernel,
        grid=(rows // t,),
        in_specs=[tok(D_MODEL), tok(LRU_WIDTH), pl.BlockSpec((n_seq, LRU_WIDTH), lambda i: (i, 0))]
        + _mix_weight_specs(),
        out_specs=[tok(ATT_WIDTH), tok(ATT_WIDTH), tok(ATT_WIDTH), tok(LRU_WIDTH),
                   pl.BlockSpec((n_seq, CONV_WIDTH - 1, LRU_WIDTH), lambda i: (i, 0, 0)),
                   pl.BlockSpec((n_seq, LRU_WIDTH), lambda i: (i, 0))],
        out_shape=[jax.ShapeDtypeStruct((rows, ATT_WIDTH), F32)] * 3
        + [jax.ShapeDtypeStruct((rows, LRU_WIDTH), BF16),
           jax.ShapeDtypeStruct((rows // SUBLANES, CONV_WIDTH - 1, LRU_WIDTH), F32),
           jax.ShapeDtypeStruct((rows // SUBLANES, LRU_WIDTH), F32)],
        scratch_shapes=[pltpu.VMEM((t, LRU_WIDTH), F32)] * 5,
        compiler_params=pltpu.CompilerParams(dimension_semantics=("arbitrary",), vmem_limit_bytes=VMEM_LIMIT),
        name="sample_mix",
    )(x, state_rows, h0, *weights)


def _prompt_attn_kernel(q_ref, k_ref, v_ref, brow_ref, o_ref, tab_sc, qp_sc, kp_sc, vp_sc, pv_sc, m_sc, den_sc):
    s_len = q_ref.shape[0]
    n_branch = len(DILATIONS)
    qi = lax.broadcasted_iota(jnp.int32, (SPAN, SPAN), 0)
    ki = lax.broadcasted_iota(jnp.int32, (SPAN, SPAN), 1)
    low_head = ki < HEAD_DIM

    for g in range(n_branch):
        for h in range(HEADS_PER_TILE):
            rows = slice(h * SPAN, (h + 1) * SPAN)
            prev = pltpu.roll(jnp.broadcast_to(brow_ref[g, h, 0:1, :], (SPAN, SPAN)), 0, 1, stride=1, stride_axis=0)
            cur = pltpu.roll(jnp.broadcast_to(brow_ref[g, h, 1:2, :], (SPAN, SPAN)), 0, 1, stride=1, stride_axis=0)
            tab_sc[g, rows, 0:SPAN] = jnp.where(ki >= qi, prev, NEG_INF)
            tab_sc[g, rows, SPAN:] = jnp.where(ki <= qi, cur, NEG_INF)

    perm = DILATIONS[1]
    sub_len = s_len // perm
    assert DILATIONS == (1, perm, perm * perm) and sub_len % (2 * SPAN) == 0
    natural = (q_ref, k_ref, v_ref)
    permuted = (qp_sc, kp_sc, vp_sc)
    for src, dst in zip(natural, permuted):
        for c in range(perm):
            for i0 in range(0, sub_len, 2 * SPAN):
                dst[c * sub_len + i0:c * sub_len + i0 + 2 * SPAN, :] = src[pl.ds(c + perm * i0, 2 * SPAN, stride=perm), :]

    def attend(branch, refs, q_rows, k_rows, first):
        qr, kr, vr = refs
        n_keys = SPAN if first else 2 * SPAN
        q = qr[q_rows, :]
        q2 = jnp.concatenate([jnp.where(low_head, q, 0.0), jnp.where(low_head, 0.0, q)], axis=0).astype(BF16)
        kk = kr[k_rows, :].astype(BF16)
        vv = vr[k_rows, :].astype(BF16)
        table = tab_sc[branch, :, SPAN:] if first else tab_sc[branch]
        logits = _dot_nt(q2, kk) + table
        m = jnp.max(logits, axis=-1, keepdims=True)
        p32 = jnp.exp2(logits - m)
        den_b = jnp.broadcast_to(jnp.sum(p32, axis=-1, keepdims=True), (2 * SPAN, LANES))
        r = _dot(p32.astype(BF16), vv)
        m_b = jnp.broadcast_to(m, (2 * SPAN, LANES))
        pv_sc[branch, q_rows, :] = jnp.where(low_head, r[:SPAN], r[SPAN:])
        den_sc[branch, q_rows, :] = jnp.where(low_head, den_b[:SPAN], den_b[SPAN:])
        m_sc[branch, q_rows, :] = jnp.where(low_head, m_b[:SPAN], m_b[SPAN:])

    for i in range(s_len // SPAN):
        if i == 0:
            attend(0, natural, pl.ds(0, SPAN), pl.ds(0, SPAN), True)
        else:
            attend(0, natural, pl.ds(i * SPAN, SPAN), pl.ds((i - 1) * SPAN, 2 * SPAN), False)
    for c in range(perm):
        for i in range(sub_len // SPAN):
            q0 = c * sub_len + i * SPAN
            if i == 0:
                attend(1, permuted, pl.ds(q0, SPAN), pl.ds(q0, SPAN), True)
            else:
                attend(1, permuted, pl.ds(q0, SPAN), pl.ds(q0 - SPAN, 2 * SPAN), False)
    assert sub_len // perm == SPAN
    for c in range(perm):
        for c_hi in range(perm):
            rows = pl.ds(c * sub_len + c_hi, SPAN, stride=perm)
            attend(2, permuted, rows, rows, True)

    chunk = 2 * SPAN
    for c in range(perm):
        for i0 in range(0, sub_len, chunk):
            rows = (pl.ds(c + perm * i0, chunk, stride=perm),) + (pl.ds(c * sub_len + i0, chunk),) * (n_branch - 1)
            ms = [m_sc[g, rows[g], :] for g in range(n_branch)]
            m_all = jnp.maximum(jnp.maximum(ms[0], ms[1]), ms[2])
            num = jnp.zeros((chunk, LANES), F32)
            den = jnp.zeros((chunk, LANES), F32)
            for g in range(n_branch):
                w = jnp.exp2(ms[g] - m_all)
                num = num + w * pv_sc[g, rows[g], :]
                den = den + w * den_sc[g, rows[g], :]
            o_ref[rows[0], :] = num / den


def _prompt_attn(q, k, v, bias_rows):
    b, s, _ = q.shape
    n_branch = len(DILATIONS)
    head_pair = pl.BlockSpec((None, s, LANES), lambda i, j: (i, 0, j))
    return pl.pallas_call(
        _prompt_attn_kernel,
        grid=(b, ATT_WIDTH // LANES),
        in_specs=[head_pair, head_pair, head_pair,
                  pl.BlockSpec((n_branch, HEADS_PER_TILE, 2, SPAN), lambda i, j: (0, j, 0, 0))],
        out_specs=head_pair,
        out_shape=jax.ShapeDtypeStruct((b, s, ATT_WIDTH), F32),
        scratch_shapes=[pltpu.VMEM((n_branch, HEADS_PER_TILE * SPAN, 2 * SPAN), F32)]
        + [pltpu.VMEM((s, LANES), F32)] * 3 + [pltpu.VMEM((n_branch, s, LANES), F32)] * 3,
        compiler_params=pltpu.CompilerParams(dimension_semantics=("arbitrary", "arbitrary"),
                                             vmem_limit_bytes=VMEM_LIMIT),
        name="prompt_attn",
    )(q, k, v, bias_rows)


def _sample_attn_kernel(q_ref, kn_ref, vn_ref, ckt_ref, cvt_ref, tc_ref, tn_ref, o_ref):
    t = q_ref.shape[0]
    n_buf = ckt_ref.shape[-1]
    n_branch = len(DILATIONS)
    head_of_lane = lax.broadcasted_iota(jnp.int32, (t, ATT_WIDTH), 1) // HEAD_DIM
    q = q_ref[...]
    q_bd = jnp.concatenate([jnp.where(head_of_lane == h, q, 0.0) for h in range(N_HEADS)], axis=0).astype(BF16)
    pad = jnp.zeros((SAMPLE_PAD - t, ATT_WIDTH), F32)
    k_new = jnp.concatenate([kn_ref[...], pad], axis=0).astype(BF16)
    v_new = jnp.concatenate([vn_ref[...], pad], axis=0).astype(BF16)

    kt = ckt_ref[...].reshape(ATT_WIDTH, n_buf).astype(BF16)
    half = n_buf // 2
    s_c = jnp.concatenate([_dot(q_bd, kt[:, :half]), _dot(q_bd, kt[:, half:])], axis=1)
    s_n = _dot_nt(q_bd, k_new)
    m = None
    for g in range(n_branch):
        m_g = jnp.maximum(jnp.max(s_c + tc_ref[g], axis=-1, keepdims=True),
                          jnp.max(s_n + tn_ref[g], axis=-1, keepdims=True))
        m = m_g if m is None else jnp.maximum(m, m_g)
    p_c = jnp.exp2(s_c + tc_ref[0] - m)
    p_n = jnp.exp2(s_n + tn_ref[0] - m)
    for g in range(1, n_branch):
        p_c = p_c + jnp.exp2(s_c + tc_ref[g] - m)
        p_n = p_n + jnp.exp2(s_n + tn_ref[g] - m)
    den = jnp.sum(p_c, axis=-1, keepdims=True) + jnp.sum(p_n, axis=-1, keepdims=True)
    vt = cvt_ref[...].reshape(ATT_WIDTH, n_buf).astype(BF16)
    p_cb = p_c.astype(BF16)
    out = (_dot_nt(p_cb[:, :half], vt[:, :half]) + _dot_nt(p_cb[:, half:], vt[:, half:])
           + _dot(p_n.astype(BF16), v_new)) / den
    att = jnp.zeros((t, ATT_WIDTH), F32)
    for h in range(N_HEADS):
        att = att + jnp.where(head_of_lane == h, out[h * t:(h + 1) * t, :], 0.0)
    o_ref[...] = att


def _sample_attn_specs(n_buf, steps, cached_tables, new_tables):
    tok = pl.BlockSpec((steps, ATT_WIDTH), lambda i: (i, 0))
    cache = pl.BlockSpec((None, N_HEADS, HEAD_DIM, n_buf), lambda i: (i, 0, 0, 0))
    return [tok, tok, tok, cache, cache, _const_spec(cached_tables.shape), _const_spec(new_tables.shape)], tok


def _dense_kernel(x_ref, att_ref, rnn_ref, gatt_ref, woa_ref, wor_ref, g2_ref, w1_ref, w2_ref, gf_ref, y_ref):
    att_n = _rms(att_ref[...], gatt_ref[...]).astype(BF16)
    x1 = x_ref[...] + _dot(att_n, woa_ref[...]) + _dot(rnn_ref[...], wor_ref[...])
    n2 = _rms(x1, g2_ref[...]).astype(BF16)
    mlp = None
    for c in range(D_FF // FF_CHUNK):
        cols = slice(c * FF_CHUNK, (c + 1) * FF_CHUNK)
        hmid = jnp.maximum(_dot(n2, w1_ref[:, cols]), 0.0)
        part = _dot((hmid * hmid).astype(BF16), w2_ref[cols, :])
        mlp = part if mlp is None else mlp + part
    y_ref[...] = _rms(x1 + mlp, gf_ref[...])


N_DENSE_IN = 10


def _dense_and_sample_attn_kernel(*refs):
    dense_in, attn_in = refs[:N_DENSE_IN], refs[N_DENSE_IN:-2]
    y_ref, att_s_ref = refs[-2:]
    _sample_attn_kernel(*attn_in, att_s_ref)
    _dense_kernel(*dense_in, y_ref)


def _dense_specs(t):
    tok = lambda width: pl.BlockSpec((t, width), lambda i: (i, 0))
    in_specs = [tok(D_MODEL), tok(ATT_WIDTH), tok(LRU_WIDTH),
                _const_spec((1, ATT_WIDTH)), _const_spec((ATT_WIDTH, D_MODEL)), _const_spec((LRU_WIDTH, D_MODEL)),
                _const_spec((1, D_MODEL)), _const_spec((D_MODEL, D_FF)), _const_spec((D_FF, D_MODEL)),
                _const_spec((1, D_MODEL))]
    assert len(in_specs) == N_DENSE_IN
    return in_specs, tok(D_MODEL)


def _dense(x, att, rnn, weights):
    rows = x.shape[0]
    t = min(TOKEN_TILE, rows)
    in_specs, out_spec = _dense_specs(t)
    return pl.pallas_call(
        _dense_kernel,
        grid=(rows // t,),
        in_specs=in_specs,
        out_specs=out_spec,
        out_shape=jax.ShapeDtypeStruct((rows, D_MODEL), F32),
        compiler_params=pltpu.CompilerParams(dimension_semantics=("arbitrary",), vmem_limit_bytes=VMEM_LIMIT),
        name="dense_tail",
    )(x, att, rnn, *weights)


def _dense_and_sample_attn(x, att, rnn, weights, q_s, k_new, v_new, cache_kt, cache_vt, cached_tables, new_tables,
                           steps):
    rows = x.shape[0]
    n_seq = q_s.shape[0] // steps
    t = rows // n_seq
    assert t * n_seq == rows and t % SUBLANES == 0
    dense_in, dense_out = _dense_specs(t)
    attn_in, attn_out = _sample_attn_specs(cache_kt.shape[-1], steps, cached_tables, new_tables)
    return pl.pallas_call(
        _dense_and_sample_attn_kernel,
        grid=(n_seq,),
        in_specs=dense_in + attn_in,
        out_specs=[dense_out, attn_out],
        out_shape=[jax.ShapeDtypeStruct((rows, D_MODEL), F32), jax.ShapeDtypeStruct(q_s.shape, F32)],
        compiler_params=pltpu.CompilerParams(dimension_semantics=("arbitrary",), vmem_limit_bytes=VMEM_LIMIT),
        name="dense_tail_sample_attn",
    )(x, att, rnn, *weights, q_s, k_new, v_new, cache_kt, cache_vt, cached_tables, new_tables)


def _t5_bucket(dist):
    max_exact = N_BUCKETS // 2
    d_f = jnp.maximum(dist, max_exact).astype(F32)
    large = max_exact + (jnp.log(d_f / max_exact) / math.log(WIN_MAX / max_exact)
                         * (N_BUCKETS - max_exact)).astype(jnp.int32)
    large = jnp.minimum(large, N_BUCKETS - 1)
    return jnp.where(dist < max_exact, dist, large)


def _branch_bias(rel_bias, dil):
    dist = jnp.arange(SPAN + 1, dtype=jnp.int32) * dil
    return rel_bias[_t5_bucket(dist)].astype(F32).T * LOG2_E


def _prompt_bias_rows(rel_bias):
    rows = []
    for dil in DILATIONS:
        bias = _branch_bias(rel_bias, dil)
        prev = bias[:, SPAN:0:-1]
        cur = jnp.concatenate([bias[:, 0:1], bias[:, SPAN - 1:0:-1]], axis=1)
        rows.append(jnp.stack([prev, cur], axis=1))
    return jnp.stack(rows)


def _sample_bias_tables(rel_bias, n_buf, steps):
    cached, new = [], []
    pos = np.arange(n_buf)
    new_row = np.arange(SAMPLE_PAD)
    delta_new = np.arange(steps)[:, None] - new_row[None, :]
    for dil in DILATIONS:
        bias = _branch_bias(rel_bias, dil)
        every = [bias[:, SPAN:0:-1]] + [jnp.full((N_HEADS, SPAN), NEG_INF, F32)] * (dil - 1)
        row0 = jnp.concatenate([jnp.full((N_HEADS, n_buf - SPAN * dil), NEG_INF, F32),
                                jnp.stack(every, axis=-1).reshape(N_HEADS, SPAN * dil)], axis=1)
        rows = [jnp.where(pos[None, :] >= t, jnp.roll(row0, t, axis=1), NEG_INF) for t in range(steps)]
        cached.append(jnp.stack(rows, axis=1).reshape(N_HEADS * steps, n_buf))
        valid = (new_row[None, :] < steps) & (delta_new >= 0) & (delta_new % dil == 0)
        j_of = np.where(valid, delta_new // dil, -1)
        one_hot = (j_of[None] == np.arange(steps)[:, None, None]).astype(np.float32)
        vals = jnp.sum(bias[:, :steps, None, None] * one_hot[None], axis=1)
        new.append(jnp.where(valid[None], vals, NEG_INF).reshape(N_HEADS * steps, SAMPLE_PAD))
    return jnp.stack(cached), jnp.stack(new)


def _paired_gate_weights(gate_a_w, gate_x_w):
    def pair_diag(w):
        z = jnp.zeros((LRU_BLOCK, LRU_BLOCK), w.dtype)
        return jnp.stack([jnp.block([[w[2 * p], z], [z, w[2 * p + 1]]]) for p in range(N_LRU_BLOCKS // 2)])
    return jnp.concatenate([pair_diag(gate_a_w), pair_diag(gate_x_w)], axis=-1).astype(BF16)


def kernel(x_prompt, x_sample, cache_k, cache_v, state_conv, state_h, norm1_g, w_in, rel_bias, conv_w, conv_b,
           gate_a_w, gate_a_b, gate_x_w, gate_x_b, lru_lambda, att_out_g, rnn_out_g, w_out, norm2_g, w_mlp_in,
           w_mlp_out, final_g):
    depth = w_in.shape[0]
    assert depth == 1, "the final norm is fused into the single layer's dense kernel"
    b, s, _ = x_prompt.shape
    db, steps, _ = x_sample.shape
    n_buf = cache_k.shape[2]
    assert steps == SUBLANES and s == WIN_MAX and n_buf == WIN_MAX
    l = 0
    row = lambda v: v.reshape(1, -1).astype(F32)

    w_in_scaled = jnp.concatenate([w_in[l, :, :ATT_WIDTH] * (Q_SCALE * LOG2_E), w_in[l, :, ATT_WIDTH:]], axis=1)
    mix_w = (row(norm1_g[l]), w_in_scaled.astype(BF16), conv_w[l], row(conv_b[l]),
             _paired_gate_weights(0.5 * gate_a_w[l], 0.5 * gate_x_w[l]), 0.5 * row(gate_a_b[l]), 0.5 * row(gate_x_b[l]),
             row(lru_lambda[l]), row(rnn_out_g[l]))
    dense_w = (row(att_out_g[l]), w_out[l, :ATT_WIDTH].astype(BF16), w_out[l, ATT_WIDTH:].astype(BF16),
               row(norm2_g[l]), w_mlp_in[l].astype(BF16), w_mlp_out[l].astype(BF16), row(final_g))

    xs = x_sample.reshape(db * steps, D_MODEL)
    state_rows = jnp.pad(state_conv[l], ((0, 0), (0, SUBLANES - (CONV_WIDTH - 1)), (0, 0))).reshape(db * steps, LRU_WIDTH)
    qs, ks, vs, rnn_s, conv_s, h_s = _sample_mix(xs, state_rows, state_h[l], mix_w)
    cache_kt = jnp.transpose(cache_k[l], (0, 2, 3, 1))
    cache_vt = jnp.transpose(cache_v[l], (0, 2, 3, 1))

    q, k, v, rnn, conv_p, h_p = _prompt_mix(x_prompt, mix_w)
    att = _prompt_attn(q, k, v, _prompt_bias_rows(rel_bias))
    y_prompt, att_s = _dense_and_sample_attn(
        x_prompt.reshape(b * s, D_MODEL), att.reshape(b * s, ATT_WIDTH), rnn.reshape(b * s, LRU_WIDTH), dense_w,
        qs, ks, vs, cache_kt, cache_vt, *_sample_bias_tables(rel_bias, n_buf, steps), steps)
    y_prompt = y_prompt.reshape(b, s, D_MODEL)
    y_sample = _dense(xs, att_s, rnn_s, dense_w).reshape(db, steps, D_MODEL)

    kv_p = (1, b, s, N_HEADS, HEAD_DIM)
    kv_s = (1, db, steps, N_HEADS, HEAD_DIM)
    return (y_prompt, y_sample,
            k.reshape(kv_p), v.reshape(kv_p), conv_p[None], h_p.reshape(1, b, LRU_WIDTH),
            ks.reshape(kv_s), vs.reshape(kv_s), conv_s[None], h_s[None])
```

```python
import functools
import math

import numpy as np
import jax
import jax.numpy as jnp
from jax import lax
from jax.experimental import pallas as pl
from jax.experimental.pallas import tpu as pltpu

F32 = jnp.float32
BF16 = jnp.bfloat16

D_MODEL = 1024
ATT_WIDTH = 512
LRU_WIDTH = 512
HEAD_DIM = 64
N_HEADS = 8
N_LRU_BLOCKS = 8
LRU_BLOCK = 64
CONV_WIDTH = 4
LRU_C = 8.0
D_FF = 4096
SPAN = 128
DILATIONS = (1, 4, 16)
WIN_MAX = 2048
N_BUCKETS = 32
NORM_EPS = 1e-6
NEG_INF = -1e30
Q_SCALE = HEAD_DIM ** -0.5
LOG2_E = math.log2(math.e)

SUBLANES = 8
LANES = 128
HEADS_PER_TILE = LANES // HEAD_DIM
VMEM_LIMIT = 56 * 1024 * 1024

TOKEN_TILE = 512
PROMPT_MIX_TILE = 1024
PROMPT_SUB_TILE = 256
FF_CHUNK = 1024
SAMPLE_PAD = 128


def _dot(a, b):
    return jnp.dot(a, b, preferred_element_type=F32)


def _dot_nt(a, b):
    return lax.dot_general(a, b, (((1,), (1,)), ((), ())), preferred_element_type=F32)


def _rms(x, g):
    return x * lax.rsqrt(jnp.mean(x * x, axis=-1, keepdims=True) + NORM_EPS) * g


def _const_spec(shape):
    nd = len(shape)
    return pl.BlockSpec(shape, lambda *_: (0,) * nd, pipeline_mode=pl.Buffered(1))


def _sigmoid_of_twice(half_x):
    return 0.5 * jnp.tanh(half_x) + 0.5


def _lru_gates(xc, wg_ref, ba_ref, bx_ref, lam_ref, a_sc, u_sc):
    xcb = xc.astype(BF16)
    lam = lam_ref[...]
    softplus_neg_lam = jnp.maximum(-lam, 0.0) + jnp.log1p(jnp.exp(-jnp.abs(lam)))
    log_a_per_r = (-LRU_C) * softplus_neg_lam
    for p in range(LRU_WIDTH // LANES):
        cols = slice(p * LANES, (p + 1) * LANES)
        g = _dot(xcb[:, cols], wg_ref[p])
        r = _sigmoid_of_twice(g[:, :LANES] + ba_ref[:, cols])
        gi = _sigmoid_of_twice(g[:, LANES:] + bx_ref[:, cols])
        log_a = r * log_a_per_r[:, cols]
        a = jnp.exp(log_a)
        a_sc[:, cols] = a
        u_sc[:, cols] = jnp.sqrt(-jnp.tanh(log_a) * (a * a + 1.0)) * (gi * xc[:, cols])


def _group_scan(a, u, row):
    for s in (1, 2, 4):
        keep = row >= s
        u = jnp.where(keep, a * pltpu.roll(u, s, 0) + u, u)
        a = jnp.where(keep, a * pltpu.roll(a, s, 0), a)
    return a, u


def _prompt_mix_kernel(x_ref, g1_ref, win_ref, cw_ref, cb_ref, wg_ref, ba_ref, bx_ref, lam_ref, grnn_ref,
                       q_ref, k_ref, v_ref, rnn_ref, conv_ref, h_ref,
                       xr_ext, gr_sc, a_sc, u_sc, hs_sc, h_carry):
    t = x_ref.shape[0]
    sub = PROMPT_SUB_TILE
    n_sub = t // sub

    @pl.when(pl.program_id(1) == 0)
    def _():
        xr_ext[0:SUBLANES, :] = jnp.zeros((SUBLANES, LRU_WIDTH), F32)
        h_carry[...] = jnp.zeros_like(h_carry)

    row = lax.broadcasted_iota(jnp.int32, (SUBLANES, LRU_WIDTH), 0)

    def project(s):
        rows = slice(s * sub, (s + 1) * sub)
        n = _rms(x_ref[rows, :], g1_ref[...]).astype(BF16)
        q_ref[rows, :] = _dot(n, win_ref[:, 0:ATT_WIDTH])
        k_ref[rows, :] = _dot(n, win_ref[:, ATT_WIDTH:2 * ATT_WIDTH])
        v_ref[rows, :] = _dot(n, win_ref[:, 2 * ATT_WIDTH:3 * ATT_WIDTH])
        xr_ext[SUBLANES + s * sub:SUBLANES + (s + 1) * sub, :] = _dot(
            n, win_ref[:, 3 * ATT_WIDTH:3 * ATT_WIDTH + LRU_WIDTH])
        gr_sc[rows, :] = _dot(n, win_ref[:, 3 * ATT_WIDTH + LRU_WIDTH:])

    def recur(s, h):
        rows = slice(s * sub, (s + 1) * sub)
        xc = cb_ref[...]
        for back in range(CONV_WIDTH):
            xc = xc + (cw_ref[CONV_WIDTH - 1 - back:CONV_WIDTH - back, :]
                       * xr_ext[pl.ds(SUBLANES + s * sub - back, sub), :])
        _lru_gates(xc, wg_ref, ba_ref, bx_ref, lam_ref, a_sc.at[rows], u_sc.at[rows])
        for g in range(sub // SUBLANES):
            grp = slice(s * sub + g * SUBLANES, s * sub + (g + 1) * SUBLANES)
            a, u = _group_scan(a_sc[grp, :], u_sc[grp, :], row)
            hs = a * h + u
            hs_sc[grp, :] = hs
            h = jnp.broadcast_to(hs[SUBLANES - 1:SUBLANES, :], hs.shape)
        rnn_ref[rows, :] = _rms(hs_sc[rows, :] * jax.nn.gelu(gr_sc[rows, :]), grnn_ref[...]).astype(BF16)
        return h

    h = h_carry[...]
    project(0)
    for s in range(1, n_sub):
        project(s)
        h = recur(s - 1, h)
    h = recur(n_sub - 1, h)
    h_carry[...] = h
    h_ref[...] = h[0:1, :]
    conv_ref[...] = xr_ext[SUBLANES + t - (CONV_WIDTH - 1):SUBLANES + t, :]
    xr_ext[0:SUBLANES, :] = xr_ext[t:t + SUBLANES, :]


def _sample_mix_kernel(x_ref, st_ref, h0_ref, g1_ref, win_ref, cw_ref, cb_ref, wg_ref, ba_ref, bx_ref, lam_ref,
                       grnn_ref, q_ref, k_ref, v_ref, rnn_ref, conv_ref, h_ref,
                       xr_sc, xc_sc, a_sc, u_sc, hs_sc):
    t = x_ref.shape[0]
    n_seq = t // SUBLANES
    n = _rms(x_ref[...], g1_ref[...]).astype(BF16)
    q_ref[...] = _dot(n, win_ref[:, 0:ATT_WIDTH])
    k_ref[...] = _dot(n, win_ref[:, ATT_WIDTH:2 * ATT_WIDTH])
    v_ref[...] = _dot(n, win_ref[:, 2 * ATT_WIDTH:3 * ATT_WIDTH])
    xr_sc[...] = _dot(n, win_ref[:, 3 * ATT_WIDTH:3 * ATT_WIDTH + LRU_WIDTH])

    row = lax.broadcasted_iota(jnp.int32, (SUBLANES, LRU_WIDTH), 0)
    n_state = CONV_WIDTH - 1

    def conv_group(b, carry):
        rows = pl.ds(pl.multiple_of(b * SUBLANES, SUBLANES), SUBLANES)
        xg = xr_sc[rows, :]
        sg = st_ref[rows, :]
        xc = cb_ref[...] + cw_ref[CONV_WIDTH - 1:CONV_WIDTH, :] * xg
        for back in range(1, CONV_WIDTH):
            prev = jnp.where(row >= back, pltpu.roll(xg, back, 0),
                             pltpu.roll(sg, (back - n_state) % SUBLANES, 0))
            xc = xc + cw_ref[CONV_WIDTH - 1 - back:CONV_WIDTH - back, :] * prev
        xc_sc[rows, :] = xc
        conv_ref[b] = xg[SUBLANES - n_state:, :]
        return carry

    lax.fori_loop(0, n_seq, conv_group, 0, unroll=8)

    _lru_gates(xc_sc[...], wg_ref, ba_ref, bx_ref, lam_ref, a_sc, u_sc)

    def scan_group(b, carry):
        rows = pl.ds(pl.multiple_of(b * SUBLANES, SUBLANES), SUBLANES)
        a, u = _group_scan(a_sc[rows, :], u_sc[rows, :], row)
        hs = a * h0_ref[pl.ds(b, 1), :] + u
        hs_sc[rows, :] = hs
        h_ref[pl.ds(b, 1), :] = hs[SUBLANES - 1:SUBLANES, :]
        return carry

    lax.fori_loop(0, n_seq, scan_group, 0, unroll=8)

    gr = _dot(n, win_ref[:, 3 * ATT_WIDTH + LRU_WIDTH:])
    rnn_ref[...] = _rms(hs_sc[...] * jax.nn.gelu(gr), grnn_ref[...]).astype(BF16)


def _mix_weight_specs():
    return [
        _const_spec((1, D_MODEL)),
        _const_spec((D_MODEL, 3 * ATT_WIDTH + 2 * LRU_WIDTH)),
        _const_spec((CONV_WIDTH, LRU_WIDTH)),
        _const_spec((1, LRU_WIDTH)),
        _const_spec((LRU_WIDTH // LANES, LANES, 2 * LANES)),
        _const_spec((1, LRU_WIDTH)),
        _const_spec((1, LRU_WIDTH)),
        _const_spec((1, LRU_WIDTH)),
        _const_spec((1, LRU_WIDTH)),
    ]


def _prompt_mix(x, weights):
    b, s, _ = x.shape
    t = PROMPT_MIX_TILE
    assert s % t == 0 and t % PROMPT_SUB_TILE == 0
    tok = lambda width: pl.BlockSpec((None, t, width), lambda i, j: (i, j, 0))
    per_seq = lambda rows: pl.BlockSpec((None, rows, LRU_WIDTH), lambda i, j: (i, 0, 0))
    return pl.pallas_call(
        _prompt_mix_kernel,
        grid=(b, s // t),
        in_specs=[tok(D_MODEL)] + _mix_weight_specs(),
        out_specs=[tok(ATT_WIDTH), tok(ATT_WIDTH), tok(ATT_WIDTH), tok(LRU_WIDTH),
                   per_seq(CONV_WIDTH - 1), per_seq(1)],
        out_shape=[jax.ShapeDtypeStruct((b, s, ATT_WIDTH), F32)] * 3
        + [jax.ShapeDtypeStruct((b, s, LRU_WIDTH), BF16),
           jax.ShapeDtypeStruct((b, CONV_WIDTH - 1, LRU_WIDTH), F32),
           jax.ShapeDtypeStruct((b, 1, LRU_WIDTH), F32)],
        scratch_shapes=[pltpu.VMEM((t + SUBLANES, LRU_WIDTH), F32)] + [pltpu.VMEM((t, LRU_WIDTH), F32)] * 4
        + [pltpu.VMEM((SUBLANES, LRU_WIDTH), F32)],
        compiler_params=pltpu.CompilerParams(dimension_semantics=("arbitrary", "arbitrary"),
                                             vmem_limit_bytes=VMEM_LIMIT),
        name="prompt_mix",
    )(x, *weights)


def _sample_mix(x, state_rows, h0, weights):
    rows = x.shape[0]
    t = min(TOKEN_TILE, rows)
    n_seq = t // SUBLANES
    tok = lambda width: pl.BlockSpec((t, width), lambda i: (i, 0))
    return pl.pallas_call(
        _sample_mix_kernel,
        grid=(rows // t,),
        in_specs=[tok(D_MODEL), tok(LRU_WIDTH), pl.BlockSpec((n_seq, LRU_WIDTH), lambda i: (i, 0))]
        + _mix_weight_specs(),
        out_specs=[tok(ATT_WIDTH), tok(ATT_WIDTH), tok(ATT_WIDTH), tok(LRU_WIDTH),
                   pl.BlockSpec((n_seq, CONV_WIDTH - 1, LRU_WIDTH), lambda i: (i, 0, 0)),
                   pl.BlockSpec((n_seq, LRU_WIDTH), lambda i: (i, 0))],
        out_shape=[jax.ShapeDtypeStruct((rows, ATT_WIDTH), F32)] * 3
        + [jax.ShapeDtypeStruct((rows, LRU_WIDTH), BF16),
           jax.ShapeDtypeStruct((rows // SUBLANES, CONV_WIDTH - 1, LRU_WIDTH), F32),
           jax.ShapeDtypeStruct((rows // SUBLANES, LRU_WIDTH), F32)],
        scratch_shapes=[pltpu.VMEM((t, LRU_WIDTH), F32)] * 5,
        compiler_params=pltpu.CompilerParams(dimension_semantics=("arbitrary",), vmem_limit_bytes=VMEM_LIMIT),
        name="sample_mix",
    )(x, state_rows, h0, *weights)


def _prompt_attn_kernel(q_ref, k_ref, v_ref, brow_ref, o_ref, tab_sc, qp_sc, kp_sc, vp_sc, pv_sc, m_sc, den_sc):
    s_len = q_ref.shape[0]
    n_branch = len(DILATIONS)
    qi = lax.broadcasted_iota(jnp.int32, (SPAN, SPAN), 0)
    ki = lax.broadcasted_iota(jnp.int32, (SPAN, SPAN), 1)
    low_head = ki < HEAD_DIM

    for g in range(n_branch):
        for h in range(HEADS_PER_TILE):
            rows = slice(h * SPAN, (h + 1) * SPAN)
            prev = pltpu.roll(jnp.broadcast_to(brow_ref[g, h, 0:1, :], (SPAN, SPAN)), 0, 1, stride=1, stride_axis=0)
            cur = pltpu.roll(jnp.broadcast_to(brow_ref[g, h, 1:2, :], (SPAN, SPAN)), 0, 1, stride=1, stride_axis=0)
            tab_sc[g, rows, 0:SPAN] = jnp.where(ki >= qi, prev, NEG_INF)
            tab_sc[g, rows, SPAN:] = jnp.where(ki <= qi, cur, NEG_INF)

    perm = DILATIONS[1]
    sub_len = s_len // perm
    assert DILATIONS == (1, perm, perm * perm) and sub_len % (2 * SPAN) == 0
    natural = (q_ref, k_ref, v_ref)
    permuted = (qp_sc, kp_sc, vp_sc)
    for src, dst in zip(natural, permuted):
        for c in range(perm):
            for i0 in range(0, sub_len, 2 * SPAN):
                dst[c * sub_len + i0:c * sub_len + i0 + 2 * SPAN, :] = src[pl.ds(c + perm * i0, 2 * SPAN, stride=perm), :]

    def attend(branch, refs, q_rows, k_rows, first):
        qr, kr, vr = refs
        n_keys = SPAN if first else 2 * SPAN
        q = qr[q_rows, :]
        q2 = jnp.concatenate([jnp.where(low_head, q, 0.0), jnp.where(low_head, 0.0, q)], axis=0).astype(BF16)
        kk = kr[k_rows, :].astype(BF16)
        vv = vr[k_rows, :].astype(BF16)
        table = tab_sc[branch, :, SPAN:] if first else tab_sc[branch]
        logits = _dot_nt(q2, kk) + table
        m = jnp.max(logits, axis=-1, keepdims=True)
        p32 = jnp.exp2(logits - m)
        den_b = jnp.broadcast_to(jnp.sum(p32, axis=-1, keepdims=True), (2 * SPAN, LANES))
        r = _dot(p32.astype(BF16), vv)
        m_b = jnp.broadcast_to(m, (2 * SPAN, LANES))
        pv_sc[branch, q_rows, :] = jnp.where(low_head, r[:SPAN], r[SPAN:])
        den_sc[branch, q_rows, :] = jnp.where(low_head, den_b[:SPAN], den_b[SPAN:])
        m_sc[branch, q_rows, :] = jnp.where(low_head, m_b[:SPAN], m_b[SPAN:])

    for i in range(s_len // SPAN):
        if i == 0:
            attend(0, natural, pl.ds(0, SPAN), pl.ds(0, SPAN), True)
        else:
            attend(0, natural, pl.ds(i * SPAN, SPAN), pl.ds((i - 1) * SPAN, 2 * SPAN), False)
    for c in range(perm):
        for i in range(sub_len // SPAN):
            q0 = c * sub_len + i * SPAN
            if i == 0:
                attend(1, permuted, pl.ds(q0, SPAN), pl.ds(q0, SPAN), True)
            else:
                attend(1, permuted, pl.ds(q0, SPAN), pl.ds(q0 - SPAN, 2 * SPAN), False)
    assert sub_len // perm == SPAN
    for c in range(perm):
        for c_hi in range(perm):
            rows = pl.ds(c * sub_len + c_hi, SPAN, stride=perm)
            attend(2, permuted, rows, rows, True)

    chunk = 2 * SPAN
    for c in range(perm):
        for i0 in range(0, sub_len, chunk):
            rows = (pl.ds(c + perm * i0, chunk, stride=perm),) + (pl.ds(c * sub_len + i0, chunk),) * (n_branch - 1)
            ms = [m_sc[g, rows[g], :] for g in range(n_branch)]
            m_all = jnp.maximum(jnp.maximum(ms[0], ms[1]), ms[2])
            num = jnp.zeros((chunk, LANES), F32)
            den = jnp.zeros((chunk, LANES), F32)
            for g in range(n_branch):
                w = jnp.exp2(ms[g] - m_all)
                num = num + w * pv_sc[g, rows[g], :]
                den = den + w * den_sc[g, rows[g], :]
            o_ref[rows[0], :] = num / den


def _prompt_attn(q, k, v, bias_rows):
    b, s, _ = q.shape
    n_branch = len(DILATIONS)
    head_pair = pl.BlockSpec((None, s, LANES), lambda i, j: (i, 0, j))
    return pl.pallas_call(
        _prompt_attn_kernel,
        grid=(b, ATT_WIDTH // LANES),
        in_specs=[head_pair, head_pair, head_pair,
                  pl.BlockSpec((n_branch, HEADS_PER_TILE, 2, SPAN), lambda i, j: (0, j, 0, 0))],
        out_specs=head_pair,
        out_shape=jax.ShapeDtypeStruct((b, s, ATT_WIDTH), F32),
        scratch_shapes=[pltpu.VMEM((n_branch, HEADS_PER_TILE * SPAN, 2 * SPAN), F32)]
        + [pltpu.VMEM((s, LANES), F32)] * 3 + [pltpu.VMEM((n_branch, s, LANES), F32)] * 3,
        compiler_params=pltpu.CompilerParams(dimension_semantics=("arbitrary", "arbitrary"),
                                             vmem_limit_bytes=VMEM_LIMIT),
        name="prompt_attn",
    )(q, k, v, bias_rows)


def _sample_attn_kernel(q_ref, kn_ref, vn_ref, ckt_ref, cvt_ref, tc_ref, tn_ref, o_ref):
    t = q_ref.shape[0]
    n_buf = ckt_ref.shape[-1]
    n_branch = len(DILATIONS)
    head_of_lane = lax.broadcasted_iota(jnp.int32, (t, ATT_WIDTH), 1) // HEAD_DIM
    q = q_ref[...]
    q_bd = jnp.concatenate([jnp.where(head_of_lane == h, q, 0.0) for h in range(N_HEADS)], axis=0).astype(BF16)
    pad = jnp.zeros((SAMPLE_PAD - t, ATT_WIDTH), F32)
    k_new = jnp.concatenate([kn_ref[...], pad], axis=0).astype(BF16)
    v_new = jnp.concatenate([vn_ref[...], pad], axis=0).astype(BF16)

    kt = ckt_ref[...].reshape(ATT_WIDTH, n_buf).astype(BF16)
    half = n_buf // 2
    s_c = jnp.concatenate([_dot(q_bd, kt[:, :half]), _dot(q_bd, kt[:, half:])], axis=1)
    s_n = _dot_nt(q_bd, k_new)
    m = None
    for g in range(n_branch):
        m_g = jnp.maximum(jnp.max(s_c + tc_ref[g], axis=-1, keepdims=True),
                          jnp.max(s_n + tn_ref[g], axis=-1, keepdims=True))
        m = m_g if m is None else jnp.maximum(m, m_g)
    p_c = jnp.exp2(s_c + tc_ref[0] - m)
    p_n = jnp.exp2(s_n + tn_ref[0] - m)
    for g in range(1, n_branch):
        p_c = p_c + jnp.exp2(s_c + tc_ref[g] - m)
        p_n = p_n + jnp.exp2(s_n + tn_ref[g] - m)
    den = jnp.sum(p_c, axis=-1, keepdims=True) + jnp.sum(p_n, axis=-1, keepdims=True)
    vt = cvt_ref[...].reshape(ATT_WIDTH, n_buf).astype(BF16)
    p_cb = p_c.astype(BF16)
    out = (_dot_nt(p_cb[:, :half], vt[:, :half]) + _dot_nt(p_cb[:, half:], vt[:, half:])
           + _dot(p_n.astype(BF16), v_new)) / den
    att = jnp.zeros((t, ATT_WIDTH), F32)
    for h in range(N_HEADS):
        att = att + jnp.where(head_of_lane == h, out[h * t:(h + 1) * t, :], 0.0)
    o_ref[...] = att


def _sample_attn_specs(n_buf, steps, cached_tables, new_tables):
    tok = pl.BlockSpec((steps, ATT_WIDTH), lambda i: (i, 0))
    cache = pl.BlockSpec((None, N_HEADS, HEAD_DIM, n_buf), lambda i: (i, 0, 0, 0))
    return [tok, tok, tok, cache, cache, _const_spec(cached_tables.shape), _const_spec(new_tables.shape)], tok


def _dense_kernel(x_ref, att_ref, rnn_ref, gatt_ref, woa_ref, wor_ref, g2_ref, w1_ref, w2_ref, gf_ref, y_ref):
    att_n = _rms(att_ref[...], gatt_ref[...]).astype(BF16)
    x1 = x_ref[...] + _dot(att_n, woa_ref[...]) + _dot(rnn_ref[...], wor_ref[...])
    n2 = _rms(x1, g2_ref[...]).astype(BF16)
    mlp = None
    for c in range(D_FF // FF_CHUNK):
        cols = slice(c * FF_CHUNK, (c + 1) * FF_CHUNK)
        hmid = jnp.maximum(_dot(n2, w1_ref[:, cols]), 0.0)
        part = _dot((hmid * hmid).astype(BF16), w2_ref[cols, :])
        mlp = part if mlp is None else mlp + part
    y_ref[...] = _rms(x1 + mlp, gf_ref[...])


N_DENSE_IN = 10


def _dense_and_sample_attn_kernel(*refs):
    dense_in, attn_in = refs[:N_DENSE_IN], refs[N_DENSE_IN:-2]
    y_ref, att_s_ref = refs[-2:]
    _sample_attn_kernel(*attn_in, att_s_ref)
    _dense_kernel(*dense_in, y_ref)


def _dense_specs(t):
    tok = lambda width: pl.BlockSpec((t, width), lambda i: (i, 0))
    in_specs = [tok(D_MODEL), tok(ATT_WIDTH), tok(LRU_WIDTH),
                _const_spec((1, ATT_WIDTH)), _const_spec((ATT_WIDTH, D_MODEL)), _const_spec((LRU_WIDTH, D_MODEL)),
                _const_spec((1, D_MODEL)), _const_spec((D_MODEL, D_FF)), _const_spec((D_FF, D_MODEL)),
                _const_spec((1, D_MODEL))]
    assert len(in_specs) == N_DENSE_IN
    return in_specs, tok(D_MODEL)


def _dense(x, att, rnn, weights):
    rows = x.shape[0]
    t = min(TOKEN_TILE, rows)
    in_specs, out_spec = _dense_specs(t)
    return pl.pallas_call(
        _dense_kernel,
        grid=(rows // t,),
        in_specs=in_specs,
        out_specs=out_spec,
        out_shape=jax.ShapeDtypeStruct((rows, D_MODEL), F32),
        compiler_params=pltpu.CompilerParams(dimension_semantics=("arbitrary",), vmem_limit_bytes=VMEM_LIMIT),
        name="dense_tail",
    )(x, att, rnn, *weights)


def _dense_and_sample_attn(x, att, rnn, weights, q_s, k_new, v_new, cache_kt, cache_vt, cached_tables, new_tables,
                           steps):
    rows = x.shape[0]
    n_seq = q_s.shape[0] // steps
    t = rows // n_seq
    assert t * n_seq == rows and t % SUBLANES == 0
    dense_in, dense_out = _dense_specs(t)
    attn_in, attn_out = _sample_attn_specs(cache_kt.shape[-1], steps, cached_tables, new_tables)
    return pl.pallas_call(
        _dense_and_sample_attn_kernel,
        grid=(n_seq,),
        in_specs=dense_in + attn_in,
        out_specs=[dense_out, attn_out],
        out_shape=[jax.ShapeDtypeStruct((rows, D_MODEL), F32), jax.ShapeDtypeStruct(q_s.shape, F32)],
        compiler_params=pltpu.CompilerParams(dimension_semantics=("arbitrary",), vmem_limit_bytes=VMEM_LIMIT),
        name="dense_tail_sample_attn",
    )(x, att, rnn, *weights, q_s, k_new, v_new, cache_kt, cache_vt, cached_tables, new_tables)


def _t5_bucket(dist):
    max_exact = N_BUCKETS // 2
    d_f = jnp.maximum(dist, max_exact).astype(F32)
    large = max_exact + (jnp.log(d_f / max_exact) / math.log(WIN_MAX / max_exact)
                         * (N_BUCKETS - max_exact)).astype(jnp.int32)
    large = jnp.minimum(large, N_BUCKETS - 1)
    return jnp.where(dist < max_exact, dist, large)


def _branch_bias(rel_bias, dil):
    dist = jnp.arange(SPAN + 1, dtype=jnp.int32) * dil
    return rel_bias[_t5_bucket(dist)].astype(F32).T * LOG2_E


def _prompt_bias_rows(rel_bias):
    rows = []
    for dil in DILATIONS:
        bias = _branch_bias(rel_bias, dil)
        prev = bias[:, SPAN:0:-1]
        cur = jnp.concatenate([bias[:, 0:1], bias[:, SPAN - 1:0:-1]], axis=1)
        rows.append(jnp.stack([prev, cur], axis=1))
    return jnp.stack(rows)


def _sample_bias_tables(rel_bias, n_buf, steps):
    step0, new = [], []
    new_row = np.arange(SAMPLE_PAD)
    delta_new = np.arange(steps)[:, None] - new_row[None, :]
    for dil in DILATIONS:
        bias = _branch_bias(rel_bias, dil)
        every = [bias[:, SPAN:0:-1]] + [jnp.full((N_HEADS, SPAN), NEG_INF, F32)] * (dil - 1)
        step0.append(jnp.concatenate([jnp.full((N_HEADS, n_buf - SPAN * dil), NEG_INF, F32),
                                      jnp.stack(every, axis=-1).reshape(N_HEADS, SPAN * dil)], axis=1))
        valid = (new_row[None, :] < steps) & (delta_new >= 0) & (delta_new % dil == 0)
        j_of = np.where(valid, delta_new // dil, -1)
        one_hot = (j_of[None] == np.arange(steps)[:, None, None]).astype(np.float32)
        vals = jnp.sum(bias[:, :steps, None, None] * one_hot[None], axis=1)
        new.append(jnp.where(valid[None], vals, NEG_INF).reshape(N_HEADS * steps, SAMPLE_PAD))
    n_branch = len(DILATIONS)
    period = jnp.concatenate([jnp.stack(step0), jnp.full((n_branch, N_HEADS, 1), NEG_INF, F32)], axis=-1)
    skewed = jnp.tile(period, (1, 1, steps))[..., :steps * n_buf].reshape(n_branch, N_HEADS, steps, n_buf)
    reachable = np.arange(n_buf)[None, :] >= np.arange(steps)[:, None]
    cached = jnp.where(reachable, skewed, NEG_INF).reshape(n_branch, N_HEADS * steps, n_buf)
    return cached, jnp.stack(new)


def _paired_gate_weights(gate_a_w, gate_x_w):
    def pair_diag(w):
        z = jnp.zeros((LRU_BLOCK, LRU_BLOCK), w.dtype)
        return jnp.stack([jnp.block([[w[2 * p], z], [z, w[2 * p + 1]]]) for p in range(N_LRU_BLOCKS // 2)])
    return jnp.concatenate([pair_diag(gate_a_w), pair_diag(gate_x_w)], axis=-1).astype(BF16)


def kernel(x_prompt, x_sample, cache_k, cache_v, state_conv, state_h, norm1_g, w_in, rel_bias, conv_w, conv_b,
           gate_a_w, gate_a_b, gate_x_w, gate_x_b, lru_lambda, att_out_g, rnn_out_g, w_out, norm2_g, w_mlp_in,
           w_mlp_out, final_g):
    depth = w_in.shape[0]
    assert depth == 1, "the final norm is fused into the single layer's dense kernel"
    b, s, _ = x_prompt.shape
    db, steps, _ = x_sample.shape
    n_buf = cache_k.shape[2]
    assert steps == SUBLANES and s == WIN_MAX and n_buf == WIN_MAX
    l = 0
    row = lambda v: v.reshape(1, -1).astype(F32)

    w_in_scaled = jnp.concatenate([w_in[l, :, :ATT_WIDTH] * (Q_SCALE * LOG2_E), w_in[l, :, ATT_WIDTH:]], axis=1)
    mix_w = (row(norm1_g[l]), w_in_scaled.astype(BF16), conv_w[l], row(conv_b[l]),
             _paired_gate_weights(0.5 * gate_a_w[l], 0.5 * gate_x_w[l]), 0.5 * row(gate_a_b[l]), 0.5 * row(gate_x_b[l]),
             row(lru_lambda[l]), row(rnn_out_g[l]))
    dense_w = (row(att_out_g[l]), w_out[l, :ATT_WIDTH].astype(BF16), w_out[l, ATT_WIDTH:].astype(BF16),
               row(norm2_g[l]), w_mlp_in[l].astype(BF16), w_mlp_out[l].astype(BF16), row(final_g))

    xs = x_sample.reshape(db * steps, D_MODEL)
    state_rows = jnp.pad(state_conv[l], ((0, 0), (0, SUBLANES - (CONV_WIDTH - 1)), (0, 0))).reshape(db * steps, LRU_WIDTH)
    qs, ks, vs, rnn_s, conv_s, h_s = _sample_mix(xs, state_rows, state_h[l], mix_w)
    cache_kt = jnp.transpose(cache_k[l], (0, 2, 3, 1))
    cache_vt = jnp.transpose(cache_v[l], (0, 2, 3, 1))

    q, k, v, rnn, conv_p, h_p = _prompt_mix(x_prompt, mix_w)
    att = _prompt_attn(q, k, v, _prompt_bias_rows(rel_bias))
    y_prompt, att_s = _dense_and_sample_attn(
        x_prompt.reshape(b * s, D_MODEL), att.reshape(b * s, ATT_WIDTH), rnn.reshape(b * s, LRU_WIDTH), dense_w,
        qs, ks, vs, cache_kt, cache_vt, *_sample_bias_tables(rel_bias, n_buf, steps), steps)
    y_prompt = y_prompt.reshape(b, s, D_MODEL)
    y_sample = _dense(xs, att_s, rnn_s, dense_w).reshape(db, steps, D_MODEL)

    kv_p = (1, b, s, N_HEADS, HEAD_DIM)
    kv_s = (1, db, steps, N_HEADS, HEAD_DIM)
    return (y_prompt, y_sample,
            k.reshape(kv_p), v.reshape(kv_p), conv_p[None], h_p.reshape(1, b, LRU_WIDTH),
            ks.reshape(kv_s), vs.reshape(kv_s), conv_s[None], h_s[None])
```

```python
import functools
import math

import numpy as np
import jax
import jax.numpy as jnp
from jax import lax
from jax.experimental import pallas as pl
from jax.experimental.pallas import tpu as pltpu

F32 = jnp.float32
BF16 = jnp.bfloat16

D_MODEL = 1024
ATT_WIDTH = 512
LRU_WIDTH = 512
HEAD_DIM = 64
N_HEADS = 8
N_LRU_BLOCKS = 8
LRU_BLOCK = 64
CONV_WIDTH = 4
LRU_C = 8.0
D_FF = 4096
SPAN = 128
DILATIONS = (1, 4, 16)
WIN_MAX = 2048
N_BUCKETS = 32
NORM_EPS = 1e-6
NEG_INF = -1e30
Q_SCALE = HEAD_DIM ** -0.5
LOG2_E = math.log2(math.e)

SUBLANES = 8
LANES = 128
HEADS_PER_TILE = LANES // HEAD_DIM
VMEM_LIMIT = 56 * 1024 * 1024

TOKEN_TILE = 512
PROMPT_MIX_TILE = 1024
PROMPT_SUB_TILE = 256
FF_CHUNK = 1024
SAMPLE_PAD = 128


def _dot(a, b):
    return jnp.dot(a, b, preferred_element_type=F32)


def _dot_nt(a, b):
    return lax.dot_general(a, b, (((1,), (1,)), ((), ())), preferred_element_type=F32)


def _rms(x, g):
    return x * lax.rsqrt(jnp.mean(x * x, axis=-1, keepdims=True) + NORM_EPS) * g


def _const_spec(shape):
    nd = len(shape)
    return pl.BlockSpec(shape, lambda *_: (0,) * nd, pipeline_mode=pl.Buffered(1))


def _sigmoid_of_twice(half_x):
    return 0.5 * jnp.tanh(half_x) + 0.5


def _lru_gates(xc, wg_ref, ba_ref, bx_ref, lam_ref, a_sc, u_sc):
    xcb = xc.astype(BF16)
    lam = lam_ref[...]
    softplus_neg_lam = jnp.maximum(-lam, 0.0) + jnp.log1p(jnp.exp(-jnp.abs(lam)))
    log_a_per_r = (-LRU_C) * softplus_neg_lam
    for p in range(LRU_WIDTH // LANES):
        cols = slice(p * LANES, (p + 1) * LANES)
        g = _dot(xcb[:, cols], wg_ref[p])
        r = _sigmoid_of_twice(g[:, :LANES] + ba_ref[:, cols])
        gi = _sigmoid_of_twice(g[:, LANES:] + bx_ref[:, cols])
        log_a = r * log_a_per_r[:, cols]
        a = jnp.exp(log_a)
        a_sc[:, cols] = a
        u_sc[:, cols] = jnp.sqrt(-jnp.tanh(log_a) * (a * a + 1.0)) * (gi * xc[:, cols])


def _group_scan(a, u, row):
    for s in (1, 2, 4):
        keep = row >= s
        u = jnp.where(keep, a * pltpu.roll(u, s, 0) + u, u)
        a = jnp.where(keep, a * pltpu.roll(a, s, 0), a)
    return a, u


def _prompt_mix_kernel(x_ref, g1_ref, win_ref, cw_ref, cb_ref, wg_ref, ba_ref, bx_ref, lam_ref, grnn_ref,
                       q_ref, k_ref, v_ref, rnn_ref, conv_ref, h_ref,
                       xr_ext, gr_sc, a_sc, u_sc, hs_sc, h_carry):
    t = x_ref.shape[0]
    sub = PROMPT_SUB_TILE
    n_sub = t // sub

    @pl.when(pl.program_id(1) == 0)
    def _():
        xr_ext[0:SUBLANES, :] = jnp.zeros((SUBLANES, LRU_WIDTH), F32)
        h_carry[...] = jnp.zeros_like(h_carry)

    row = lax.broadcasted_iota(jnp.int32, (SUBLANES, LRU_WIDTH), 0)

    def project(s):
        rows = slice(s * sub, (s + 1) * sub)
        n = _rms(x_ref[rows, :], g1_ref[...]).astype(BF16)
        q_ref[rows, :] = _dot(n, win_ref[:, 0:ATT_WIDTH])
        k_ref[rows, :] = _dot(n, win_ref[:, ATT_WIDTH:2 * ATT_WIDTH])
        v_ref[rows, :] = _dot(n, win_ref[:, 2 * ATT_WIDTH:3 * ATT_WIDTH])
        xr_ext[SUBLANES + s * sub:SUBLANES + (s + 1) * sub, :] = _dot(
            n, win_ref[:, 3 * ATT_WIDTH:3 * ATT_WIDTH + LRU_WIDTH])
        gr_sc[rows, :] = _dot(n, win_ref[:, 3 * ATT_WIDTH + LRU_WIDTH:])

    def recur(s, h):
        rows = slice(s * sub, (s + 1) * sub)
        xc = cb_ref[...]
        for back in range(CONV_WIDTH):
            xc = xc + (cw_ref[CONV_WIDTH - 1 - back:CONV_WIDTH - back, :]
                       * xr_ext[pl.ds(SUBLANES + s * sub - back, sub), :])
        _lru_gates(xc, wg_ref, ba_ref, bx_ref, lam_ref, a_sc.at[rows], u_sc.at[rows])
        for g in range(sub // SUBLANES):
            grp = slice(s * sub + g * SUBLANES, s * sub + (g + 1) * SUBLANES)
            a, u = _group_scan(a_sc[grp, :], u_sc[grp, :], row)
            hs = a * h + u
            hs_sc[grp, :] = hs
            h = jnp.broadcast_to(hs[SUBLANES - 1:SUBLANES, :], hs.shape)
        rnn_ref[rows, :] = _rms(hs_sc[rows, :] * jax.nn.gelu(gr_sc[rows, :]), grnn_ref[...]).astype(BF16)
        return h

    h = h_carry[...]
    project(0)
    for s in range(1, n_sub):
        project(s)
        h = recur(s - 1, h)
    h = recur(n_sub - 1, h)
    h_carry[...] = h
    h_ref[...] = h[0:1, :]
    conv_ref[...] = xr_ext[SUBLANES + t - (CONV_WIDTH - 1):SUBLANES + t, :]
    xr_ext[0:SUBLANES, :] = xr_ext[t:t + SUBLANES, :]


def _sample_mix_kernel(x_ref, st_ref, h0_ref, g1_ref, win_ref, cw_ref, cb_ref, wg_ref, ba_ref, bx_ref, lam_ref,
                       grnn_ref, q_ref, k_ref, v_ref, rnn_ref, conv_ref, h_ref,
                       xr_sc, xc_sc, a_sc, u_sc, hs_sc):
    t = x_ref.shape[0]
    n_seq = t // SUBLANES
    n = _rms(x_ref[...], g1_ref[...]).astype(BF16)
    q_ref[...] = _dot(n, win_ref[:, 0:ATT_WIDTH])
    k_ref[...] = _dot(n, win_ref[:, ATT_WIDTH:2 * ATT_WIDTH])
    v_ref[...] = _dot(n, win_ref[:, 2 * ATT_WIDTH:3 * ATT_WIDTH])
    xr_sc[...] = _dot(n, win_ref[:, 3 * ATT_WIDTH:3 * ATT_WIDTH + LRU_WIDTH])

    row = lax.broadcasted_iota(jnp.int32, (SUBLANES, LRU_WIDTH), 0)
    n_state = CONV_WIDTH - 1

    def conv_group(b, carry):
        rows = pl.ds(pl.multiple_of(b * SUBLANES, SUBLANES), SUBLANES)
        xg = xr_sc[rows, :]
        sg = st_ref[rows, :]
        xc = cb_ref[...] + cw_ref[CONV_WIDTH - 1:CONV_WIDTH, :] * xg
        for back in range(1, CONV_WIDTH):
            prev = jnp.where(row >= back, pltpu.roll(xg, back, 0),
                             pltpu.roll(sg, (back - n_state) % SUBLANES, 0))
            xc = xc + cw_ref[CONV_WIDTH - 1 - back:CONV_WIDTH - back, :] * prev
        xc_sc[rows, :] = xc
        conv_ref[b] = xg[SUBLANES - n_state:, :]
        return carry

    lax.fori_loop(0, n_seq, conv_group, 0, unroll=8)

    _lru_gates(xc_sc[...], wg_ref, ba_ref, bx_ref, lam_ref, a_sc, u_sc)

    def scan_group(b, carry):
        rows = pl.ds(pl.multiple_of(b * SUBLANES, SUBLANES), SUBLANES)
        a, u = _group_scan(a_sc[rows, :], u_sc[rows, :], row)
        hs = a * h0_ref[pl.ds(b, 1), :] + u
        hs_sc[rows, :] = hs
        h_ref[pl.ds(b, 1), :] = hs[SUBLANES - 1:SUBLANES, :]
        return carry

    lax.fori_loop(0, n_seq, scan_group, 0, unroll=8)

    gr = _dot(n, win_ref[:, 3 * ATT_WIDTH + LRU_WIDTH:])
    rnn_ref[...] = _rms(hs_sc[...] * jax.nn.gelu(gr), grnn_ref[...]).astype(BF16)


def _mix_weight_specs():
    return [
        _const_spec((1, D_MODEL)),
        _const_spec((D_MODEL, 3 * ATT_WIDTH + 2 * LRU_WIDTH)),
        _const_spec((CONV_WIDTH, LRU_WIDTH)),
        _const_spec((1, LRU_WIDTH)),
        _const_spec((LRU_WIDTH // LANES, LANES, 2 * LANES)),
        _const_spec((1, LRU_WIDTH)),
        _const_spec((1, LRU_WIDTH)),
        _const_spec((1, LRU_WIDTH)),
        _const_spec((1, LRU_WIDTH)),
    ]


def _prompt_mix(x, weights):
    b, s, _ = x.shape
    t = PROMPT_MIX_TILE
    assert s % t == 0 and t % PROMPT_SUB_TILE == 0
    tok = lambda width: pl.BlockSpec((None, t, width), lambda i, j: (i, j, 0))
    per_seq = lambda rows: pl.BlockSpec((None, rows, LRU_WIDTH), lambda i, j: (i, 0, 0))
    return pl.pallas_call(
        _prompt_mix_kernel,
        grid=(b, s // t),
        in_specs=[tok(D_MODEL)] + _mix_weight_specs(),
        out_specs=[tok(ATT_WIDTH), tok(ATT_WIDTH), tok(ATT_WIDTH), tok(LRU_WIDTH),
                   per_seq(CONV_WIDTH - 1), per_seq(1)],
        out_shape=[jax.ShapeDtypeStruct((b, s, ATT_WIDTH), F32)] * 3
        + [jax.ShapeDtypeStruct((b, s, LRU_WIDTH), BF16),
           jax.ShapeDtypeStruct((b, CONV_WIDTH - 1, LRU_WIDTH), F32),
           jax.ShapeDtypeStruct((b, 1, LRU_WIDTH), F32)],
        scratch_shapes=[pltpu.VMEM((t + SUBLANES, LRU_WIDTH), F32)] + [pltpu.VMEM((t, LRU_WIDTH), F32)] * 4
        + [pltpu.VMEM((SUBLANES, LRU_WIDTH), F32)],
        compiler_params=pltpu.CompilerParams(dimension_semantics=("arbitrary", "arbitrary"),
                                             vmem_limit_bytes=VMEM_LIMIT),
        name="prompt_mix",
    )(x, *weights)


def _sample_mix(x, state_rows, h0, weights):
    rows = x.shape[0]
    t = min(TOKEN_TILE, rows)
    n_seq = t // SUBLANES
    tok = lambda width: pl.BlockSpec((t, width), lambda i: (i, 0))
    return pl.pallas_call(
        _sample_mix_kernel,
        grid=(rows // t,),
        in_specs=[tok(D_MODEL), tok(LRU_WIDTH), pl.BlockSpec((n_seq, LRU_WIDTH), lambda i: (i, 0))]
        + _mix_weight_specs(),
        out_specs=[tok(ATT_WIDTH), tok(ATT_WIDTH), tok(ATT_WIDTH), tok(LRU_WIDTH),
                   pl.BlockSpec((n_seq, CONV_WIDTH - 1, LRU_WIDTH), lambda i: (i, 0, 0)),
                   pl.BlockSpec((n_seq, LRU_WIDTH), lambda i: (i, 0))],
        out_shape=[jax.ShapeDtypeStruct((rows, ATT_WIDTH), F32)] * 3
        + [jax.ShapeDtypeStruct((rows, LRU_WIDTH), BF16),
           jax.ShapeDtypeStruct((rows // SUBLANES, CONV_WIDTH - 1, LRU_WIDTH), F32),
           jax.ShapeDtypeStruct((rows // SUBLANES, LRU_WIDTH), F32)],
        scratch_shapes=[pltpu.VMEM((t, LRU_WIDTH), F32)] * 5,
        compiler_params=pltpu.CompilerParams(dimension_semantics=("arbitrary",), vmem_limit_bytes=VMEM_LIMIT),
        name="sample_mix",
    )(x, state_rows, h0, *weights)


def _prompt_attn_kernel(q_ref, k_ref, v_ref, brow_ref, o_ref, tab_sc, qp_sc, kp_sc, vp_sc, pv_sc, m_sc, den_sc):
    s_len = q_ref.shape[0]
    n_branch = len(DILATIONS)
    qi = lax.broadcasted_iota(jnp.int32, (SPAN, SPAN), 0)
    ki = lax.broadcasted_iota(jnp.int32, (SPAN, SPAN), 1)
    low_head = ki < HEAD_DIM

    for g in range(n_branch):
        for h in range(HEADS_PER_TILE):
            rows = slice(h * SPAN, (h + 1) * SPAN)
            prev = pltpu.roll(jnp.broadcast_to(brow_ref[g, h, 0:1, :], (SPAN, SPAN)), 0, 1, stride=1, stride_axis=0)
            cur = pltpu.roll(jnp.broadcast_to(brow_ref[g, h, 1:2, :], (SPAN, SPAN)), 0, 1, stride=1, stride_axis=0)
            tab_sc[g, rows, 0:SPAN] = jnp.where(ki >= qi, prev, NEG_INF)
            tab_sc[g, rows, SPAN:] = jnp.where(ki <= qi, cur, NEG_INF)

    perm = DILATIONS[1]
    sub_len = s_len // perm
    assert DILATIONS == (1, perm, perm * perm) and sub_len % (2 * SPAN) == 0
    natural = (q_ref, k_ref, v_ref)
    permuted = (qp_sc, kp_sc, vp_sc)
    for src, dst in zip(natural, permuted):
        for c in range(perm):
            for i0 in range(0, sub_len, 2 * SPAN):
                dst[c * sub_len + i0:c * sub_len + i0 + 2 * SPAN, :] = src[pl.ds(c + perm * i0, 2 * SPAN, stride=perm), :]

    def attend(branch, refs, q_rows, k_rows, first):
        qr, kr, vr = refs
        n_keys = SPAN if first else 2 * SPAN
        q = qr[q_rows, :]
        q2 = jnp.concatenate([jnp.where(low_head, q, 0.0), jnp.where(low_head, 0.0, q)], axis=0).astype(BF16)
        kk = kr[k_rows, :].astype(BF16)
        vv = vr[k_rows, :].astype(BF16)
        table = tab_sc[branch, :, SPAN:] if first else tab_sc[branch]
        logits = _dot_nt(q2, kk) + table
        m = jnp.max(logits, axis=-1, keepdims=True)
        p32 = jnp.exp2(logits - m)
        den_b = jnp.broadcast_to(jnp.sum(p32, axis=-1, keepdims=True), (2 * SPAN, LANES))
        r = _dot(p32.astype(BF16), vv)
        m_b = jnp.broadcast_to(m, (2 * SPAN, LANES))
        pv_sc[branch, q_rows, :] = jnp.where(low_head, r[:SPAN], r[SPAN:])
        den_sc[branch, q_rows, :] = jnp.where(low_head, den_b[:SPAN], den_b[SPAN:])
        m_sc[branch, q_rows, :] = jnp.where(low_head, m_b[:SPAN], m_b[SPAN:])

    for i in range(s_len // SPAN):
        if i == 0:
            attend(0, natural, pl.ds(0, SPAN), pl.ds(0, SPAN), True)
        else:
            attend(0, natural, pl.ds(i * SPAN, SPAN), pl.ds((i - 1) * SPAN, 2 * SPAN), False)
    for c in range(perm):
        for i in range(sub_len // SPAN):
            q0 = c * sub_len + i * SPAN
            if i == 0:
                attend(1, permuted, pl.ds(q0, SPAN), pl.ds(q0, SPAN), True)
            else:
                attend(1, permuted, pl.ds(q0, SPAN), pl.ds(q0 - SPAN, 2 * SPAN), False)
    assert sub_len // perm == SPAN
    for c in range(perm):
        for c_hi in range(perm):
            rows = pl.ds(c * sub_len + c_hi, SPAN, stride=perm)
            attend(2, permuted, rows, rows, True)

    chunk = 2 * SPAN
    for c in range(perm):
        for i0 in range(0, sub_len, chunk):
            rows = (pl.ds(c + perm * i0, chunk, stride=perm),) + (pl.ds(c * sub_len + i0, chunk),) * (n_branch - 1)
            ms = [m_sc[g, rows[g], :] for g in range(n_branch)]
            m_all = jnp.maximum(jnp.maximum(ms[0], ms[1]), ms[2])
            num = jnp.zeros((chunk, LANES), F32)
            den = jnp.zeros((chunk, LANES), F32)
            for g in range(n_branch):
                w = jnp.exp2(ms[g] - m_all)
                num = num + w * pv_sc[g, rows[g], :]
                den = den + w * den_sc[g, rows[g], :]
            o_ref[rows[0], :] = num / den


def _prompt_attn(q, k, v, bias_rows):
    b, s, _ = q.shape
    n_branch = len(DILATIONS)
    head_pair = pl.BlockSpec((None, s, LANES), lambda i, j: (i, 0, j))
    return pl.pallas_call(
        _prompt_attn_kernel,
        grid=(b, ATT_WIDTH // LANES),
        in_specs=[head_pair, head_pair, head_pair,
                  pl.BlockSpec((n_branch, HEADS_PER_TILE, 2, SPAN), lambda i, j: (0, j, 0, 0))],
        out_specs=head_pair,
        out_shape=jax.ShapeDtypeStruct((b, s, ATT_WIDTH), F32),
        scratch_shapes=[pltpu.VMEM((n_branch, HEADS_PER_TILE * SPAN, 2 * SPAN), F32)]
        + [pltpu.VMEM((s, LANES), F32)] * 3 + [pltpu.VMEM((n_branch, s, LANES), F32)] * 3,
        compiler_params=pltpu.CompilerParams(dimension_semantics=("arbitrary", "arbitrary"),
                                             vmem_limit_bytes=VMEM_LIMIT),
        name="prompt_attn",
    )(q, k, v, bias_rows)


def _sample_attn_kernel(q_ref, kn_ref, vn_ref, ckt_ref, cvt_ref, tc_ref, tn_ref, o_ref):
    t = q_ref.shape[0]
    n_buf = ckt_ref.shape[-1]
    n_branch = len(DILATIONS)
    head_of_lane = lax.broadcasted_iota(jnp.int32, (t, ATT_WIDTH), 1) // HEAD_DIM
    q = q_ref[...]
    q_bd = jnp.concatenate([jnp.where(head_of_lane == h, q, 0.0) for h in range(N_HEADS)], axis=0).astype(BF16)
    pad = jnp.zeros((SAMPLE_PAD - t, ATT_WIDTH), F32)
    k_new = jnp.concatenate([kn_ref[...], pad], axis=0).astype(BF16)
    v_new = jnp.concatenate([vn_ref[...], pad], axis=0).astype(BF16)

    kt = ckt_ref[...].reshape(ATT_WIDTH, n_buf).astype(BF16)
    half = n_buf // 2
    s_c = jnp.concatenate([_dot(q_bd, kt[:, :half]), _dot(q_bd, kt[:, half:])], axis=1)
    s_n = _dot_nt(q_bd, k_new)
    m = None
    for g in range(n_branch):
        m_g = jnp.maximum(jnp.max(s_c + tc_ref[g], axis=-1, keepdims=True),
                          jnp.max(s_n + tn_ref[g], axis=-1, keepdims=True))
        m = m_g if m is None else jnp.maximum(m, m_g)
    p_c = jnp.exp2(s_c + tc_ref[0] - m)
    p_n = jnp.exp2(s_n + tn_ref[0] - m)
    for g in range(1, n_branch):
        p_c = p_c + jnp.exp2(s_c + tc_ref[g] - m)
        p_n = p_n + jnp.exp2(s_n + tn_ref[g] - m)
    den = jnp.sum(p_c, axis=-1, keepdims=True) + jnp.sum(p_n, axis=-1, keepdims=True)
    vt = cvt_ref[...].reshape(ATT_WIDTH, n_buf).astype(BF16)
    p_cb = p_c.astype(BF16)
    out = (_dot_nt(p_cb[:, :half], vt[:, :half]) + _dot_nt(p_cb[:, half:], vt[:, half:])
           + _dot(p_n.astype(BF16), v_new)) / den
    att = jnp.zeros((t, ATT_WIDTH), F32)
    for h in range(N_HEADS):
        att = att + jnp.where(head_of_lane == h, out[h * t:(h + 1) * t, :], 0.0)
    o_ref[...] = att


def _sample_attn_specs(n_buf, steps, cached_tables, new_tables):
    tok = pl.BlockSpec((steps, ATT_WIDTH), lambda i: (i, 0))
    cache = pl.BlockSpec((None, N_HEADS, HEAD_DIM, n_buf), lambda i: (i, 0, 0, 0))
    return [tok, tok, tok, cache, cache, _const_spec(cached_tables.shape), _const_spec(new_tables.shape)], tok


def _dense_kernel(x_ref, att_ref, rnn_ref, gatt_ref, woa_ref, wor_ref, g2_ref, w1_ref, w2_ref, gf_ref, y_ref):
    att_n = _rms(att_ref[...], gatt_ref[...]).astype(BF16)
    x1 = x_ref[...] + _dot(att_n, woa_ref[...]) + _dot(rnn_ref[...], wor_ref[...])
    n2 = _rms(x1, g2_ref[...]).astype(BF16)
    mlp = None
    for c in range(D_FF // FF_CHUNK):
        cols = slice(c * FF_CHUNK, (c + 1) * FF_CHUNK)
        hmid = jnp.maximum(_dot(n2, w1_ref[:, cols]), 0.0)
        part = _dot((hmid * hmid).astype(BF16), w2_ref[cols, :])
        mlp = part if mlp is None else mlp + part
    y_ref[...] = _rms(x1 + mlp, gf_ref[...])


N_DENSE_IN = 10


def _dense_and_sample_attn_kernel(*refs):
    dense_in, attn_in = refs[:N_DENSE_IN], refs[N_DENSE_IN:-2]
    y_ref, att_s_ref = refs[-2:]
    _sample_attn_kernel(*attn_in, att_s_ref)
    _dense_kernel(*dense_in, y_ref)


def _dense_specs(t):
    tok = lambda width: pl.BlockSpec((t, width), lambda i: (i, 0))
    in_specs = [tok(D_MODEL), tok(ATT_WIDTH), tok(LRU_WIDTH),
                _const_spec((1, ATT_WIDTH)), _const_spec((ATT_WIDTH, D_MODEL)), _const_spec((LRU_WIDTH, D_MODEL)),
                _const_spec((1, D_MODEL)), _const_spec((D_MODEL, D_FF)), _const_spec((D_FF, D_MODEL)),
                _const_spec((1, D_MODEL))]
    assert len(in_specs) == N_DENSE_IN
    return in_specs, tok(D_MODEL)


def _dense(x, att, rnn, weights):
    rows = x.shape[0]
    t = min(TOKEN_TILE, rows)
    in_specs, out_spec = _dense_specs(t)
    return pl.pallas_call(
        _dense_kernel,
        grid=(rows // t,),
        in_specs=in_specs,
        out_specs=out_spec,
        out_shape=jax.ShapeDtypeStruct((rows, D_MODEL), F32),
        compiler_params=pltpu.CompilerParams(dimension_semantics=("arbitrary",), vmem_limit_bytes=VMEM_LIMIT),
        name="dense_tail",
    )(x, att, rnn, *weights)


def _dense_and_sample_attn(x, att, rnn, weights, q_s, k_new, v_new, cache_kt, cache_vt, cached_tables, new_tables,
                           steps):
    rows = x.shape[0]
    n_seq = q_s.shape[0] // steps
    t = rows // n_seq
    assert t * n_seq == rows and t % SUBLANES == 0
    dense_in, dense_out = _dense_specs(t)
    attn_in, attn_out = _sample_attn_specs(cache_kt.shape[-1], steps, cached_tables, new_tables)
    return pl.pallas_call(
        _dense_and_sample_attn_kernel,
        grid=(n_seq,),
        in_specs=dense_in + attn_in,
        out_specs=[dense_out, attn_out],
        out_shape=[jax.ShapeDtypeStruct((rows, D_MODEL), F32), jax.ShapeDtypeStruct(q_s.shape, F32)],
        compiler_params=pltpu.CompilerParams(dimension_semantics=("arbitrary",), vmem_limit_bytes=VMEM_LIMIT),
        name="dense_tail_sample_attn",
    )(x, att, rnn, *weights, q_s, k_new, v_new, cache_kt, cache_vt, cached_tables, new_tables)


def _t5_bucket(dist):
    max_exact = N_BUCKETS // 2
    d_f = jnp.maximum(dist, max_exact).astype(F32)
    large = max_exact + (jnp.log(d_f / max_exact) / math.log(WIN_MAX / max_exact)
                         * (N_BUCKETS - max_exact)).astype(jnp.int32)
    large = jnp.minimum(large, N_BUCKETS - 1)
    return jnp.where(dist < max_exact, dist, large)


def _branch_bias(rel_bias, dil):
    dist = jnp.arange(SPAN + 1, dtype=jnp.int32) * dil
    return rel_bias[_t5_bucket(dist)].astype(F32).T * LOG2_E


def _prompt_bias_rows(rel_bias):
    rows = []
    for dil in DILATIONS:
        bias = _branch_bias(rel_bias, dil)
        prev = bias[:, SPAN:0:-1]
        cur = jnp.concatenate([bias[:, 0:1], bias[:, SPAN - 1:0:-1]], axis=1)
        rows.append(jnp.stack([prev, cur], axis=1))
    return jnp.stack(rows)


def _sample_bias_tables(rel_bias, n_buf, steps):
    step0, new = [], []
    new_row = np.arange(SAMPLE_PAD)
    delta_new = np.arange(steps)[:, None] - new_row[None, :]
    for dil in DILATIONS:
        bias = _branch_bias(rel_bias, dil)
        every = [bias[:, SPAN:0:-1]] + [jnp.full((N_HEADS, SPAN), NEG_INF, F32)] * (dil - 1)
        step0.append(jnp.concatenate([jnp.full((N_HEADS, n_buf - SPAN * dil), NEG_INF, F32),
                                      jnp.stack(every, axis=-1).reshape(N_HEADS, SPAN * dil)], axis=1))
        valid = (new_row[None, :] < steps) & (delta_new >= 0) & (delta_new % dil == 0)
        j_of = np.where(valid, delta_new // dil, -1)
        one_hot = (j_of[None] == np.arange(steps)[:, None, None]).astype(np.float32)
        vals = jnp.sum(bias[:, :steps, None, None] * one_hot[None], axis=1)
        new.append(jnp.where(valid[None], vals, NEG_INF).reshape(N_HEADS * steps, SAMPLE_PAD))
    n_branch = len(DILATIONS)
    period = jnp.concatenate([jnp.stack(step0), jnp.full((n_branch, N_HEADS, 1), NEG_INF, F32)], axis=-1)
    skewed = jnp.tile(period, (1, 1, steps))[..., :steps * n_buf].reshape(n_branch, N_HEADS, steps, n_buf)
    reachable = np.arange(n_buf)[None, :] >= np.arange(steps)[:, None]
    cached = jnp.where(reachable, skewed, NEG_INF).reshape(n_branch, N_HEADS * steps, n_buf)
    return cached, jnp.stack(new)


def _paired_gate_weights(gate_a_w, gate_x_w):
    def pair_diag(w):
        z = jnp.zeros((LRU_BLOCK, LRU_BLOCK), w.dtype)
        return jnp.stack([jnp.block([[w[2 * p], z], [z, w[2 * p + 1]]]) for p in range(N_LRU_BLOCKS // 2)])
    return jnp.concatenate([pair_diag(gate_a_w), pair_diag(gate_x_w)], axis=-1).astype(BF16)


def kernel(x_prompt, x_sample, cache_k, cache_v, state_conv, state_h, norm1_g, w_in, rel_bias, conv_w, conv_b,
           gate_a_w, gate_a_b, gate_x_w, gate_x_b, lru_lambda, att_out_g, rnn_out_g, w_out, norm2_g, w_mlp_in,
           w_mlp_out, final_g):
    depth = w_in.shape[0]
    assert depth == 1, "the final norm is fused into the single layer's dense kernel"
    b, s, _ = x_prompt.shape
    db, steps, _ = x_sample.shape
    n_buf = cache_k.shape[2]
    assert steps == SUBLANES and s == WIN_MAX and n_buf == WIN_MAX
    l = 0
    row = lambda v: v.reshape(1, -1).astype(F32)

    col_scale = np.where(np.arange(w_in.shape[-1]) < ATT_WIDTH, Q_SCALE * LOG2_E, 1.0).astype(np.float32)
    w_in_scaled = w_in[l] * col_scale[None, :]
    mix_w = (row(norm1_g[l]), w_in_scaled.astype(BF16), conv_w[l], row(conv_b[l]),
             _paired_gate_weights(0.5 * gate_a_w[l], 0.5 * gate_x_w[l]), 0.5 * row(gate_a_b[l]), 0.5 * row(gate_x_b[l]),
             row(lru_lambda[l]), row(rnn_out_g[l]))
    dense_w = (row(att_out_g[l]), w_out[l, :ATT_WIDTH].astype(BF16), w_out[l, ATT_WIDTH:].astype(BF16),
               row(norm2_g[l]), w_mlp_in[l].astype(BF16), w_mlp_out[l].astype(BF16), row(final_g))

    xs = x_sample.reshape(db * steps, D_MODEL)
    state_rows = jnp.pad(state_conv[l], ((0, 0), (0, SUBLANES - (CONV_WIDTH - 1)), (0, 0))).reshape(db * steps, LRU_WIDTH)
    qs, ks, vs, rnn_s, conv_s, h_s = _sample_mix(xs, state_rows, state_h[l], mix_w)
    cache_kt = jnp.transpose(cache_k[l], (0, 2, 3, 1))
    cache_vt = jnp.transpose(cache_v[l], (0, 2, 3, 1))

    q, k, v, rnn, conv_p, h_p = _prompt_mix(x_prompt, mix_w)
    att = _prompt_attn(q, k, v, _prompt_bias_rows(rel_bias))
    y_prompt, att_s = _dense_and_sample_attn(
        x_prompt.reshape(b * s, D_MODEL), att.reshape(b * s, ATT_WIDTH), rnn.reshape(b * s, LRU_WIDTH), dense_w,
        qs, ks, vs, cache_kt, cache_vt, *_sample_bias_tables(rel_bias, n_buf, steps), steps)
    y_prompt = y_prompt.reshape(b, s, D_MODEL)
    y_sample = _dense(xs, att_s, rnn_s, dense_w).reshape(db, steps, D_MODEL)

    kv_p = (1, b, s, N_HEADS, HEAD_DIM)
    kv_s = (1, db, steps, N_HEADS, HEAD_DIM)
    return (y_prompt, y_sample,
            k.reshape(kv_p), v.reshape(kv_p), conv_p[None], h_p.reshape(1, b, LRU_WIDTH),
            ks.reshape(kv_s), vs.reshape(kv_s), conv_s[None], h_s[None])
```

```python
import functools
import math

import numpy as np
import jax
import jax.numpy as jnp
from jax import lax
from jax.experimental import pallas as pl
from jax.experimental.pallas import tpu as pltpu

F32 = jnp.float32
BF16 = jnp.bfloat16

D_MODEL = 1024
ATT_WIDTH = 512
LRU_WIDTH = 512
HEAD_DIM = 64
N_HEADS = 8
N_LRU_BLOCKS = 8
LRU_BLOCK = 64
CONV_WIDTH = 4
LRU_C = 8.0
D_FF = 4096
SPAN = 128
DILATIONS = (1, 4, 16)
WIN_MAX = 2048
N_BUCKETS = 32
NORM_EPS = 1e-6
NEG_INF = -1e30
Q_SCALE = HEAD_DIM ** -0.5
LOG2_E = math.log2(math.e)

SUBLANES = 8
LANES = 128
HEADS_PER_TILE = LANES // HEAD_DIM
VMEM_LIMIT = 56 * 1024 * 1024

TOKEN_TILE = 512
PROMPT_MIX_TILE = 1024
PROMPT_SUB_TILE = 256
FF_CHUNK = 1024
SAMPLE_PAD = 128


def _dot(a, b):
    return jnp.dot(a, b, preferred_element_type=F32)


def _dot_nt(a, b):
    return lax.dot_general(a, b, (((1,), (1,)), ((), ())), preferred_element_type=F32)


def _rms(x, g):
    return x * lax.rsqrt(jnp.mean(x * x, axis=-1, keepdims=True) + NORM_EPS) * g


def _const_spec(shape):
    nd = len(shape)
    return pl.BlockSpec(shape, lambda *_: (0,) * nd, pipeline_mode=pl.Buffered(1))


def _sigmoid_of_twice(half_x):
    return 0.5 * jnp.tanh(half_x) + 0.5


def _lru_gates(xc, wg_ref, ba_ref, bx_ref, lam_ref, a_sc, u_sc):
    xcb = xc.astype(BF16)
    lam = lam_ref[...]
    softplus_neg_lam = jnp.maximum(-lam, 0.0) + jnp.log1p(jnp.exp(-jnp.abs(lam)))
    log_a_per_r = (-LRU_C) * softplus_neg_lam
    for p in range(LRU_WIDTH // LANES):
        cols = slice(p * LANES, (p + 1) * LANES)
        g = _dot(xcb[:, cols], wg_ref[p])
        r = _sigmoid_of_twice(g[:, :LANES] + ba_ref[:, cols])
        gi = _sigmoid_of_twice(g[:, LANES:] + bx_ref[:, cols])
        log_a = r * log_a_per_r[:, cols]
        a = jnp.exp(log_a)
        a_sc[:, cols] = a
        u_sc[:, cols] = jnp.sqrt(-jnp.tanh(log_a) * (a * a + 1.0)) * (gi * xc[:, cols])


def _group_scan(a, u, row):
    for s in (1, 2, 4):
        keep = row >= s
        u = jnp.where(keep, a * pltpu.roll(u, s, 0) + u, u)
        a = jnp.where(keep, a * pltpu.roll(a, s, 0), a)
    return a, u


def _prompt_mix_kernel(x_ref, g1_ref, win_ref, cw_ref, cb_ref, wg_ref, ba_ref, bx_ref, lam_ref, grnn_ref,
                       q_ref, k_ref, v_ref, rnn_ref, conv_ref, h_ref,
                       xr_ext, gr_sc, a_sc, u_sc, hs_sc, h_carry):
    t = x_ref.shape[0]
    sub = PROMPT_SUB_TILE
    n_sub = t // sub

    @pl.when(pl.program_id(1) == 0)
    def _():
        xr_ext[0:SUBLANES, :] = jnp.zeros((SUBLANES, LRU_WIDTH), F32)
        h_carry[...] = jnp.zeros_like(h_carry)

    row = lax.broadcasted_iota(jnp.int32, (SUBLANES, LRU_WIDTH), 0)

    def project(s):
        rows = slice(s * sub, (s + 1) * sub)
        n = _rms(x_ref[rows, :], g1_ref[...]).astype(BF16)
        q_ref[rows, :] = _dot(n, win_ref[:, 0:ATT_WIDTH])
        k_ref[rows, :] = _dot(n, win_ref[:, ATT_WIDTH:2 * ATT_WIDTH])
        v_ref[rows, :] = _dot(n, win_ref[:, 2 * ATT_WIDTH:3 * ATT_WIDTH])
        xr_ext[SUBLANES + s * sub:SUBLANES + (s + 1) * sub, :] = _dot(
            n, win_ref[:, 3 * ATT_WIDTH:3 * ATT_WIDTH + LRU_WIDTH])
        gr_sc[rows, :] = _dot(n, win_ref[:, 3 * ATT_WIDTH + LRU_WIDTH:])

    def recur(s, h):
        rows = slice(s * sub, (s + 1) * sub)
        xc = cb_ref[...]
        for back in range(CONV_WIDTH):
            xc = xc + (cw_ref[CONV_WIDTH - 1 - back:CONV_WIDTH - back, :]
                       * xr_ext[pl.ds(SUBLANES + s * sub - back, sub), :])
        _lru_gates(xc, wg_ref, ba_ref, bx_ref, lam_ref, a_sc.at[rows], u_sc.at[rows])
        for g in range(sub // SUBLANES):
            grp = slice(s * sub + g * SUBLANES, s * sub + (g + 1) * SUBLANES)
            a, u = _group_scan(a_sc[grp, :], u_sc[grp, :], row)
            hs = a * h + u
            hs_sc[grp, :] = hs
            h = jnp.broadcast_to(hs[SUBLANES - 1:SUBLANES, :], hs.shape)
        rnn_ref[rows, :] = _rms(hs_sc[rows, :] * jax.nn.gelu(gr_sc[rows, :]), grnn_ref[...]).astype(BF16)
        return h

    h = h_carry[...]
    project(0)
    for s in range(1, n_sub):
        project(s)
        h = recur(s - 1, h)
    h = recur(n_sub - 1, h)
    h_carry[...] = h
    h_ref[...] = h[0:1, :]
    conv_ref[...] = xr_ext[SUBLANES + t - (CONV_WIDTH - 1):SUBLANES + t, :]
    xr_ext[0:SUBLANES, :] = xr_ext[t:t + SUBLANES, :]


def _sample_mix_kernel(x_ref, st_ref, h0_ref, g1_ref, win_ref, cw_ref, cb_ref, wg_ref, ba_ref, bx_ref, lam_ref,
                       grnn_ref, q_ref, k_ref, v_ref, rnn_ref, conv_ref, h_ref,
                       xr_sc, xc_sc, a_sc, u_sc, hs_sc):
    t = x_ref.shape[0]
    n_seq = t // SUBLANES
    n = _rms(x_ref[...], g1_ref[...]).astype(BF16)
    q_ref[...] = _dot(n, win_ref[:, 0:ATT_WIDTH])
    k_ref[...] = _dot(n, win_ref[:, ATT_WIDTH:2 * ATT_WIDTH])
    v_ref[...] = _dot(n, win_ref[:, 2 * ATT_WIDTH:3 * ATT_WIDTH])
    xr_sc[...] = _dot(n, win_ref[:, 3 * ATT_WIDTH:3 * ATT_WIDTH + LRU_WIDTH])

    row = lax.broadcasted_iota(jnp.int32, (SUBLANES, LRU_WIDTH), 0)
    n_state = CONV_WIDTH - 1

    def conv_group(b, carry):
        rows = pl.ds(pl.multiple_of(b * SUBLANES, SUBLANES), SUBLANES)
        xg = xr_sc[rows, :]
        sg = st_ref[rows, :]
        xc = cb_ref[...] + cw_ref[CONV_WIDTH - 1:CONV_WIDTH, :] * xg
        for back in range(1, CONV_WIDTH):
            prev = jnp.where(row >= back, pltpu.roll(xg, back, 0),
                             pltpu.roll(sg, (back - n_state) % SUBLANES, 0))
            xc = xc + cw_ref[CONV_WIDTH - 1 - back:CONV_WIDTH - back, :] * prev
        xc_sc[rows, :] = xc
        conv_ref[b] = xg[SUBLANES - n_state:, :]
        return carry

    lax.fori_loop(0, n_seq, conv_group, 0, unroll=8)

    _lru_gates(xc_sc[...], wg_ref, ba_ref, bx_ref, lam_ref, a_sc, u_sc)

    def scan_group(b, carry):
        rows = pl.ds(pl.multiple_of(b * SUBLANES, SUBLANES), SUBLANES)
        a, u = _group_scan(a_sc[rows, :], u_sc[rows, :], row)
        hs = a * h0_ref[pl.ds(b, 1), :] + u
        hs_sc[rows, :] = hs
        h_ref[pl.ds(b, 1), :] = hs[SUBLANES - 1:SUBLANES, :]
        return carry

    lax.fori_loop(0, n_seq, scan_group, 0, unroll=8)

    gr = _dot(n, win_ref[:, 3 * ATT_WIDTH + LRU_WIDTH:])
    rnn_ref[...] = _rms(hs_sc[...] * jax.nn.gelu(gr), grnn_ref[...]).astype(BF16)


def _mix_weight_specs():
    return [
        _const_spec((1, D_MODEL)),
        _const_spec((D_MODEL, 3 * ATT_WIDTH + 2 * LRU_WIDTH)),
        _const_spec((CONV_WIDTH, LRU_WIDTH)),
        _const_spec((1, LRU_WIDTH)),
        _const_spec((LRU_WIDTH // LANES, LANES, 2 * LANES)),
        _const_spec((1, LRU_WIDTH)),
        _const_spec((1, LRU_WIDTH)),
        _const_spec((1, LRU_WIDTH)),
        _const_spec((1, LRU_WIDTH)),
    ]


def _prompt_mix(x, weights):
    b, s, _ = x.shape
    t = PROMPT_MIX_TILE
    assert s % t == 0 and t % PROMPT_SUB_TILE == 0
    tok = lambda width: pl.BlockSpec((None, t, width), lambda i, j: (i, j, 0))
    per_seq = lambda rows: pl.BlockSpec((None, rows, LRU_WIDTH), lambda i, j: (i, 0, 0))
    return pl.pallas_call(
        _prompt_mix_kernel,
        grid=(b, s // t),
        in_specs=[tok(D_MODEL)] + _mix_weight_specs(),
        out_specs=[tok(ATT_WIDTH), tok(ATT_WIDTH), tok(ATT_WIDTH), tok(LRU_WIDTH),
                   per_seq(CONV_WIDTH - 1), per_seq(1)],
        out_shape=[jax.ShapeDtypeStruct((b, s, ATT_WIDTH), F32)] * 3
        + [jax.ShapeDtypeStruct((b, s, LRU_WIDTH), BF16),
           jax.ShapeDtypeStruct((b, CONV_WIDTH - 1, LRU_WIDTH), F32),
           jax.ShapeDtypeStruct((b, 1, LRU_WIDTH), F32)],
        scratch_shapes=[pltpu.VMEM((t + SUBLANES, LRU_WIDTH), F32)] + [pltpu.VMEM((t, LRU_WIDTH), F32)] * 4
        + [pltpu.VMEM((SUBLANES, LRU_WIDTH), F32)],
        compiler_params=pltpu.CompilerParams(dimension_semantics=("arbitrary", "arbitrary"),
                                             vmem_limit_bytes=VMEM_LIMIT),
        name="prompt_mix",
    )(x, *weights)


def _sample_mix(x, state_rows, h0, weights):
    rows = x.shape[0]
    t = min(TOKEN_TILE, rows)
    n_seq = t // SUBLANES
    tok = lambda width: pl.BlockSpec((t, width), lambda i: (i, 0))
    return pl.pallas_call(
        _sample_mix_kernel,
        grid=(rows // t,),
        in_specs=[tok(D_MODEL), tok(LRU_WIDTH), pl.BlockSpec((n_seq, LRU_WIDTH), lambda i: (i, 0))]
        + _mix_weight_specs(),
        out_specs=[tok(ATT_WIDTH), tok(ATT_WIDTH), tok(ATT_WIDTH), tok(LRU_WIDTH),
                   pl.BlockSpec((n_seq, CONV_WIDTH - 1, LRU_WIDTH), lambda i: (i, 0, 0)),
                   pl.BlockSpec((n_seq, LRU_WIDTH), lambda i: (i, 0))],
        out_shape=[jax.ShapeDtypeStruct((rows, ATT_WIDTH), F32)] * 3
        + [jax.ShapeDtypeStruct((rows, LRU_WIDTH), BF16),
           jax.ShapeDtypeStruct((rows // SUBLANES, CONV_WIDTH - 1, LRU_WIDTH), F32),
           jax.ShapeDtypeStruct((rows // SUBLANES, LRU_WIDTH), F32)],
        scratch_shapes=[pltpu.VMEM((t, LRU_WIDTH), F32)] * 5,
        compiler_params=pltpu.CompilerParams(dimension_semantics=("arbitrary",), vmem_limit_bytes=VMEM_LIMIT),
        name="sample_mix",
    )(x, state_rows, h0, *weights)


def _prompt_attn_kernel(q_ref, k_ref, v_ref, brow_ref, o_ref, tab_sc, qp_sc, kp_sc, vp_sc, pv_sc, m_sc, den_sc):
    s_len = q_ref.shape[0]
    n_branch = len(DILATIONS)
    qi = lax.broadcasted_iota(jnp.int32, (SPAN, SPAN), 0)
    ki = lax.broadcasted_iota(jnp.int32, (SPAN, SPAN), 1)
    low_head = ki < HEAD_DIM

    for g in range(n_branch):
        for h in range(HEADS_PER_TILE):
            rows = slice(h * SPAN, (h + 1) * SPAN)
            prev = pltpu.roll(jnp.broadcast_to(brow_ref[g, h, 0:1, :], (SPAN, SPAN)), 0, 1, stride=1, stride_axis=0)
            cur = pltpu.roll(jnp.broadcast_to(brow_ref[g, h, 1:2, :], (SPAN, SPAN)), 0, 1, stride=1, stride_axis=0)
            tab_sc[g, rows, 0:SPAN] = jnp.where(ki >= qi, prev, NEG_INF)
            tab_sc[g, rows, SPAN:] = jnp.where(ki <= qi, cur, NEG_INF)

    perm = DILATIONS[1]
    sub_len = s_len // perm
    assert DILATIONS == (1, perm, perm * perm) and sub_len % (2 * SPAN) == 0
    natural = (q_ref, k_ref, v_ref)
    permuted = (qp_sc, kp_sc, vp_sc)
    for src, dst in zip(natural, permuted):
        for c in range(perm):
            for i0 in range(0, sub_len, 2 * SPAN):
                dst[c * sub_len + i0:c * sub_len + i0 + 2 * SPAN, :] = src[pl.ds(c + perm * i0, 2 * SPAN, stride=perm), :]

    def attend(branch, refs, q_rows, k_rows, first):
        qr, kr, vr = refs
        n_keys = SPAN if first else 2 * SPAN
        q = qr[q_rows, :]
        q2 = jnp.concatenate([jnp.where(low_head, q, 0.0), jnp.where(low_head, 0.0, q)], axis=0).astype(BF16)
        kk = kr[k_rows, :].astype(BF16)
        vv = vr[k_rows, :].astype(BF16)
        table = tab_sc[branch, :, SPAN:] if first else tab_sc[branch]
        logits = _dot_nt(q2, kk) + table
        m = jnp.max(logits, axis=-1, keepdims=True)
        p32 = jnp.exp2(logits - m)
        den_b = jnp.broadcast_to(jnp.sum(p32, axis=-1, keepdims=True), (2 * SPAN, LANES))
        r = _dot(p32.astype(BF16), vv)
        m_b = jnp.broadcast_to(m, (2 * SPAN, LANES))
        pv_sc[branch, q_rows, :] = jnp.where(low_head, r[:SPAN], r[SPAN:])
        den_sc[branch, q_rows, :] = jnp.where(low_head, den_b[:SPAN], den_b[SPAN:])
        m_sc[branch, q_rows, :] = jnp.where(low_head, m_b[:SPAN], m_b[SPAN:])

    for i in range(s_len // SPAN):
        if i == 0:
            attend(0, natural, pl.ds(0, SPAN), pl.ds(0, SPAN), True)
        else:
            attend(0, natural, pl.ds(i * SPAN, SPAN), pl.ds((i - 1) * SPAN, 2 * SPAN), False)
    for c in range(perm):
        for i in range(sub_len // SPAN):
            q0 = c * sub_len + i * SPAN
            if i == 0:
                attend(1, permuted, pl.ds(q0, SPAN), pl.ds(q0, SPAN), True)
            else:
                attend(1, permuted, pl.ds(q0, SPAN), pl.ds(q0 - SPAN, 2 * SPAN), False)
    assert sub_len // perm == SPAN
    for c in range(perm):
        for c_hi in range(perm):
            rows = pl.ds(c * sub_len + c_hi, SPAN, stride=perm)
            attend(2, permuted, rows, rows, True)

    chunk = 2 * SPAN
    for c in range(perm):
        for i0 in range(0, sub_len, chunk):
            rows = (pl.ds(c + perm * i0, chunk, stride=perm),) + (pl.ds(c * sub_len + i0, chunk),) * (n_branch - 1)
            ms = [m_sc[g, rows[g], :] for g in range(n_branch)]
            m_all = jnp.maximum(jnp.maximum(ms[0], ms[1]), ms[2])
            num = jnp.zeros((chunk, LANES), F32)
            den = jnp.zeros((chunk, LANES), F32)
            for g in range(n_branch):
                w = jnp.exp2(ms[g] - m_all)
                num = num + w * pv_sc[g, rows[g], :]
                den = den + w * den_sc[g, rows[g], :]
            o_ref[rows[0], :] = num / den


def _prompt_attn(q, k, v, bias_rows):
    b, s, _ = q.shape
    n_branch = len(DILATIONS)
    head_pair = pl.BlockSpec((None, s, LANES), lambda i, j: (i, 0, j))
    return pl.pallas_call(
        _prompt_attn_kernel,
        grid=(b, ATT_WIDTH // LANES),
        in_specs=[head_pair, head_pair, head_pair,
                  pl.BlockSpec((n_branch, HEADS_PER_TILE, 2, SPAN), lambda i, j: (0, j, 0, 0))],
        out_specs=head_pair,
        out_shape=jax.ShapeDtypeStruct((b, s, ATT_WIDTH), F32),
        scratch_shapes=[pltpu.VMEM((n_branch, HEADS_PER_TILE * SPAN, 2 * SPAN), F32)]
        + [pltpu.VMEM((s, LANES), F32)] * 3 + [pltpu.VMEM((n_branch, s, LANES), F32)] * 3,
        compiler_params=pltpu.CompilerParams(dimension_semantics=("arbitrary", "arbitrary"),
                                             vmem_limit_bytes=VMEM_LIMIT),
        name="prompt_attn",
    )(q, k, v, bias_rows)


def _sample_attn_kernel(q_ref, kn_ref, vn_ref, ckt_ref, cvt_ref, tc_ref, tn_ref, o_ref):
    t = q_ref.shape[0]
    n_buf = ckt_ref.shape[-1]
    n_branch = len(DILATIONS)
    head_of_lane = lax.broadcasted_iota(jnp.int32, (t, ATT_WIDTH), 1) // HEAD_DIM
    q = q_ref[...]
    q_bd = jnp.concatenate([jnp.where(head_of_lane == h, q, 0.0) for h in range(N_HEADS)], axis=0).astype(BF16)
    pad = jnp.zeros((SAMPLE_PAD - t, ATT_WIDTH), F32)
    k_new = jnp.concatenate([kn_ref[...], pad], axis=0).astype(BF16)
    v_new = jnp.concatenate([vn_ref[...], pad], axis=0).astype(BF16)

    kt = ckt_ref[...].reshape(ATT_WIDTH, n_buf).astype(BF16)
    half = n_buf // 2
    s_c = jnp.concatenate([_dot(q_bd, kt[:, :half]), _dot(q_bd, kt[:, half:])], axis=1)
    s_n = _dot_nt(q_bd, k_new)
    m = None
    for g in range(n_branch):
        m_g = jnp.maximum(jnp.max(s_c + tc_ref[g], axis=-1, keepdims=True),
                          jnp.max(s_n + tn_ref[g], axis=-1, keepdims=True))
        m = m_g if m is None else jnp.maximum(m, m_g)
    p_c = jnp.exp2(s_c + tc_ref[0] - m)
    p_n = jnp.exp2(s_n + tn_ref[0] - m)
    for g in range(1, n_branch):
        p_c = p_c + jnp.exp2(s_c + tc_ref[g] - m)
        p_n = p_n + jnp.exp2(s_n + tn_ref[g] - m)
    den = jnp.sum(p_c, axis=-1, keepdims=True) + jnp.sum(p_n, axis=-1, keepdims=True)
    vt = cvt_ref[...].reshape(ATT_WIDTH, n_buf).astype(BF16)
    p_cb = p_c.astype(BF16)
    out = (_dot_nt(p_cb[:, :half], vt[:, :half]) + _dot_nt(p_cb[:, half:], vt[:, half:])
           + _dot(p_n.astype(BF16), v_new)) / den
    att = jnp.zeros((t, ATT_WIDTH), F32)
    for h in range(N_HEADS):
        att = att + jnp.where(head_of_lane == h, out[h * t:(h + 1) * t, :], 0.0)
    o_ref[...] = att


def _sample_attn_specs(n_buf, steps, cached_tables, new_tables):
    tok = pl.BlockSpec((steps, ATT_WIDTH), lambda i: (i, 0))
    cache = pl.BlockSpec((None, N_HEADS, HEAD_DIM, n_buf), lambda i: (i, 0, 0, 0))
    return [tok, tok, tok, cache, cache, _const_spec(cached_tables.shape), _const_spec(new_tables.shape)], tok


def _dense_kernel(x_ref, att_ref, rnn_ref, gatt_ref, woa_ref, wor_ref, g2_ref, w1_ref, w2_ref, gf_ref, y_ref):
    att_n = _rms(att_ref[...], gatt_ref[...]).astype(BF16)
    x1 = x_ref[...] + _dot(att_n, woa_ref[...]) + _dot(rnn_ref[...], wor_ref[...])
    n2 = _rms(x1, g2_ref[...]).astype(BF16)
    mlp = None
    for c in range(D_FF // FF_CHUNK):
        cols = slice(c * FF_CHUNK, (c + 1) * FF_CHUNK)
        hmid = jnp.maximum(_dot(n2, w1_ref[:, cols]), 0.0)
        part = _dot((hmid * hmid).astype(BF16), w2_ref[cols, :])
        mlp = part if mlp is None else mlp + part
    y_ref[...] = _rms(x1 + mlp, gf_ref[...])


N_DENSE_IN = 10


def _dense_and_sample_attn_kernel(*refs):
    dense_in, attn_in = refs[:N_DENSE_IN], refs[N_DENSE_IN:-2]
    y_ref, att_s_ref = refs[-2:]
    _sample_attn_kernel(*attn_in, att_s_ref)
    _dense_kernel(*dense_in, y_ref)


def _dense_specs(t):
    tok = lambda width: pl.BlockSpec((t, width), lambda i: (i, 0))
    in_specs = [tok(D_MODEL), tok(ATT_WIDTH), tok(LRU_WIDTH),
                _const_spec((1, ATT_WIDTH)), _const_spec((ATT_WIDTH, D_MODEL)), _const_spec((LRU_WIDTH, D_MODEL)),
                _const_spec((1, D_MODEL)), _const_spec((D_MODEL, D_FF)), _const_spec((D_FF, D_MODEL)),
                _const_spec((1, D_MODEL))]
    assert len(in_specs) == N_DENSE_IN
    return in_specs, tok(D_MODEL)


def _dense(x, att, rnn, weights):
    rows = x.shape[0]
    t = min(TOKEN_TILE, rows)
    in_specs, out_spec = _dense_specs(t)
    return pl.pallas_call(
        _dense_kernel,
        grid=(rows // t,),
        in_specs=in_specs,
        out_specs=out_spec,
        out_shape=jax.ShapeDtypeStruct((rows, D_MODEL), F32),
        compiler_params=pltpu.CompilerParams(dimension_semantics=("arbitrary",), vmem_limit_bytes=VMEM_LIMIT),
        name="dense_tail",
    )(x, att, rnn, *weights)


def _dense_and_sample_attn(x, att, rnn, weights, q_s, k_new, v_new, cache_kt, cache_vt, cached_tables, new_tables,
                           steps):
    rows = x.shape[0]
    n_seq = q_s.shape[0] // steps
    t = rows // n_seq
    assert t * n_seq == rows and t % SUBLANES == 0
    dense_in, dense_out = _dense_specs(t)
    attn_in, attn_out = _sample_attn_specs(cache_kt.shape[-1], steps, cached_tables, new_tables)
    return pl.pallas_call(
        _dense_and_sample_attn_kernel,
        grid=(n_seq,),
        in_specs=dense_in + attn_in,
        out_specs=[dense_out, attn_out],
        out_shape=[jax.ShapeDtypeStruct((rows, D_MODEL), F32), jax.ShapeDtypeStruct(q_s.shape, F32)],
        compiler_params=pltpu.CompilerParams(dimension_semantics=("arbitrary",), vmem_limit_bytes=VMEM_LIMIT),
        name="dense_tail_sample_attn",
    )(x, att, rnn, *weights, q_s, k_new, v_new, cache_kt, cache_vt, cached_tables, new_tables)


def _t5_bucket(dist):
    max_exact = N_BUCKETS // 2
    d_f = jnp.maximum(dist, max_exact).astype(F32)
    large = max_exact + (jnp.log(d_f / max_exact) / math.log(WIN_MAX / max_exact)
                         * (N_BUCKETS - max_exact)).astype(jnp.int32)
    large = jnp.minimum(large, N_BUCKETS - 1)
    return jnp.where(dist < max_exact, dist, large)


def _branch_bias(rel_bias, dil):
    dist = jnp.arange(SPAN + 1, dtype=jnp.int32) * dil
    return rel_bias[_t5_bucket(dist)].astype(F32).T * LOG2_E


def _prompt_bias_rows(rel_bias):
    rows = []
    for dil in DILATIONS:
        bias = _branch_bias(rel_bias, dil)
        prev = bias[:, SPAN:0:-1]
        cur = jnp.concatenate([bias[:, 0:1], bias[:, SPAN - 1:0:-1]], axis=1)
        rows.append(jnp.stack([prev, cur], axis=1))
    return jnp.stack(rows)


def _sample_bias_tables(rel_bias, n_buf, steps):
    step0, new = [], []
    new_row = np.arange(SAMPLE_PAD)
    delta_new = np.arange(steps)[:, None] - new_row[None, :]
    for dil in DILATIONS:
        bias = _branch_bias(rel_bias, dil)
        every = [bias[:, SPAN:0:-1]] + [jnp.full((N_HEADS, SPAN), NEG_INF, F32)] * (dil - 1)
        step0.append(jnp.concatenate([jnp.full((N_HEADS, n_buf - SPAN * dil), NEG_INF, F32),
                                      jnp.stack(every, axis=-1).reshape(N_HEADS, SPAN * dil)], axis=1))
        valid = (new_row[None, :] < steps) & (delta_new >= 0) & (delta_new % dil == 0)
        j_of = np.where(valid, delta_new // dil, -1)
        one_hot = (j_of[None] == np.arange(steps)[:, None, None]).astype(np.float32)
        vals = jnp.sum(bias[:, :steps, None, None] * one_hot[None], axis=1)
        new.append(jnp.where(valid[None], vals, NEG_INF).reshape(N_HEADS * steps, SAMPLE_PAD))
    n_branch = len(DILATIONS)
    period = jnp.concatenate([jnp.stack(step0), jnp.full((n_branch, N_HEADS, 1), NEG_INF, F32)], axis=-1)
    skewed = jnp.tile(period, (1, 1, steps))[..., :steps * n_buf].reshape(n_branch, N_HEADS, steps, n_buf)
    reachable = np.arange(n_buf)[None, :] >= np.arange(steps)[:, None]
    cached = jnp.where(reachable, skewed, NEG_INF).reshape(n_branch, N_HEADS * steps, n_buf)
    return cached, jnp.stack(new)


def _paired_gate_weights(gate_a_w, gate_x_w):
    def pair_diag(w):
        z = jnp.zeros((LRU_BLOCK, LRU_BLOCK), w.dtype)
        return jnp.stack([jnp.block([[w[2 * p], z], [z, w[2 * p + 1]]]) for p in range(N_LRU_BLOCKS // 2)])
    return jnp.concatenate([pair_diag(gate_a_w), pair_diag(gate_x_w)], axis=-1).astype(BF16)


def kernel(x_prompt, x_sample, cache_k, cache_v, state_conv, state_h, norm1_g, w_in, rel_bias, conv_w, conv_b,
           gate_a_w, gate_a_b, gate_x_w, gate_x_b, lru_lambda, att_out_g, rnn_out_g, w_out, norm2_g, w_mlp_in,
           w_mlp_out, final_g):
    depth = w_in.shape[0]
    assert depth == 1, "the final norm is fused into the single layer's dense kernel"
    b, s, _ = x_prompt.shape
    db, steps, _ = x_sample.shape
    n_buf = cache_k.shape[2]
    assert steps == SUBLANES and s == WIN_MAX and n_buf == WIN_MAX
    l = 0
    row = lambda v: v.reshape(1, -1).astype(F32)

    w_in_scaled = jnp.concatenate([w_in[l, :, :ATT_WIDTH] * (Q_SCALE * LOG2_E), w_in[l, :, ATT_WIDTH:]], axis=1)
    mix_w = (row(norm1_g[l]), w_in_scaled.astype(BF16), conv_w[l], row(conv_b[l]),
             _paired_gate_weights(0.5 * gate_a_w[l], 0.5 * gate_x_w[l]), 0.5 * row(gate_a_b[l]), 0.5 * row(gate_x_b[l]),
             row(lru_lambda[l]), row(rnn_out_g[l]))
    dense_w = (row(att_out_g[l]), w_out[l, :ATT_WIDTH].astype(BF16), w_out[l, ATT_WIDTH:].astype(BF16),
               row(norm2_g[l]), w_mlp_in[l].astype(BF16), w_mlp_out[l].astype(BF16), row(final_g))

    xs = x_sample.reshape(db * steps, D_MODEL)
    state_rows = jnp.pad(state_conv[l], ((0, 0), (0, SUBLANES - (CONV_WIDTH - 1)), (0, 0))).reshape(db * steps, LRU_WIDTH)
    qs, ks, vs, rnn_s, conv_s, h_s = _sample_mix(xs, state_rows, state_h[l], mix_w)
    cache_kt = jnp.transpose(cache_k[l], (0, 2, 3, 1))
    cache_vt = jnp.transpose(cache_v[l], (0, 2, 3, 1))

    q, k, v, rnn, conv_p, h_p = _prompt_mix(x_prompt, mix_w)
    att = _prompt_attn(q, k, v, _prompt_bias_rows(rel_bias))
    y_prompt, att_s = _dense_and_sample_attn(
        x_prompt.reshape(b * s, D_MODEL), att.reshape(b * s, ATT_WIDTH), rnn.reshape(b * s, LRU_WIDTH), dense_w,
        qs, ks, vs, cache_kt, cache_vt, *_sample_bias_tables(rel_bias, n_buf, steps), steps)
    y_prompt = y_prompt.reshape(b, s, D_MODEL)
    y_sample = _dense(xs, att_s, rnn_s, dense_w).reshape(db, steps, D_MODEL)

    kv_p = (1, b, s, N_HEADS, HEAD_DIM)
    kv_s = (1, db, steps, N_HEADS, HEAD_DIM)
    return (y_prompt, y_sample,
            k.reshape(kv_p), v.reshape(kv_p), conv_p[None], h_p.reshape(1, b, LRU_WIDTH),
            ks.reshape(kv_s), vs.reshape(kv_s), conv_s[None], h_s[None])
```
